```python
import math
import jax, jax.numpy as jnp
from jax import lax
import numpy as np

D_MODEL = 2048
BATCH = 4
SEQ = 2048
DEPTH = 4
DEC_BATCH = 8
DEC_SEQ = 4
PAST_LEN = 16384
PAGE_SIZE = 128

N_EVEN = (DEPTH + 1) // 2
N_ODD = DEPTH // 2
MIX_W = D_MODEL
A_HEAD_DIM = 128
A_W = MIX_W // 2
A_HEADS = A_W // A_HEAD_DIM
MOBA_BLOCK = 256
MOBA_TOPK = 3
MOBA_Q_CHUNK = 16
B_HEAD_DIM = 64
B_W = MIX_W // 2
B_HEADS = B_W // B_HEAD_DIM
LORA_W = 64
LORA_A = 64
B_COLS = 3 * B_W + LORA_W + LORA_A
RWKV_GN_EPS = 64e-5
GATE_EVEN = A_W + B_W
COLS_EVEN = 3 * A_W + B_COLS + GATE_EVEN
C_W = MIX_W
POOL_WINDOWS = (2, 4, 8, 16)
POOL_GROUPS = len(POOL_WINDOWS)
POOL_GW = C_W // POOL_GROUPS
POOL_BUF = max(POOL_WINDOWS) - 1
COLS_ODD = 2 * C_W
MEM_LEN = 256
X_HEADS = 4
X_HEAD_DIM = D_MODEL // X_HEADS
T5_BUCKETS = 32
T5_MAX_DIST = 128
LN_EPS = 1e-5
ALPHA = (2 * DEPTH) ** 0.25
BETA = (8 * DEPTH) ** -0.25

kernel_name = 'moba_rwkv7_pool_hybrid_step'


def layer_norm(x, g, b):
    xf = x.astype(jnp.float32)
    mu = jnp.mean(xf, axis=-1, keepdims=True)
    var = jnp.mean(jnp.square(xf - mu), axis=-1, keepdims=True)
    y = (xf - mu) * lax.rsqrt(var + LN_EPS) * g.astype(jnp.float32) + b.astype(jnp.float32)
    return y.astype(x.dtype)


def t5_bucket(rel):
    exact = T5_BUCKETS // 2
    rel = jnp.maximum(rel, 0)
    relf = jnp.maximum(rel, exact).astype(jnp.float32)
    large = exact + (jnp.log(relf / exact) / math.log(T5_MAX_DIST / exact) * (T5_BUCKETS - exact)).astype(jnp.int32)
    large = jnp.minimum(large, T5_BUCKETS - 1)
    return jnp.where(rel < exact, rel, large)


def moba_attend(q, k_all, v_all, pos0, t5_table):
    Bq, Tq, H, D = q.shape
    Lk = k_all.shape[1]
    nb = -(-Lk // MOBA_BLOCK)
    pad = nb * MOBA_BLOCK - Lk
    kb = jnp.pad(k_all, ((0, 0), (0, pad), (0, 0), (0, 0))).reshape(Bq, nb, MOBA_BLOCK, H, D).transpose(0, 3, 1, 2, 4)
    vb = jnp.pad(v_all, ((0, 0), (0, pad), (0, 0), (0, 0))).reshape(Bq, nb, MOBA_BLOCK, H, D).transpose(0, 3, 1, 2, 4)
    kmean = jnp.mean(kb.astype(jnp.float32), axis=3)
    k_sel = min(MOBA_TOPK, nb)
    qc = math.gcd(Tq, MOBA_Q_CHUNK)
    n_chunks = Tq // qc
    qh = jnp.moveaxis(q.transpose(0, 2, 1, 3).reshape(Bq, H, n_chunks, qc, D), 2, 0)
    bi = jnp.arange(Bq)[:, None, None, None]
    hi = jnp.arange(H)[None, :, None, None]
    offs = jnp.arange(MOBA_BLOCK, dtype=jnp.int32)
    scale = 1.0 / math.sqrt(D)
    bias_tab = t5_table.T.astype(jnp.float32)

    def chunk(args):
        c, qq = args
        pos = pos0 + c * qc + jnp.arange(qc, dtype=jnp.int32)
        own = pos // MOBA_BLOCK
        gate = jnp.einsum('bhqd,bhnd->bhqn', qq.astype(jnp.float32), kmean)
        past = jnp.arange(nb)[None, :] < own[:, None]
        gate = jnp.where(past, gate, -jnp.inf)
        _, top = lax.top_k(gate, k_sel)
        top_ok = top < own[:, None]
        own_b = jnp.broadcast_to(own[:, None], (Bq, H, qc, 1)).astype(top.dtype)
        idx = jnp.concatenate([top, own_b], axis=-1)
        ok = jnp.concatenate([top_ok, jnp.ones_like(own_b, dtype=bool)], axis=-1)
        kg = kb[bi, hi, idx]
        vg = vb[bi, hi, idx]
        keypos = idx[..., None] * MOBA_BLOCK + offs
        rel = pos[:, None, None] - keypos
        logits = jnp.einsum('bhqd,bhqjkd->bhqjk', qq, kg).astype(jnp.float32) * scale
        logits = logits + bias_tab[hi[..., None], t5_bucket(rel)]
        mask = ok[..., None] & (rel >= 0)
        logits = jnp.where(mask, logits, -1e30)
        p = jax.nn.softmax(logits.reshape(Bq, H, qc, -1), axis=-1).reshape(logits.shape)
        return jnp.einsum('bhqjk,bhqjkd->bhqd', p.astype(vg.dtype), vg)

    out = lax.map(chunk, (jnp.arange(n_chunks, dtype=jnp.int32), qh))
    out = jnp.moveaxis(out, 0, 2).reshape(Bq, H, Tq, D).transpose(0, 2, 1, 3)
    return out.reshape(Bq, Tq, H * D)


def rwkv_scan(r, decay, kk, a, k, v, S0):
    def step(S, inp):
        r_t, w_t, kk_t, a_t, k_t, v_t = inp
        sa = jnp.einsum('bhvk,bhk->bhv', S, -kk_t)
        S = S * w_t[:, :, None, :] + sa[..., None] * (kk_t * a_t)[:, :, None, :] + v_t[..., None] * k_t[:, :, None, :]
        return S, jnp.einsum('bhvk,bhk->bhv', S, r_t)
    seq = tuple(jnp.moveaxis(t, 1, 0) for t in (r, decay, kk, a, k, v))
    S_T, ys = lax.scan(step, S0.astype(jnp.float32), seq)
    return jnp.moveaxis(ys, 0, 1), S_T


def rwkv_mix(xb, shift_prev, S0, mu, w0, w_up, a0, a_up, k_k, k_a, r_k, gn_g, gn_b):
    Bq, T, _ = xb.shape
    prev = jnp.concatenate([shift_prev[:, None, :].astype(xb.dtype), xb[:, :-1]], axis=1)
    xm = (xb + (prev - xb) * mu).astype(jnp.float32)
    r, k, v, wd, ad = jnp.split(xm, [B_W, 2 * B_W, 3 * B_W, 3 * B_W + LORA_W], axis=-1)
    w_log = -jnp.exp(-jax.nn.softplus(-(w0 + jnp.tanh(wd) @ w_up)) - 0.5)
    a = jax.nn.sigmoid(a0 + ad @ a_up)
    hs = lambda t: t.reshape(Bq, T, B_HEADS, B_HEAD_DIM)
    kk = hs(k * k_k)
    kk = kk / jnp.maximum(jnp.sqrt(jnp.sum(jnp.square(kk), axis=-1, keepdims=True)), 1e-12)
    k = k * (1.0 + (a - 1.0) * k_a)
    r, k, v, a, decay = hs(r), hs(k), hs(v), hs(a), hs(jnp.exp(w_log))
    y, S_T = rwkv_scan(r, decay, kk, a, k, v, S0)
    m = jnp.mean(y, axis=-1, keepdims=True)
    var = jnp.mean(jnp.square(y - m), axis=-1, keepdims=True)
    y = ((y - m) * lax.rsqrt(var + RWKV_GN_EPS)).reshape(Bq, T, B_W) * gn_g + gn_b
    y = y + (jnp.sum(r * k * r_k, axis=-1, keepdims=True) * v).reshape(Bq, T, B_W)
    return y.astype(xb.dtype), xb[:, -1], S_T.astype(xb.dtype)


def even_mixer(x, past_k, past_v, shift_prev, S0, pos0, w_in, w_out, t5_table, rw):
    Bq, T, _ = x.shape
    h = x @ w_in
    q, k, v, xb, z = jnp.split(h, [A_W, 2 * A_W, 3 * A_W, 3 * A_W + B_COLS], axis=-1)
    q = q.reshape(Bq, T, A_HEADS, A_HEAD_DIM)
    k = k.reshape(Bq, T, A_HEADS, A_HEAD_DIM)
    v = v.reshape(Bq, T, A_HEADS, A_HEAD_DIM)
    if past_k is None:
        k_all, v_all = k, v
    else:
        k_all = jnp.concatenate([past_k.astype(k.dtype), k], axis=1)
        v_all = jnp.concatenate([past_v.astype(v.dtype), v], axis=1)
    a_out = moba_attend(q, k_all, v_all, pos0, t5_table)
    b_out, shift_new, S_new = rwkv_mix(xb, shift_prev, S0, *rw)
    y = jnp.concatenate([a_out.astype(x.dtype), b_out], axis=-1) * jax.nn.silu(z)
    return y @ w_out, k, v, shift_new, S_new


def pool_mix(xc, buf, pos0):
    Bq, T, C = xc.shape
    z = jnp.concatenate([buf.astype(jnp.float32), xc.astype(jnp.float32)], axis=1)
    cs = jnp.concatenate([jnp.zeros((Bq, 1, C), jnp.float32), jnp.cumsum(z, axis=1)], axis=1)
    pos = pos0 + jnp.arange(T, dtype=jnp.int32)
    outs = []
    for g, w in enumerate(POOL_WINDOWS):
        sl = slice(g * POOL_GW, (g + 1) * POOL_GW)
        s = cs[:, POOL_BUF + 1:POOL_BUF + T + 1, sl] - cs[:, POOL_BUF + 1 - w:POOL_BUF + T + 1 - w, sl]
        cnt = jnp.minimum(w, pos + 1).astype(jnp.float32)
        outs.append(s / cnt[None, :, None])
    mean = jnp.concatenate(outs, axis=-1)
    return (mean - xc.astype(jnp.float32)).astype(xc.dtype), z[:, -POOL_BUF:].astype(xc.dtype)


def odd_mixer(x, buf, pos0, w_in, group_w, scale, w_out):
    Bq, T, _ = x.shape
    h = x @ w_in
    xc, z = h[..., :C_W], h[..., C_W:]
    pooled, new_buf = pool_mix(xc, buf, pos0)
    y = jnp.einsum('btgc,gcd->btgd', pooled.reshape(Bq, T, POOL_GROUPS, POOL_GW), group_w).reshape(Bq, T, C_W) * scale
    return (y * jax.nn.silu(z)) @ w_out, new_buf


def cross_attn(x, mk, mv, w_q, w_o):
    Bq, T, _ = x.shape
    q = (x @ w_q).reshape(Bq, T, X_HEADS, X_HEAD_DIM)
    s = jnp.einsum('bthd,bmhd->bhtm', q, mk.astype(q.dtype)).astype(jnp.float32) / math.sqrt(X_HEAD_DIM)
    p = jax.nn.softmax(s, axis=-1).astype(x.dtype)
    o = jnp.einsum('bhtm,bmhd->bthd', p, mv.astype(x.dtype)).reshape(Bq, T, X_HEADS * X_HEAD_DIM)
    return o @ w_o


def setup_inputs(seed: int = 0) -> dict:
    key = jax.random.key(seed)
    keys = jax.random.split(key, 48)
    ctr = [0]

    def nk():
        k = keys[ctr[0]]
        ctr[0] += 1
        return k

    def nrm(shape, s):
        return jax.random.normal(nk(), shape, jnp.float32) * s

    def uni(shape):
        return jax.random.uniform(nk(), shape, jnp.float32)

    n_pages = PAST_LEN // PAGE_SIZE
    n_phys = (DEC_BATCH * n_pages * 5) // 4
    perm = jax.random.permutation(nk(), n_phys)
    page_table = perm[:DEC_BATCH * n_pages].reshape(DEC_BATCH, n_pages).astype(jnp.int32)
    return {
        'x_prompt': nrm((BATCH, SEQ, D_MODEL), 1.0),
        'x_sample': nrm((DEC_BATCH, DEC_SEQ, D_MODEL), 1.0),
        'cache_moba_k': nrm((N_EVEN, n_phys, PAGE_SIZE, A_HEADS, A_HEAD_DIM), 1.0),
        'cache_moba_v': nrm((N_EVEN, n_phys, PAGE_SIZE, A_HEADS, A_HEAD_DIM), 1.0),
        'page_table': page_table,
        'state_rwkv': nrm((N_EVEN, DEC_BATCH, B_HEADS, B_HEAD_DIM, B_HEAD_DIM), 0.1),
        'state_shift': nrm((N_EVEN, DEC_BATCH, B_COLS), 1.0),
        'state_pool': nrm((N_ODD, DEC_BATCH, POOL_BUF, C_W), 1.0),
        'cache_mem_k': nrm((DEPTH, DEC_BATCH, MEM_LEN, X_HEADS, X_HEAD_DIM), 1.0),
        'cache_mem_v': nrm((DEPTH, DEC_BATCH, MEM_LEN, X_HEADS, X_HEAD_DIM), 1.0),
        'mem_prompt': nrm((BATCH, MEM_LEN, D_MODEL), 1.0),
        'w_in_even': nrm((N_EVEN, D_MODEL, COLS_EVEN), D_MODEL ** -0.5),
        'w_out_even': nrm((N_EVEN, GATE_EVEN, D_MODEL), BETA * GATE_EVEN ** -0.5),
        'rwkv_mu': uni((N_EVEN, B_COLS)),
        'rwkv_w0': nrm((N_EVEN, B_W), 0.5),
        'rwkv_w_up': nrm((N_EVEN, LORA_W, B_W), 0.5 * LORA_W ** -0.5),
        'rwkv_a0': nrm((N_EVEN, B_W), 0.5),
        'rwkv_a_up': nrm((N_EVEN, LORA_A, B_W), 0.5 * LORA_A ** -0.5),
        'rwkv_k_k': 0.85 + nrm((N_EVEN, B_W), 0.05),
        'rwkv_k_a': 1.0 + nrm((N_EVEN, B_W), 0.05),
        'rwkv_r_k': nrm((N_EVEN, B_HEADS, B_HEAD_DIM), 0.1),
        'rwkv_gn_g': 1.0 + nrm((N_EVEN, B_W), 0.05),
        'rwkv_gn_b': nrm((N_EVEN, B_W), 0.02),
        't5_bias': nrm((T5_BUCKETS, A_HEADS), 0.5),
        'w_in_odd': nrm((N_ODD, D_MODEL, COLS_ODD), D_MODEL ** -0.5),
        'pool_w': nrm((N_ODD, POOL_GROUPS, POOL_GW, POOL_GW), POOL_GW ** -0.5),
        'pool_scale': 1.0 + nrm((N_ODD, C_W), 0.05),
        'w_out_odd': nrm((N_ODD, C_W, D_MODEL), BETA * C_W ** -0.5),
        'xattn_w_q': nrm((DEPTH, D_MODEL, X_HEADS * X_HEAD_DIM), D_MODEL ** -0.5),
        'xattn_w_k': nrm((DEPTH, D_MODEL, X_HEADS * X_HEAD_DIM), D_MODEL ** -0.5),
        'xattn_w_v': nrm((DEPTH, D_MODEL, X_HEADS * X_HEAD_DIM), D_MODEL ** -0.5),
        'xattn_w_o': nrm((DEPTH, X_HEADS * X_HEAD_DIM, D_MODEL), BETA * D_MODEL ** -0.5),
        'ln_mix_g': 1.0 + nrm((DEPTH, D_MODEL), 0.05),
        'ln_mix_b': nrm((DEPTH, D_MODEL), 0.02),
        'ln_x_g': 1.0 + nrm((DEPTH, D_MODEL), 0.05),
        'ln_x_b': nrm((DEPTH, D_MODEL), 0.02),
    }


def reference(x_prompt, x_sample, cache_moba_k, cache_moba_v, page_table, state_rwkv, state_shift, state_pool,
              cache_mem_k, cache_mem_v, mem_prompt, w_in_even, w_out_even, rwkv_mu, rwkv_w0, rwkv_w_up, rwkv_a0,
              rwkv_a_up, rwkv_k_k, rwkv_k_a, rwkv_r_k, rwkv_gn_g, rwkv_gn_b, t5_bias, w_in_odd, pool_w, pool_scale,
              w_out_odd, xattn_w_q, xattn_w_k, xattn_w_v, xattn_w_o, ln_mix_g, ln_mix_b, ln_x_g, ln_x_b):
    xp, xs = x_prompt, x_sample
    bp, bs = xp.shape[0], xs.shape[0]
    kp_l, vp_l, sp_l, shp_l, poolp_l, mkp_l, mvp_l = [], [], [], [], [], [], []
    ks_l, vs_l, ss_l, shs_l, pools_l = [], [], [], [], []
    for l in range(DEPTH):
        if l % 2 == 0:
            e = l // 2
            rw = (rwkv_mu[e], rwkv_w0[e], rwkv_w_up[e], rwkv_a0[e], rwkv_a_up[e], rwkv_k_k[e], rwkv_k_a[e],
                  rwkv_r_k[e], rwkv_gn_g[e], rwkv_gn_b[e])
            mp, kp, vp, shp, Sp = even_mixer(
                xp, None, None, jnp.zeros((bp, B_COLS), xp.dtype),
                jnp.zeros((bp, B_HEADS, B_HEAD_DIM, B_HEAD_DIM), jnp.float32),
                0, w_in_even[e], w_out_even[e], t5_bias, rw)
            past_k = cache_moba_k[e][page_table].reshape(bs, -1, A_HEADS, A_HEAD_DIM)
            past_v = cache_moba_v[e][page_table].reshape(bs, -1, A_HEADS, A_HEAD_DIM)
            ms, ks_, vs_, shs, Ss = even_mixer(
                xs, past_k, past_v, state_shift[e], state_rwkv[e], PAST_LEN,
                w_in_even[e], w_out_even[e], t5_bias, rw)
            kp_l.append(kp); vp_l.append(vp); sp_l.append(Sp); shp_l.append(shp)
            ks_l.append(ks_); vs_l.append(vs_); ss_l.append(Ss); shs_l.append(shs)
        else:
            o = l // 2
            mp, bufp = odd_mixer(xp, jnp.zeros((bp, POOL_BUF, C_W), xp.dtype), 0,
                                 w_in_odd[o], pool_w[o], pool_scale[o], w_out_odd[o])
            ms, bufs = odd_mixer(xs, state_pool[o], PAST_LEN,
                                 w_in_odd[o], pool_w[o], pool_scale[o], w_out_odd[o])
            poolp_l.append(bufp); pools_l.append(bufs)
        xp = layer_norm(ALPHA * xp + mp, ln_mix_g[l], ln_mix_b[l])
        xs = layer_norm(ALPHA * xs + ms, ln_mix_g[l], ln_mix_b[l])
        mk_p = (mem_prompt @ xattn_w_k[l]).reshape(bp, MEM_LEN, X_HEADS, X_HEAD_DIM)
        mv_p = (mem_prompt @ xattn_w_v[l]).reshape(bp, MEM_LEN, X_HEADS, X_HEAD_DIM)
        mkp_l.append(mk_p); mvp_l.append(mv_p)
        xp = layer_norm(ALPHA * xp + cross_attn(xp, mk_p, mv_p, xattn_w_q[l], xattn_w_o[l]), ln_x_g[l], ln_x_b[l])
        xs = layer_norm(ALPHA * xs + cross_attn(xs, cache_mem_k[l], cache_mem_v[l], xattn_w_q[l], xattn_w_o[l]),
                        ln_x_g[l], ln_x_b[l])
    new_moba_k_prompt = jnp.stack(kp_l)
    new_moba_v_prompt = jnp.stack(vp_l)
    new_rwkv_prompt = jnp.stack(sp_l)
    new_shift_prompt = jnp.stack(shp_l)
    new_pool_prompt = jnp.stack(poolp_l)
    new_mem_k_prompt = jnp.stack(mkp_l)
    new_mem_v_prompt = jnp.stack(mvp_l)
    new_moba_k_sample = jnp.stack(ks_l)
    new_moba_v_sample = jnp.stack(vs_l)
    new_rwkv_sample = jnp.stack(ss_l)
    new_shift_sample = jnp.stack(shs_l)
    new_pool_sample = jnp.stack(pools_l)
    return (xp, xs, new_moba_k_prompt, new_moba_v_prompt, new_rwkv_prompt, new_shift_prompt, new_pool_prompt,
            new_mem_k_prompt, new_mem_v_prompt, new_moba_k_sample, new_moba_v_sample, new_rwkv_sample,
            new_shift_sample, new_pool_sample)
```

```python
import functools
import math

import numpy as np
import jax
import jax.numpy as jnp
from jax import lax
from jax.experimental import pallas as pl
from jax.experimental.pallas import tpu as pltpu

f32, bf16, i32 = jnp.float32, jnp.bfloat16, jnp.int32

D_MODEL = 2048
DEPTH = 4
PAST_LEN = 16384
PAGE_SIZE = 128
A_HEAD_DIM = 128
A_W = 1024
A_HEADS = 8
MOBA_BLOCK = 256
MOBA_TOPK = 3
B_HEAD_DIM = 64
B_W = 1024
B_HEADS = 16
LORA_W = 64
B_COLS = 3 * B_W + 2 * LORA_W
RWKV_GN_EPS = 64e-5
C_W = 2048
POOL_WINDOWS = (2, 4, 8, 16)
POOL_GW = C_W // len(POOL_WINDOWS)
POOL_BUF = max(POOL_WINDOWS) - 1
MEM_LEN = 256
X_HEADS = 4
X_HEAD_DIM = D_MODEL // X_HEADS
T5_BUCKETS = 32
T5_MAX_DIST = 128
LN_EPS = 1e-5
ALPHA = (2 * DEPTH) ** 0.25

NEG = -1e30
LANES = 128
SUBLANES = 8
RWKV_CHUNK = 64
PAIR = 2 * B_HEAD_DIM
VMEM_LIMIT = 48 * 1024 * 1024

NT = (((1,), (1,)), ((), ()))
TN = (((0,), (0,)), ((), ()))


def _t5_thresholds():
    exact = T5_BUCKETS // 2
    rel = np.arange(0, 4 * T5_MAX_DIST)
    relf = np.maximum(rel, exact).astype(np.float32)
    large = exact + (np.log(relf / np.float32(exact)) / np.float32(math.log(T5_MAX_DIST / exact))
                     * np.float32(T5_BUCKETS - exact)).astype(np.int32)
    bucket = np.where(rel < exact, rel, np.minimum(large, T5_BUCKETS - 1))
    assert (np.diff(bucket) >= 0).all() and bucket[-1] == T5_BUCKETS - 1
    return [int(np.argmax(bucket >= b)) for b in range(T5_BUCKETS)]


T5_THR = _t5_thresholds()


def _dot(a, b, dn=None):
    if dn is None:
        return jnp.dot(a, b, preferred_element_type=f32)
    return lax.dot_general(a, b, dn, preferred_element_type=f32)


def _split(x):
    hi = x.astype(bf16)
    lo = (x - hi.astype(f32)).astype(bf16)
    return hi, lo


def _dot3(a, b, dn=None):
    ah, al = _split(a)
    bh, bl = _split(b)
    return _dot(ah, bh, dn) + _dot(ah, bl, dn) + _dot(al, bh, dn)


def _dot2l(a, b_exact, dn=None):
    ah, al = _split(a)
    return _dot(ah, b_exact, dn) + _dot(al, b_exact, dn)


def _t5_bias_of(rel, tab):
    bias = jnp.full(rel.shape, tab(0), f32)
    for b in range(1, T5_BUCKETS):
        bias = jnp.where(rel >= T5_THR[b], tab(b), bias)
    return bias


def _params(*sem):
    return pltpu.CompilerParams(dimension_semantics=sem, vmem_limit_bytes=VMEM_LIMIT)


def _mm_body(x_ref, w_ref, o_ref, xb_ref):
    @pl.when(pl.program_id(1) == 0)
    def _():
        xb_ref[...] = x_ref[...].astype(bf16)

    o_ref[...] = _dot(xb_ref[...], w_ref[...])


def _matmul(x, w, tn=512):
    m, k = x.shape
    n = w.shape[1]
    tm = min(m, 1024)
    return pl.pallas_call(
        _mm_body,
        grid=(m // tm, n // tn),
        in_specs=[pl.BlockSpec((tm, k), lambda i, j: (i, 0)), pl.BlockSpec((k, tn), lambda i, j: (0, j))],
        out_specs=pl.BlockSpec((tm, tn), lambda i, j: (i, j)),
        out_shape=jax.ShapeDtypeStruct((m, n), f32),
        scratch_shapes=[pltpu.VMEM((tm, k), bf16)],
        compiler_params=_params("parallel", "arbitrary"),
        name="proj_matmul",
    )(x, w)


def _proj_ln_body(n_lhs, gated, *refs):
    lhs = [r[...] for r in refs[:n_lhs]]
    refs = refs[n_lhs:]
    y = lhs[0] if n_lhs == 1 else jnp.concatenate(lhs, axis=1)
    if gated:
        z = refs[0][...]
        refs = refs[1:]
        y = y * (z * jax.nn.sigmoid(z))
    w_ref, res_ref, g_ref, b_ref, o_ref = refs
    u = ALPHA * res_ref[...] + _dot(y.astype(bf16), w_ref[...])
    mu = jnp.mean(u, axis=-1, keepdims=True)
    d = u - mu
    var = jnp.mean(d * d, axis=-1, keepdims=True)
    o_ref[...] = d * lax.rsqrt(var + LN_EPS) * g_ref[...] + b_ref[...]


def _proj_ln(lhs, z, w, res, g, b):
    m = res.shape[0]
    tm = min(m, 256)
    row = lambda width: pl.BlockSpec((tm, width), lambda i: (i, 0))
    full = lambda a: pl.BlockSpec(a.shape, lambda i: (0, 0))
    args = list(lhs) + ([z] if z is not None else []) + [w, res, g, b]
    specs = [row(a.shape[1]) for a in lhs] + ([row(z.shape[1])] if z is not None else []) + [
        full(w), row(D_MODEL), full(g), full(b)]
    return pl.pallas_call(
        functools.partial(_proj_ln_body, len(lhs), z is not None),
        grid=(m // tm,),
        in_specs=specs,
        out_specs=row(D_MODEL),
        out_shape=jax.ShapeDtypeStruct((m, D_MODEL), f32),
        compiler_params=_params("parallel"),
        name="proj_ln",
    )(*args)


def _t5_tiles_body(t5_ref, o_ref):
    h = pl.program_id(0)
    key = lax.broadcasted_iota(i32, (MOBA_BLOCK, MOBA_BLOCK), 0)
    qry = lax.broadcasted_iota(i32, (MOBA_BLOCK, MOBA_BLOCK), 1)
    tab = lambda b: t5_ref[h, b]
    rel = qry - key
    o_ref[0, 0] = jnp.where(rel >= 0, _t5_bias_of(jnp.maximum(rel, 0), tab), NEG)
    o_ref[0, 1] = _t5_bias_of(rel + MOBA_BLOCK, tab)


def _t5_tiles(t5_t):
    return pl.pallas_call(
        _t5_tiles_body,
        grid=(A_HEADS,),
        in_specs=[pl.BlockSpec(memory_space=pltpu.SMEM)],
        out_specs=pl.BlockSpec((1, 2, MOBA_BLOCK, MOBA_BLOCK), lambda h: (h, 0, 0, 0)),
        out_shape=jax.ShapeDtypeStruct((A_HEADS, 2, MOBA_BLOCK, MOBA_BLOCK), f32),
        compiler_params=_params("parallel"),
        name="t5_tiles",
    )(t5_t)


def _moba_prompt_body(nb, t5_ref, q_ref, k_ref, v_ref, bias_ref, o_ref, vt_ref, km_ref, sel_ref, m_ref, l_ref, acc_ref):
    h = pl.program_id(1)
    qi = pl.program_id(2)

    @pl.when(qi == 0)
    def _():
        for n in range(nb):
            vt_ref[n] = v_ref[n].T.astype(bf16)
        km_ref[...] = jnp.concatenate(
            [jnp.mean(k_ref[n], axis=0, keepdims=True) for n in range(nb)], axis=0)

    q = q_ref[...]
    gate = _dot3(km_ref[...], q, NT)
    blk = lax.broadcasted_iota(i32, gate.shape, 0)
    beaten = jnp.zeros(gate.shape, f32)
    for m in range(nb):
        gm = gate[m:m + 1, :]
        wins = (gm > gate) | ((gm == gate) & (m < blk))
        beaten = beaten + jnp.where(wins, 1.0, 0.0) * jnp.where(m < qi, 1.0, 0.0)
    sel_ref[...] = jnp.where((blk < qi) & (beaten < MOBA_TOPK), 0.0, NEG)

    qs = (q * (1.0 / math.sqrt(A_HEAD_DIM))).astype(bf16)

    def scores(n):
        return _dot(k_ref[n].astype(bf16), qs, NT)

    s = scores(qi) + bias_ref[0, 0]
    m0 = jnp.max(s, axis=0, keepdims=True)
    p = jnp.exp(s - m0)
    m_ref[...] = m0
    l_ref[...] = jnp.sum(p, axis=0, keepdims=True)
    acc_ref[...] = _dot(vt_ref[qi], p.astype(bf16))

    far_bias = t5_ref[h, T5_BUCKETS - 1]
    for d in range(1, nb):
        @pl.when(qi >= d)
        def _():
            n = qi - d
            s = scores(n) + sel_ref[pl.ds(n, 1), :]
            s = s + (bias_ref[0, 1] if d == 1 else far_bias)
            m_old = m_ref[...]
            m_new = jnp.maximum(m_old, jnp.max(s, axis=0, keepdims=True))
            alpha = jnp.exp(m_old - m_new)
            p = jnp.exp(s - m_new)
            m_ref[...] = m_new
            l_ref[...] = alpha * l_ref[...] + jnp.sum(p, axis=0, keepdims=True)
            acc_ref[...] = alpha * acc_ref[...] + _dot(vt_ref[n], p.astype(bf16))

    o_ref[...] = (acc_ref[...] / l_ref[...]).T


def _moba_prompt(q, k, v, bias_tiles, t5_t, batch, seq):
    assert 2 * MOBA_BLOCK - (MOBA_BLOCK - 1) >= T5_THR[-1]
    nb = seq // MOBA_BLOCK
    k3 = k.reshape(batch * nb, MOBA_BLOCK, A_W)
    v3 = v.reshape(batch * nb, MOBA_BLOCK, A_W)
    qspec = pl.BlockSpec((MOBA_BLOCK, A_HEAD_DIM), lambda b, h, i: (b * nb + i, h))
    kvspec = pl.BlockSpec((nb, MOBA_BLOCK, A_HEAD_DIM), lambda b, h, i: (b, 0, h))
    return pl.pallas_call(
        functools.partial(_moba_prompt_body, nb),
        grid=(batch, A_HEADS, nb),
        in_specs=[pl.BlockSpec(memory_space=pltpu.SMEM), qspec, kvspec, kvspec,
                  pl.BlockSpec((1, 2, MOBA_BLOCK, MOBA_BLOCK), lambda b, h, i: (h, 0, 0, 0))],
        out_specs=qspec,
        out_shape=jax.ShapeDtypeStruct((batch * seq, A_W), f32),
        scratch_shapes=[pltpu.VMEM((nb, A_HEAD_DIM, MOBA_BLOCK), bf16), pltpu.VMEM((nb, A_HEAD_DIM), f32),
                        pltpu.VMEM((nb, MOBA_BLOCK), f32), pltpu.VMEM((1, MOBA_BLOCK), f32),
                        pltpu.VMEM((1, MOBA_BLOCK), f32), pltpu.VMEM((A_HEAD_DIM, MOBA_BLOCK), f32)],
        compiler_params=_params("parallel", "parallel", "arbitrary"),
        name="moba_prompt",
    )(t5_t, q, k3, v3, bias_tiles)


def _kmean_body(pt_ref, k0_ref, k1_ref, o_ref):
    s = jnp.sum(k0_ref[0], axis=0, keepdims=True) + jnp.sum(k1_ref[0], axis=0, keepdims=True)
    o_ref[0, 0] = s * (1.0 / MOBA_BLOCK)


def _moba_kmean(pt, cache_k, batch):
    pages_per_seq = PAST_LEN // PAGE_SIZE
    nblk = PAST_LEN // MOBA_BLOCK
    assert MOBA_BLOCK == 2 * PAGE_SIZE
    page = lambda half: pl.BlockSpec((1, PAGE_SIZE, A_W),
                                     lambda b, n, pt: (pt[b * pages_per_seq + 2 * n + half], 0, 0))
    return pl.pallas_call(
        _kmean_body,
        grid_spec=pltpu.PrefetchScalarGridSpec(
            num_scalar_prefetch=1, grid=(batch, nblk),
            in_specs=[page(0), page(1)],
            out_specs=pl.BlockSpec((1, 1, 1, A_W), lambda b, n, pt: (b, n, 0, 0))),
        out_shape=jax.ShapeDtypeStruct((batch, nblk, 1, A_W), f32),
        compiler_params=_params("parallel", "arbitrary"),
        name="moba_kmean",
    )(pt, cache_k, cache_k)


def _moba_topk_body(q_ref, km_ref, o_ref):
    nblk = km_ref.shape[1]
    lane = lax.broadcasted_iota(i32, (SUBLANES, LANES), 1)
    for h in range(A_HEADS):
        cols = slice(h * A_HEAD_DIM, (h + 1) * A_HEAD_DIM)
        gate = _dot3(q_ref[0, :, cols], km_ref[0, :, cols], NT)
        blk = lax.broadcasted_iota(i32, gate.shape, 1)
        out = jnp.zeros((SUBLANES, LANES), i32)
        for j in range(MOBA_TOPK):
            best = jnp.max(gate, axis=1, keepdims=True)
            idx = jnp.min(jnp.where(gate == best, blk, nblk), axis=1, keepdims=True)
            out = jnp.where(lane == j, idx, out)
            gate = jnp.where(blk == idx, -jnp.inf, gate)
        o_ref[0, h * SUBLANES:(h + 1) * SUBLANES, :] = out


def _moba_topk(q8, kmean, batch):
    nblk = kmean.shape[1]
    return pl.pallas_call(
        _moba_topk_body,
        grid=(batch,),
        in_specs=[pl.BlockSpec((1, SUBLANES, A_W), lambda b: (b, 0, 0)),
                  pl.BlockSpec((1, nblk, A_W), lambda b: (b, 0, 0))],
        out_specs=pl.BlockSpec((1, A_HEADS * SUBLANES, LANES), lambda b: (b, 0, 0)),
        out_shape=jax.ShapeDtypeStruct((batch, A_HEADS * SUBLANES, LANES), i32),
        compiler_params=_params("parallel"),
        name="moba_topk",
    )(q8, kmean)


def _moba_sample_body(t_new, pt_ref, top_ref, t5_ref, q_ref, kn_ref, vn_ref, kc_ref, vc_ref, o_ref,
                      m_ref, l_ref, acc_ref):
    b, h, s = pl.program_id(0), pl.program_id(1), pl.program_id(2)
    steps_per_q = 2 * MOBA_TOPK
    qq = s // steps_per_q
    j = (s % steps_per_q) // 2
    half = s % 2
    tab = lambda bkt: t5_ref[h, bkt]
    q = q_ref[0] * (1.0 / math.sqrt(A_HEAD_DIM))
    row = lax.broadcasted_iota(i32, (SUBLANES, 1), 0)

    @pl.when(s == 0)
    def _():
        kn, vn = kn_ref[0], vn_ref[0]
        lg = []
        for t in range(t_new):
            sc = jnp.sum(q * kn[t:t + 1, :], axis=1, keepdims=True) + _t5_bias_of(jnp.maximum(row - t, 0), tab)
            lg.append(jnp.where(row >= t, sc, NEG))
        m0 = functools.reduce(jnp.maximum, lg)
        ps = [jnp.exp(x - m0) for x in lg]
        m_ref[...] = m0
        l_ref[...] = functools.reduce(jnp.add, ps)
        acc_ref[...] = functools.reduce(jnp.add, [ps[t] * vn[t:t + 1, :] for t in range(t_new)])

    blk = top_ref[((b * A_HEADS + h) * t_new + qq) * MOBA_TOPK + j]
    key0 = blk * MOBA_BLOCK + half * PAGE_SIZE
    keypos = key0 + lax.broadcasted_iota(i32, (SUBLANES, PAGE_SIZE), 1)
    qpos = PAST_LEN + lax.broadcasted_iota(i32, (SUBLANES, PAGE_SIZE), 0)
    lg = _dot(q.astype(bf16), kc_ref[0].astype(bf16), NT) + _t5_bias_of(qpos - keypos, tab)
    mine = row == qq
    m_old = m_ref[...]
    m_new = jnp.where(mine, jnp.maximum(m_old, jnp.max(lg, axis=1, keepdims=True)), m_old)
    alpha = jnp.exp(m_old - m_new)
    p = jnp.where(mine, jnp.exp(lg - m_new), 0.0)
    m_ref[...] = m_new
    l_ref[...] = alpha * l_ref[...] + jnp.sum(p, axis=1, keepdims=True)
    acc_ref[...] = alpha * acc_ref[...] + _dot(p.astype(bf16), vc_ref[0].astype(bf16))

    @pl.when(s == pl.num_programs(2) - 1)
    def _():
        o_ref[0] = acc_ref[...] / l_ref[...]


def _moba_sample(pt, top, t5_t, q8, kn8, vn8, cache_k, cache_v, batch, t_new):
    pages_per_seq = PAST_LEN // PAGE_SIZE
    steps = t_new * MOBA_TOPK * 2

    def page_map(b, h, s, pt, top):
        blk = top[((b * A_HEADS + h) * t_new + s // (2 * MOBA_TOPK)) * MOBA_TOPK + (s % (2 * MOBA_TOPK)) // 2]
        return (pt[b * pages_per_seq + 2 * blk + s % 2], 0, h)

    rows = pl.BlockSpec((1, SUBLANES, A_HEAD_DIM), lambda b, h, s, pt, top: (b, 0, h))
    page = pl.BlockSpec((1, PAGE_SIZE, A_HEAD_DIM), page_map)
    return pl.pallas_call(
        functools.partial(_moba_sample_body, t_new),
        grid_spec=pltpu.PrefetchScalarGridSpec(
            num_scalar_prefetch=2, grid=(batch, A_HEADS, steps),
            in_specs=[pl.BlockSpec(memory_space=pltpu.SMEM), rows, rows, rows, page, page],
            out_specs=rows,
            scratch_shapes=[pltpu.VMEM((SUBLANES, 1), f32), pltpu.VMEM((SUBLANES, 1), f32),
                            pltpu.VMEM((SUBLANES, A_HEAD_DIM), f32)]),
        out_shape=jax.ShapeDtypeStruct((batch, SUBLANES, A_W), f32),
        compiler_params=_params("parallel", "parallel", "arbitrary"),
        name="moba_sample",
    )(pt, top, t5_t, q8, kn8, vn8, cache_k, cache_v)


def _rwkv_prep_body(t_valid, t_total, xb_ref, halo_ref, sp_ref, mu_ref, w0_ref, a0_ref, kk_ref, ka_ref, lora_ref, ones_ref,
                    r_ref, l_ref, a_ref, b_ref, k_ref, v_ref):
    i = pl.program_id(1)
    x = xb_ref[0]
    tt = x.shape[0]
    row = lax.broadcasted_iota(i32, (tt, 1), 0)
    first = jnp.where(i == 0, sp_ref[0], halo_ref[0, SUBLANES - 1:SUBLANES, :])
    prev = jnp.where(row == 0, first, pltpu.roll(x, 1, 0))
    xm = x + (prev - x) * mu_ref[...]
    r, k, v = xm[:, :B_W], xm[:, B_W:2 * B_W], xm[:, 2 * B_W:3 * B_W]
    wa = xm[:, 3 * B_W:]
    lane = lax.broadcasted_iota(i32, wa.shape, 1)
    lw = _dot3(jnp.where(lane < LORA_W, jnp.tanh(wa), 0.0), lora_ref[...])
    la = _dot3(jnp.where(lane < LORA_W, 0.0, wa), lora_ref[...])
    wlog = -math.exp(-0.5) * jax.nn.sigmoid(w0_ref[...] + lw)
    a = jax.nn.sigmoid(a0_ref[...] + la)
    kk = k * kk_ref[...]
    ss = _dot2l(kk * kk, ones_ref[...])
    kk = kk / jnp.maximum(jnp.sqrt(ss), 1e-12)
    kp = k * (1.0 + (a - 1.0) * ka_ref[...])
    outs = (r, wlog, -kk, kk * a, kp, v)
    if t_valid < t_total:
        ok = (i * tt + row) < t_valid
        outs = tuple(jnp.where(ok, o, 0.0) for o in outs)
    for ref, o in zip((r_ref, l_ref, a_ref, b_ref, k_ref, v_ref), outs):
        ref[0] = o


def _rwkv_prep(xb3, shift_prev, t_valid, mu, w0, a0, k_k, k_a, lora, ones_bd):
    bn, tp, _ = xb3.shape
    tt = min(tp, 256)
    halo_blocks = tt // SUBLANES
    tile = lambda width: pl.BlockSpec((1, tt, width), lambda b, i: (b, i, 0))
    vec = lambda a: pl.BlockSpec(a.shape, lambda b, i: (0,) * a.ndim)
    out = jax.ShapeDtypeStruct((bn, tp, B_W), f32)
    return pl.pallas_call(
        functools.partial(_rwkv_prep_body, t_valid, tp),
        grid=(bn, tp // tt),
        in_specs=[tile(B_COLS),
                  pl.BlockSpec((1, SUBLANES, B_COLS), lambda b, i: (b, jnp.maximum(i * halo_blocks - 1, 0), 0)),
                  pl.BlockSpec((1, 1, B_COLS), lambda b, i: (b, 0, 0)),
                  vec(mu), vec(w0), vec(a0), vec(k_k), vec(k_a), vec(lora), vec(ones_bd)],
        out_specs=[tile(B_W)] * 6,
        out_shape=[out] * 6,
        compiler_params=_params("parallel", "arbitrary"),
        name="rwkv_prep",
    )(xb3, xb3, shift_prev, mu, w0, a0, k_k, k_a, lora, ones_bd)


def _stack(p, lo):
    return jnp.concatenate([jnp.where(lo, p, 0.0), jnp.where(lo, 0.0, p)], axis=0)


def _rwkv_chunk(R, L, A, B, K, V, BD):
    c = RWKV_CHUNK
    lane = lax.broadcasted_iota(i32, (c, PAIR), 1)
    lo = lane < B_HEAD_DIM
    t = lax.broadcasted_iota(i32, (c, PAIR), 0)
    j = lane & (B_HEAD_DIM - 1)
    strict, incl = t > j, t >= j
    tri = jnp.where(lax.broadcasted_iota(i32, (c, c), 0) >= lax.broadcasted_iota(i32, (c, c), 1), 1.0, 0.0).astype(bf16)
    l1, l2 = _split(L)
    l3 = (L - l1.astype(f32) - l2.astype(f32)).astype(bf16)
    cum = _dot(tri, l1) + _dot(tri, l2) + _dot(tri, l3)
    last = cum[c - 1:c, :]
    At, Rt = A * jnp.exp(cum - L), R * jnp.exp(cum)
    einv, ew = jnp.exp(-cum), jnp.exp(last - cum)
    Bt, Kt, Bw, Kw = B * einv, K * einv, B * ew, K * ew
    st = lambda p: _stack(p, lo).astype(bf16)
    ar = jnp.concatenate([At, Rt], axis=0).astype(bf16)
    pb = _dot(ar, st(Bt), NT)
    pk = _dot(ar, st(Kt), NT)
    pab, prb = jnp.where(strict, pb[:c], 0.0), jnp.where(incl, pb[c:], 0.0)
    pak, prk = jnp.where(strict, pk[:c], 0.0), jnp.where(incl, pk[c:], 0.0)
    x = jnp.where(t == j, 1.0, 0.0) + pab
    qp = pab
    for _ in range(int(math.log2(c)) - 1):
        qp = _dot(qp.astype(bf16), st(qp))
        x = x + _dot(x.astype(bf16), st(qp))
    xb = x.astype(bf16)
    akv = _dot(pak.astype(bf16), st(V))
    ua = _dot(xb, jnp.concatenate([st(akv), st(At)], axis=1))
    u0, ah = ua[:, :PAIR], ua[:, PAIR:]
    ry = _dot(prb.astype(bf16), jnp.concatenate([st(ah), st(u0)], axis=1))
    rh = Rt + ry[:, :PAIR]
    y0 = ry[:, PAIR:] + _dot(prk.astype(bf16), st(V))
    z1 = _dot(Bw.astype(bf16), jnp.concatenate([ah, u0], axis=1).astype(bf16), TN)
    z2 = _dot(Kw.astype(bf16), V.astype(bf16), TN)
    rowi = lax.broadcasted_iota(i32, (PAIR, PAIR), 0)
    coli = lax.broadcasted_iota(i32, (PAIR, PAIR), 1)
    same_head = (rowi < B_HEAD_DIM) == (coli < B_HEAD_DIM)
    wc = jnp.exp(last)
    mt = jnp.where(rowi == coli, wc, 0.0) + jnp.where(same_head, z1[:, :PAIR], 0.0)
    gt = jnp.where(same_head, z1[:, PAIR:] + z2, 0.0)
    y = _dot3(rh, BD) + y0
    return y, _dot3(mt, BD) + gt


def _rwkv_scan_body(npair, r_ref, l_ref, a_ref, b_ref, k_ref, v_ref, bd0_ref, rk_ref, gg_ref, gb_ref, y_ref, bd_ref):
    @pl.when(pl.program_id(2) == 0)
    def _():
        bd_ref[...] = bd0_ref[...]

    rowi = lax.broadcasted_iota(i32, (PAIR, PAIR), 0)
    coli = lax.broadcasted_iota(i32, (PAIR, PAIR), 1)
    same_head = (rowi < B_HEAD_DIM) == (coli < B_HEAD_DIM)
    head_sum = jnp.where(same_head, 1.0, 0.0).astype(bf16)
    head_mean = jnp.where(same_head, 1.0 / B_HEAD_DIM, 0.0).astype(bf16)
    for p in range(npair):
        cols = slice(p * PAIR, (p + 1) * PAIR)
        R, K, V = r_ref[0, :, cols], k_ref[0, :, cols], v_ref[0, :, cols]
        y, bd = _rwkv_chunk(R, l_ref[0, :, cols], a_ref[0, :, cols], b_ref[0, :, cols], K, V, bd_ref[0, p])
        bd_ref[0, p] = bd
        d = y - _dot2l(y, head_mean)
        var = _dot2l(d * d, head_mean)
        yn = d * lax.rsqrt(var + RWKV_GN_EPS) * gg_ref[:, cols] + gb_ref[:, cols]
        y_ref[0, :, cols] = yn + _dot2l(R * K * rk_ref[:, cols], head_sum) * V


def _rwkv_scan(prep, bd0, r_k, gn_g, gn_b, npair=2):
    bn, tp, _ = prep[0].shape
    npg = B_W // (PAIR * npair)
    tile = pl.BlockSpec((1, RWKV_CHUNK, PAIR * npair), lambda b, g, c: (b, c, g))
    state = pl.BlockSpec((1, npair, PAIR, PAIR), lambda b, g, c: (b, g, 0, 0))
    vec = pl.BlockSpec((1, PAIR * npair), lambda b, g, c: (0, g))
    return pl.pallas_call(
        functools.partial(_rwkv_scan_body, npair),
        grid=(bn, npg, tp // RWKV_CHUNK),
        in_specs=[tile] * 6 + [state, vec, vec, vec],
        out_specs=[tile, state],
        out_shape=[jax.ShapeDtypeStruct((bn, tp, B_W), f32), jax.ShapeDtypeStruct(bd0.shape, f32)],
        compiler_params=_params("parallel", "parallel", "arbitrary"),
        name="rwkv_scan",
    )(*prep, bd0, r_k, gn_g, gn_b)


def _state_to_bd(s):
    bn = s.shape[0]
    st = jnp.swapaxes(s, -1, -2).reshape(bn, B_HEADS // 2, 2, B_HEAD_DIM, B_HEAD_DIM)
    return jnp.einsum('bphkv,hg->bphkgv', st, jnp.eye(2, dtype=s.dtype)).reshape(bn, B_HEADS // 2, PAIR, PAIR)


def _bd_to_state(bd):
    bn = bd.shape[0]
    bd6 = bd.reshape(bn, B_HEADS // 2, 2, B_HEAD_DIM, 2, B_HEAD_DIM)
    return jnp.einsum('bphkhv->bphvk', bd6).reshape(bn, B_HEADS, B_HEAD_DIM, B_HEAD_DIM)


def _pool_body(pos0, x_ref, halo_ref, buf_ref, pw_ref, sc_ref, o_ref):
    i = pl.program_id(1)
    x = x_ref[0]
    tt = x.shape[0]
    halo = jnp.where(i == 0, buf_ref[0], halo_ref[0])
    xe = jnp.concatenate([halo, x], axis=0)
    row = lax.broadcasted_iota(i32, (tt, 1), 0)
    pos = pos0 + i * tt + row
    outs = []
    for g, w in enumerate(POOL_WINDOWS):
        cols = slice(g * POOL_GW, (g + 1) * POOL_GW)
        s = xe[:, cols]
        sh = 1
        while sh < w:
            s = s + pltpu.roll(s, sh, 0)
            sh *= 2
        cnt = jnp.minimum(w, pos + 1).astype(f32)
        pooled = s[POOL_BUF + 1:] / cnt - x[:, cols]
        outs.append(_dot(pooled.astype(bf16), pw_ref[g]))
    o_ref[0] = jnp.concatenate(outs, axis=1) * sc_ref[...]


def _pool(xc3, buf16, pos0, pool_w, scale):
    bn, tp, _ = xc3.shape
    tt = min(tp, 256)
    hb = POOL_BUF + 1
    return pl.pallas_call(
        functools.partial(_pool_body, pos0),
        grid=(bn, tp // tt),
        in_specs=[pl.BlockSpec((1, tt, C_W), lambda b, i: (b, i, 0)),
                  pl.BlockSpec((1, hb, C_W), lambda b, i: (b, jnp.maximum(i * (tt // hb) - 1, 0), 0)),
                  pl.BlockSpec((1, hb, C_W), lambda b, i: (b, 0, 0)),
                  pl.BlockSpec(pool_w.shape, lambda b, i: (0, 0, 0)),
                  pl.BlockSpec(scale.shape, lambda b, i: (0, 0))],
        out_specs=pl.BlockSpec((1, tt, C_W), lambda b, i: (b, i, 0)),
        out_shape=jax.ShapeDtypeStruct((bn, tp, C_W), f32),
        compiler_params=_params("parallel", "arbitrary"),
        name="pool_mix",
    )(xc3, xc3, buf16, pool_w, scale)


def _xattn_body(q_ref, k_ref, v_ref, o_ref):
    q = (q_ref[...] * (1.0 / math.sqrt(X_HEAD_DIM))).astype(bf16)
    s = _dot(q, k_ref[0].astype(bf16), NT)
    p = jnp.exp(s - jnp.max(s, axis=-1, keepdims=True))
    o_ref[...] = _dot(p.astype(bf16), v_ref[0].astype(bf16)) / jnp.sum(p, axis=-1, keepdims=True)


def _xattn(q, mk, mv, batch, tq):
    rows = q.shape[0] // batch
    nt = rows // tq
    qspec = pl.BlockSpec((tq, X_HEAD_DIM), lambda b, i, h: (b * nt + i, h))
    mspec = pl.BlockSpec((1, MEM_LEN, X_HEAD_DIM), lambda b, i, h: (b, 0, h))
    return pl.pallas_call(
        _xattn_body,
        grid=(batch, nt, X_HEADS),
        in_specs=[qspec, mspec, mspec],
        out_specs=qspec,
        out_shape=jax.ShapeDtypeStruct(q.shape, f32),
        compiler_params=_params("parallel", "parallel", "parallel"),
        name="mem_xattn",
    )(q, mk, mv)


def _pad_rows(x3, rows):
    return jnp.pad(x3, ((0, 0), (0, rows - x3.shape[1]), (0, 0)))


def kernel(x_prompt, x_sample, cache_moba_k, cache_moba_v, page_table, state_rwkv, state_shift, state_pool, cache_mem_k, cache_mem_v, mem_prompt, w_in_even, w_out_even, rwkv_mu, rwkv_w0, rwkv_w_up, rwkv_a0, rwkv_a_up, rwkv_k_k, rwkv_k_a, rwkv_r_k, rwkv_gn_g, rwkv_gn_b, t5_bias, w_in_odd, pool_w, pool_scale, w_out_odd, xattn_w_q, xattn_w_k, xattn_w_v, xattn_w_o, ln_mix_g, ln_mix_b, ln_x_g, ln_x_b):
    bp, tp, _ = x_prompt.shape
    bs, ts, _ = x_sample.shape
    xp = x_prompt.reshape(bp * tp, D_MODEL)
    xs = x_sample.reshape(bs * ts, D_MODEL)
    mem = mem_prompt.reshape(bp * MEM_LEN, D_MODEL)
    pt = page_table.reshape(-1)
    t5_t = t5_bias.T
    bias_tiles = _t5_tiles(t5_t)
    head_ones = jnp.asarray(np.kron(np.eye(B_HEADS), np.ones((B_HEAD_DIM, B_HEAD_DIM))), bf16)
    row2 = lambda a: a.reshape(1, -1)
    cast = lambda a: a.astype(bf16)

    kp_l, vp_l, sp_l, shp_l, poolp_l, mkp_l, mvp_l = [], [], [], [], [], [], []
    ks_l, vs_l, ss_l, shs_l, pools_l = [], [], [], [], []
    for l in range(DEPTH):
        if l % 2 == 0:
            e = l // 2
            w_in = w_in_even[e]
            wq, wk, wv = cast(w_in[:, :A_W]), cast(w_in[:, A_W:2 * A_W]), cast(w_in[:, 2 * A_W:3 * A_W])
            wxb, wz = cast(w_in[:, 3 * A_W:3 * A_W + B_COLS]), cast(w_in[:, 3 * A_W + B_COLS:])
            w_out = cast(w_out_even[e])
            lora = jnp.concatenate([rwkv_w_up[e], rwkv_a_up[e]], axis=0)
            rw = (row2(rwkv_mu[e]), row2(rwkv_w0[e]), row2(rwkv_a0[e]), row2(rwkv_k_k[e]), row2(rwkv_k_a[e]),
                  lora, head_ones)
            gn = (row2(rwkv_r_k[e]), row2(rwkv_gn_g[e]), row2(rwkv_gn_b[e]))

            def rwkv(xb, batch, t_real, t_pad, shift_prev, s0):
                xb3 = xb.reshape(batch, t_real, B_COLS)
                if t_pad != t_real:
                    xb3 = _pad_rows(xb3, t_pad)
                prep = _rwkv_prep(xb3, shift_prev, t_real, *rw)
                y, bd = _rwkv_scan(prep, _state_to_bd(s0), *gn)
                return y[:, :t_real].reshape(batch * t_real, B_W), xb3[:, t_real - 1], _bd_to_state(bd)

            q, k, v = _matmul(xp, wq), _matmul(xp, wk), _matmul(xp, wv)
            xb, z = _matmul(xp, wxb, tn=640), _matmul(xp, wz)
            a_out = _moba_prompt(q, k, v, bias_tiles, t5_t, bp, tp)
            b_out, shp, s_p = rwkv(xb, bp, tp, tp, jnp.zeros((bp, 1, B_COLS), f32),
                                   jnp.zeros((bp, B_HEADS, B_HEAD_DIM, B_HEAD_DIM), f32))
            mp_args = ([a_out, b_out], z, w_out)
            kp_l.append(k.reshape(bp, tp, A_HEADS, A_HEAD_DIM))
            vp_l.append(v.reshape(bp, tp, A_HEADS, A_HEAD_DIM))
            sp_l.append(s_p)
            shp_l.append(shp)

            q, k, v = _matmul(xs, wq), _matmul(xs, wk), _matmul(xs, wv)
            xb, z = _matmul(xs, wxb, tn=640), _matmul(xs, wz)
            ck = cache_moba_k[e].reshape(-1, PAGE_SIZE, A_W)
            cv = cache_moba_v[e].reshape(-1, PAGE_SIZE, A_W)
            pad8 = lambda a: _pad_rows(a.reshape(bs, ts, A_W), SUBLANES)
            q8, k8, v8 = pad8(q), pad8(k), pad8(v)
            kmean = _moba_kmean(pt, ck, bs).reshape(bs, PAST_LEN // MOBA_BLOCK, A_W)
            top = _moba_topk(q8, kmean, bs)
            top = top.reshape(bs, A_HEADS, SUBLANES, LANES)[:, :, :ts, :MOBA_TOPK].reshape(-1)
            a_out = _moba_sample(pt, top, t5_t, q8, k8, v8, ck, cv, bs, ts)[:, :ts].reshape(bs * ts, A_W)
            b_out, shs, s_s = rwkv(xb, bs, ts, RWKV_CHUNK, state_shift[e].reshape(bs, 1, B_COLS), state_rwkv[e])
            ms_args = ([a_out, b_out], z, w_out)
            ks_l.append(k.reshape(bs, ts, A_HEADS, A_HEAD_DIM))
            vs_l.append(v.reshape(bs, ts, A_HEADS, A_HEAD_DIM))
            ss_l.append(s_s)
            shs_l.append(shs)
        else:
            o = l // 2
            w_in = w_in_odd[o]
            wc, wz = cast(w_in[:, :C_W]), cast(w_in[:, C_W:])
            w_out = cast(w_out_odd[o])
            pw, sc = cast(pool_w[o]), row2(pool_scale[o])

            xc, z = _matmul(xp, wc), _matmul(xp, wz)
            xc3 = xc.reshape(bp, tp, C_W)
            y = _pool(xc3, jnp.zeros((bp, POOL_BUF + 1, C_W), f32), 0, pw, sc)
            mp_args = ([y.reshape(bp * tp, C_W)], z, w_out)
            poolp_l.append(xc3[:, tp - POOL_BUF:])

            xc, z = _matmul(xs, wc), _matmul(xs, wz)
            xc3 = xc.reshape(bs, ts, C_W)
            buf16 = jnp.pad(state_pool[o], ((0, 0), (1, 0), (0, 0)))
            y = _pool(_pad_rows(xc3, POOL_BUF + 1), buf16, PAST_LEN, pw, sc)[:, :ts]
            ms_args = ([y.reshape(bs * ts, C_W)], z, w_out)
            pools_l.append(jnp.concatenate([state_pool[o], xc3], axis=1)[:, -POOL_BUF:])

        g, b = row2(ln_mix_g[l]), row2(ln_mix_b[l])
        xp = _proj_ln(*mp_args, xp, g, b)
        xs = _proj_ln(*ms_args, xs, g, b)

        wq, wk, wv, wo = cast(xattn_w_q[l]), cast(xattn_w_k[l]), cast(xattn_w_v[l]), cast(xattn_w_o[l])
        g, b = row2(ln_x_g[l]), row2(ln_x_b[l])
        mk, mv = _matmul(mem, wk), _matmul(mem, wv)
        mkp_l.append(mk.reshape(bp, MEM_LEN, X_HEADS, X_HEAD_DIM))
        mvp_l.append(mv.reshape(bp, MEM_LEN, X_HEADS, X_HEAD_DIM))
        att = _xattn(_matmul(xp, wq), mk.reshape(bp, MEM_LEN, D_MODEL), mv.reshape(bp, MEM_LEN, D_MODEL), bp, 256)
        xp = _proj_ln([att], None, wo, xp, g, b)
        qs8 = _pad_rows(_matmul(xs, wq).reshape(bs, ts, D_MODEL), SUBLANES).reshape(bs * SUBLANES, D_MODEL)
        att = _xattn(qs8, cache_mem_k[l].reshape(bs, MEM_LEN, D_MODEL), cache_mem_v[l].reshape(bs, MEM_LEN, D_MODEL),
                     bs, SUBLANES)
        att = att.reshape(bs, SUBLANES, D_MODEL)[:, :ts].reshape(bs * ts, D_MODEL)
        xs = _proj_ln([att], None, wo, xs, g, b)

    return (xp.reshape(bp, tp, D_MODEL), xs.reshape(bs, ts, D_MODEL),
            jnp.stack(kp_l), jnp.stack(vp_l), jnp.stack(sp_l), jnp.stack(shp_l), jnp.stack(poolp_l),
            jnp.stack(mkp_l), jnp.stack(mvp_l),
            jnp.stack(ks_l), jnp.stack(vs_l), jnp.stack(ss_l), jnp.stack(shs_l), jnp.stack(pools_l))
```

```python
import functools
import math

import numpy as np
import jax
import jax.numpy as jnp
from jax import lax
from jax.experimental import pallas as pl
from jax.experimental.pallas import tpu as pltpu

f32, bf16, i32 = jnp.float32, jnp.bfloat16, jnp.int32

D_MODEL = 2048
DEPTH = 4
PAST_LEN = 16384
PAGE_SIZE = 128
A_HEAD_DIM = 128
A_W = 1024
A_HEADS = 8
MOBA_BLOCK = 256
MOBA_TOPK = 3
B_HEAD_DIM = 64
B_W = 1024
B_HEADS = 16
LORA_W = 64
B_COLS = 3 * B_W + 2 * LORA_W
RWKV_GN_EPS = 64e-5
C_W = 2048
POOL_WINDOWS = (2, 4, 8, 16)
POOL_GW = C_W // len(POOL_WINDOWS)
POOL_BUF = max(POOL_WINDOWS) - 1
MEM_LEN = 256
X_HEADS = 4
X_HEAD_DIM = D_MODEL // X_HEADS
T5_BUCKETS = 32
T5_MAX_DIST = 128
LN_EPS = 1e-5
ALPHA = (2 * DEPTH) ** 0.25

NEG = -1e30
LANES = 128
SUBLANES = 8
RWKV_CHUNK = 64
PAIR = 2 * B_HEAD_DIM
VMEM_LIMIT = 48 * 1024 * 1024

NT = (((1,), (1,)), ((), ()))
TN = (((0,), (0,)), ((), ()))


def _t5_thresholds():
    exact = T5_BUCKETS // 2
    rel = np.arange(0, 4 * T5_MAX_DIST)
    relf = np.maximum(rel, exact).astype(np.float32)
    large = exact + (np.log(relf / np.float32(exact)) / np.float32(math.log(T5_MAX_DIST / exact))
                     * np.float32(T5_BUCKETS - exact)).astype(np.int32)
    bucket = np.where(rel < exact, rel, np.minimum(large, T5_BUCKETS - 1))
    assert (np.diff(bucket) >= 0).all() and bucket[-1] == T5_BUCKETS - 1
    return [int(np.argmax(bucket >= b)) for b in range(T5_BUCKETS)]


T5_THR = _t5_thresholds()


def _dot(a, b, dn=None):
    if dn is None:
        return jnp.dot(a, b, preferred_element_type=f32)
    return lax.dot_general(a, b, dn, preferred_element_type=f32)


def _split(x):
    hi = x.astype(bf16)
    lo = (x - hi.astype(f32)).astype(bf16)
    return hi, lo


def _dot3(a, b, dn=None):
    ah, al = _split(a)
    bh, bl = _split(b)
    return _dot(ah, bh, dn) + _dot(ah, bl, dn) + _dot(al, bh, dn)


def _dot2l(a, b_exact, dn=None):
    ah, al = _split(a)
    return _dot(ah, b_exact, dn) + _dot(al, b_exact, dn)


def _t5_bias_of(rel, tab):
    bias = jnp.full(rel.shape, tab(0), f32)
    for b in range(1, T5_BUCKETS):
        bias = jnp.where(rel >= T5_THR[b], tab(b), bias)
    return bias


def _params(*sem):
    return pltpu.CompilerParams(dimension_semantics=sem, vmem_limit_bytes=VMEM_LIMIT)


def _mm_body(x_ref, w_ref, o_ref, xb_ref):
    @pl.when(pl.program_id(1) == 0)
    def _():
        xb_ref[...] = x_ref[...].astype(bf16)

    o_ref[...] = _dot(xb_ref[...], w_ref[...])


def _matmul(x, w, tn=512):
    m, k = x.shape
    n = w.shape[1]
    tm = min(m, 1024)
    return pl.pallas_call(
        _mm_body,
        grid=(m // tm, n // tn),
        in_specs=[pl.BlockSpec((tm, k), lambda i, j: (i, 0)), pl.BlockSpec((k, tn), lambda i, j: (0, j))],
        out_specs=pl.BlockSpec((tm, tn), lambda i, j: (i, j)),
        out_shape=jax.ShapeDtypeStruct((m, n), f32),
        scratch_shapes=[pltpu.VMEM((tm, k), bf16)],
        compiler_params=_params("parallel", "arbitrary"),
        name="proj_matmul",
    )(x, w)


def _proj_ln_body(n_lhs, gated, *refs):
    lhs = [r[...] for r in refs[:n_lhs]]
    refs = refs[n_lhs:]
    y = lhs[0] if n_lhs == 1 else jnp.concatenate(lhs, axis=1)
    if gated:
        z = refs[0][...]
        refs = refs[1:]
        y = y * (z * jax.nn.sigmoid(z))
    w_ref, res_ref, g_ref, b_ref, o_ref = refs
    u = ALPHA * res_ref[...] + _dot(y.astype(bf16), w_ref[...])
    mu = jnp.mean(u, axis=-1, keepdims=True)
    d = u - mu
    var = jnp.mean(d * d, axis=-1, keepdims=True)
    o_ref[...] = d * lax.rsqrt(var + LN_EPS) * g_ref[...] + b_ref[...]


def _proj_ln(lhs, z, w, res, g, b):
    m = res.shape[0]
    tm = min(m, 256)
    row = lambda width: pl.BlockSpec((tm, width), lambda i: (i, 0))
    full = lambda a: pl.BlockSpec(a.shape, lambda i: (0, 0))
    args = list(lhs) + ([z] if z is not None else []) + [w, res, g, b]
    specs = [row(a.shape[1]) for a in lhs] + ([row(z.shape[1])] if z is not None else []) + [
        full(w), row(D_MODEL), full(g), full(b)]
    return pl.pallas_call(
        functools.partial(_proj_ln_body, len(lhs), z is not None),
        grid=(m // tm,),
        in_specs=specs,
        out_specs=row(D_MODEL),
        out_shape=jax.ShapeDtypeStruct((m, D_MODEL), f32),
        compiler_params=_params("parallel"),
        name="proj_ln",
    )(*args)


def _t5_tiles_body(t5_ref, o_ref):
    h = pl.program_id(0)
    key = lax.broadcasted_iota(i32, (MOBA_BLOCK, MOBA_BLOCK), 0)
    qry = lax.broadcasted_iota(i32, (MOBA_BLOCK, MOBA_BLOCK), 1)
    tab = lambda b: t5_ref[h, b]
    rel = qry - key
    o_ref[0, 0] = jnp.where(rel >= 0, _t5_bias_of(jnp.maximum(rel, 0), tab), NEG)
    o_ref[0, 1] = _t5_bias_of(rel + MOBA_BLOCK, tab)


def _t5_tiles(t5_t):
    return pl.pallas_call(
        _t5_tiles_body,
        grid=(A_HEADS,),
        in_specs=[pl.BlockSpec(memory_space=pltpu.SMEM)],
        out_specs=pl.BlockSpec((1, 2, MOBA_BLOCK, MOBA_BLOCK), lambda h: (h, 0, 0, 0)),
        out_shape=jax.ShapeDtypeStruct((A_HEADS, 2, MOBA_BLOCK, MOBA_BLOCK), f32),
        compiler_params=_params("parallel"),
        name="t5_tiles",
    )(t5_t)


def _moba_prompt_body(nb, t5_ref, q_ref, k_ref, v_ref, bias_ref, o_ref, vt_ref, km_ref, sel_ref, m_ref, l_ref, acc_ref):
    h = pl.program_id(1)
    qi = pl.program_id(2)

    @pl.when(qi == 0)
    def _():
        for n in range(nb):
            vt_ref[n] = v_ref[n].T.astype(bf16)
        km_ref[...] = jnp.concatenate(
            [jnp.mean(k_ref[n], axis=0, keepdims=True) for n in range(nb)], axis=0)

    q = q_ref[...]
    gate = _dot3(km_ref[...], q, NT)
    blk = lax.broadcasted_iota(i32, gate.shape, 0)
    beaten = jnp.zeros(gate.shape, f32)
    for m in range(nb):
        gm = gate[m:m + 1, :]
        wins = (gm > gate) | ((gm == gate) & (m < blk))
        beaten = beaten + jnp.where(wins, 1.0, 0.0) * jnp.where(m < qi, 1.0, 0.0)
    sel_ref[...] = jnp.where((blk < qi) & (beaten < MOBA_TOPK), 0.0, NEG)

    qs = (q * (1.0 / math.sqrt(A_HEAD_DIM))).astype(bf16)

    def scores(n):
        return _dot(k_ref[n].astype(bf16), qs, NT)

    s = scores(qi) + bias_ref[0, 0]
    m0 = jnp.max(s, axis=0, keepdims=True)
    p = jnp.exp(s - m0)
    m_ref[...] = m0
    l_ref[...] = jnp.sum(p, axis=0, keepdims=True)
    acc_ref[...] = _dot(vt_ref[qi], p.astype(bf16))

    far_bias = t5_ref[h, T5_BUCKETS - 1]
    for d in range(1, nb):
        @pl.when(qi >= d)
        def _():
            n = qi - d
            s = scores(n) + sel_ref[pl.ds(n, 1), :]
            s = s + (bias_ref[0, 1] if d == 1 else far_bias)
            m_old = m_ref[...]
            m_new = jnp.maximum(m_old, jnp.max(s, axis=0, keepdims=True))
            alpha = jnp.exp(m_old - m_new)
            p = jnp.exp(s - m_new)
            m_ref[...] = m_new
            l_ref[...] = alpha * l_ref[...] + jnp.sum(p, axis=0, keepdims=True)
            acc_ref[...] = alpha * acc_ref[...] + _dot(vt_ref[n], p.astype(bf16))

    o_ref[...] = (acc_ref[...] / l_ref[...]).T


def _moba_prompt(q, k, v, bias_tiles, t5_t, batch, seq):
    assert 2 * MOBA_BLOCK - (MOBA_BLOCK - 1) >= T5_THR[-1]
    nb = seq // MOBA_BLOCK
    k3 = k.reshape(batch * nb, MOBA_BLOCK, A_W)
    v3 = v.reshape(batch * nb, MOBA_BLOCK, A_W)
    qspec = pl.BlockSpec((MOBA_BLOCK, A_HEAD_DIM), lambda b, h, i: (b * nb + i, h))
    kvspec = pl.BlockSpec((nb, MOBA_BLOCK, A_HEAD_DIM), lambda b, h, i: (b, 0, h))
    return pl.pallas_call(
        functools.partial(_moba_prompt_body, nb),
        grid=(batch, A_HEADS, nb),
        in_specs=[pl.BlockSpec(memory_space=pltpu.SMEM), qspec, kvspec, kvspec,
                  pl.BlockSpec((1, 2, MOBA_BLOCK, MOBA_BLOCK), lambda b, h, i: (h, 0, 0, 0))],
        out_specs=qspec,
        out_shape=jax.ShapeDtypeStruct((batch * seq, A_W), f32),
        scratch_shapes=[pltpu.VMEM((nb, A_HEAD_DIM, MOBA_BLOCK), bf16), pltpu.VMEM((nb, A_HEAD_DIM), f32),
                        pltpu.VMEM((nb, MOBA_BLOCK), f32), pltpu.VMEM((1, MOBA_BLOCK), f32),
                        pltpu.VMEM((1, MOBA_BLOCK), f32), pltpu.VMEM((A_HEAD_DIM, MOBA_BLOCK), f32)],
        compiler_params=_params("parallel", "parallel", "arbitrary"),
        name="moba_prompt",
    )(t5_t, q, k3, v3, bias_tiles)


PAGES_PER_SEQ = PAST_LEN // PAGE_SIZE
PAST_BLOCKS = PAST_LEN // MOBA_BLOCK
PAGES_PER_BLOCK = MOBA_BLOCK // PAGE_SIZE
KMEAN_BLOCKS_PER_STEP = SUBLANES


def _kmean_body(pt_ref, *refs):
    pages, o_ref = refs[:-1], refs[-1]
    for jj in range(KMEAN_BLOCKS_PER_STEP):
        s = functools.reduce(jnp.add, [jnp.sum(pages[PAGES_PER_BLOCK * jj + i][0, 0], axis=0)
                                       for i in range(PAGES_PER_BLOCK)])
        s = s * (1.0 / MOBA_BLOCK)
        for h in range(A_HEADS):
            o_ref[0, h, jj:jj + 1, :] = s[h:h + 1, :]


def _moba_kmean(pt, cache_k, layer, batch):
    per_step = PAGES_PER_BLOCK * KMEAN_BLOCKS_PER_STEP
    page = lambda i: pl.BlockSpec((1, 1, PAGE_SIZE, A_HEADS, A_HEAD_DIM),
                                  lambda b, g, pt: (layer, pt[b * PAGES_PER_SEQ + g * per_step + i], 0, 0, 0))
    return pl.pallas_call(
        _kmean_body,
        grid_spec=pltpu.PrefetchScalarGridSpec(
            num_scalar_prefetch=1, grid=(batch, PAST_BLOCKS // KMEAN_BLOCKS_PER_STEP),
            in_specs=[page(i) for i in range(per_step)],
            out_specs=pl.BlockSpec((1, A_HEADS, KMEAN_BLOCKS_PER_STEP, A_HEAD_DIM), lambda b, g, pt: (b, 0, g, 0))),
        out_shape=jax.ShapeDtypeStruct((batch, A_HEADS, PAST_BLOCKS, A_HEAD_DIM), f32),
        compiler_params=_params("parallel", "arbitrary"),
        name="moba_kmean",
    )(pt, *([cache_k] * per_step))


def _moba_topk_body(q_ref, km_ref, o_ref):
    lane = lax.broadcasted_iota(i32, (SUBLANES, LANES), 1)
    for h in range(A_HEADS):
        cols = slice(h * A_HEAD_DIM, (h + 1) * A_HEAD_DIM)
        gate = _dot3(q_ref[0, :, cols], km_ref[0, h], NT)
        blk = lax.broadcasted_iota(i32, gate.shape, 1)
        out = jnp.zeros((SUBLANES, LANES), i32)
        for j in range(MOBA_TOPK):
            best = jnp.max(gate, axis=1, keepdims=True)
            idx = jnp.min(jnp.where(gate == best, blk, PAST_BLOCKS), axis=1, keepdims=True)
            out = jnp.where(lane == j, idx, out)
            gate = jnp.where(blk == idx, -jnp.inf, gate)
        o_ref[0, h * SUBLANES:(h + 1) * SUBLANES, :] = out


def _moba_topk(q8, kmean, batch):
    return pl.pallas_call(
        _moba_topk_body,
        grid=(batch,),
        in_specs=[pl.BlockSpec((1, SUBLANES, A_W), lambda b: (b, 0, 0)),
                  pl.BlockSpec((1, A_HEADS, PAST_BLOCKS, A_HEAD_DIM), lambda b: (b, 0, 0, 0))],
        out_specs=pl.BlockSpec((1, A_HEADS * SUBLANES, LANES), lambda b: (b, 0, 0)),
        out_shape=jax.ShapeDtypeStruct((batch, A_HEADS * SUBLANES, LANES), i32),
        compiler_params=_params("parallel"),
        name="moba_topk",
    )(q8, kmean)


def _moba_sample_body(t_new, layer, pt_ref, top_ref, t5_ref, q_ref, kn_ref, vn_ref, kc_hbm, vc_hbm, o_ref,
                      kbuf, vbuf, sem):
    b, h = pl.program_id(0), pl.program_id(1)
    nh = pl.num_programs(1)
    step = b * nh + h
    nstep = pl.num_programs(0) * nh
    per_q = MOBA_TOPK * PAGES_PER_BLOCK
    top_at = lambda bb, hh, qq, j: top_ref[((bb * A_HEADS + hh) * t_new + qq) * MOBA_TOPK + j]

    def copies(bb, hh, slot):
        out = []
        for qq in range(t_new):
            for j in range(MOBA_TOPK):
                blk = top_at(bb, hh, qq, j)
                for half in range(PAGES_PER_BLOCK):
                    pg = pt_ref[bb * PAGES_PER_SEQ + PAGES_PER_BLOCK * blk + half]
                    i = qq * per_q + j * PAGES_PER_BLOCK + half
                    out.append(pltpu.make_async_copy(kc_hbm.at[layer, pg, :, hh, :], kbuf.at[slot, i], sem.at[slot, 0]))
                    out.append(pltpu.make_async_copy(vc_hbm.at[layer, pg, :, hh, :], vbuf.at[slot, i], sem.at[slot, 1]))
        return out

    slot = step % 2

    @pl.when(step == 0)
    def _():
        for c in copies(b, h, slot):
            c.start()

    @pl.when(step + 1 < nstep)
    def _():
        nxt = step + 1
        for c in copies(nxt // nh, nxt % nh, 1 - slot):
            c.start()

    tab = lambda bkt: t5_ref[h, bkt]
    q = q_ref[0] * (1.0 / math.sqrt(A_HEAD_DIM))
    qb = q.astype(bf16)
    row = lax.broadcasted_iota(i32, (SUBLANES, 1), 0)
    kn, vn = kn_ref[0], vn_ref[0]
    own = []
    for t in range(t_new):
        sc = jnp.sum(q * kn[t:t + 1, :], axis=1, keepdims=True) + _t5_bias_of(jnp.maximum(row - t, 0), tab)
        own.append(jnp.where(row >= t, sc, NEG))
    own_max = functools.reduce(jnp.maximum, own)

    for c in copies(b, h, slot):
        c.wait()

    lane = lax.broadcasted_iota(i32, (1, per_q * PAGE_SIZE), 1)
    l_sum = jnp.zeros((SUBLANES, 1), f32)
    acc = jnp.zeros((SUBLANES, A_HEAD_DIM), f32)
    for qq in range(t_new):
        kq = kbuf[slot, qq * per_q:(qq + 1) * per_q].reshape(per_q * PAGE_SIZE, A_HEAD_DIM)
        vq = vbuf[slot, qq * per_q:(qq + 1) * per_q].reshape(per_q * PAGE_SIZE, A_HEAD_DIM)
        keypos = lane - (MOBA_TOPK - 1) * MOBA_BLOCK + top_at(b, h, qq, MOBA_TOPK - 1) * MOBA_BLOCK
        for j in range(MOBA_TOPK - 2, -1, -1):
            keypos = jnp.where(lane < (j + 1) * MOBA_BLOCK, lane - j * MOBA_BLOCK + top_at(b, h, qq, j) * MOBA_BLOCK,
                               keypos)
        lg = _dot(qb, kq.astype(bf16), NT) + _t5_bias_of(PAST_LEN + qq - keypos, tab)
        mine = row == qq
        m = jnp.maximum(own_max, jnp.max(lg, axis=1, keepdims=True))
        p = jnp.where(mine, jnp.exp(lg - m), 0.0)
        l_sum = l_sum + jnp.sum(p, axis=1, keepdims=True)
        acc = acc + _dot(p.astype(bf16), vq.astype(bf16))
        for t in range(t_new):
            po = jnp.where(mine, jnp.exp(own[t] - m), 0.0)
            l_sum = l_sum + po
            acc = acc + po * vn[t:t + 1, :]
    o_ref[0] = acc / jnp.where(row < t_new, l_sum, 1.0)


def _moba_sample(pt, top, t5_t, q8, kn8, vn8, cache_k, cache_v, layer, batch, t_new):
    npage = t_new * MOBA_TOPK * PAGES_PER_BLOCK
    rows = pl.BlockSpec((1, SUBLANES, A_HEAD_DIM), lambda b, h, pt, top: (b, 0, h))
    hbm = pl.BlockSpec(memory_space=pl.ANY)
    return pl.pallas_call(
        functools.partial(_moba_sample_body, t_new, layer),
        grid_spec=pltpu.PrefetchScalarGridSpec(
            num_scalar_prefetch=2, grid=(batch, A_HEADS),
            in_specs=[pl.BlockSpec(memory_space=pltpu.SMEM), rows, rows, rows, hbm, hbm],
            out_specs=rows,
            scratch_shapes=[pltpu.VMEM((2, npage, PAGE_SIZE, A_HEAD_DIM), f32),
                            pltpu.VMEM((2, npage, PAGE_SIZE, A_HEAD_DIM), f32),
                            pltpu.SemaphoreType.DMA((2, 2))]),
        out_shape=jax.ShapeDtypeStruct((batch, SUBLANES, A_W), f32),
        compiler_params=_params("arbitrary", "arbitrary"),
        name="moba_sample",
    )(pt, top, t5_t, q8, kn8, vn8, cache_k, cache_v)


def _rwkv_prep_body(t_valid, t_total, xb_ref, halo_ref, sp_ref, mu_ref, w0_ref, a0_ref, kk_ref, ka_ref, lora_ref, ones_ref,
                    r_ref, l_ref, a_ref, b_ref, k_ref, v_ref):
    i = pl.program_id(1)
    x = xb_ref[0]
    tt = x.shape[0]
    row = lax.broadcasted_iota(i32, (tt, 1), 0)
    first = jnp.where(i == 0, sp_ref[0], halo_ref[0, SUBLANES - 1:SUBLANES, :])
    prev = jnp.where(row == 0, first, pltpu.roll(x, 1, 0))
    xm = x + (prev - x) * mu_ref[...]
    r, k, v = xm[:, :B_W], xm[:, B_W:2 * B_W], xm[:, 2 * B_W:3 * B_W]
    wa = xm[:, 3 * B_W:]
    lane = lax.broadcasted_iota(i32, wa.shape, 1)
    lw = _dot3(jnp.where(lane < LORA_W, jnp.tanh(wa), 0.0), lora_ref[...])
    la = _dot3(jnp.where(lane < LORA_W, 0.0, wa), lora_ref[...])
    wlog = -math.exp(-0.5) * jax.nn.sigmoid(w0_ref[...] + lw)
    a = jax.nn.sigmoid(a0_ref[...] + la)
    kk = k * kk_ref[...]
    ss = _dot2l(kk * kk, ones_ref[...])
    kk = kk / jnp.maximum(jnp.sqrt(ss), 1e-12)
    kp = k * (1.0 + (a - 1.0) * ka_ref[...])
    outs = (r, wlog, -kk, kk * a, kp, v)
    if t_valid < t_total:
        ok = (i * tt + row) < t_valid
        outs = tuple(jnp.where(ok, o, 0.0) for o in outs)
    for ref, o in zip((r_ref, l_ref, a_ref, b_ref, k_ref, v_ref), outs):
        ref[0] = o


def _rwkv_prep(xb3, shift_prev, t_valid, mu, w0, a0, k_k, k_a, lora, ones_bd):
    bn, tp, _ = xb3.shape
    tt = min(tp, 256)
    halo_blocks = tt // SUBLANES
    tile = lambda width: pl.BlockSpec((1, tt, width), lambda b, i: (b, i, 0))
    vec = lambda a: pl.BlockSpec(a.shape, lambda b, i: (0,) * a.ndim)
    out = jax.ShapeDtypeStruct((bn, tp, B_W), f32)
    return pl.pallas_call(
        functools.partial(_rwkv_prep_body, t_valid, tp),
        grid=(bn, tp // tt),
        in_specs=[tile(B_COLS),
                  pl.BlockSpec((1, SUBLANES, B_COLS), lambda b, i: (b, jnp.maximum(i * halo_blocks - 1, 0), 0)),
                  pl.BlockSpec((1, 1, B_COLS), lambda b, i: (b, 0, 0)),
                  vec(mu), vec(w0), vec(a0), vec(k_k), vec(k_a), vec(lora), vec(ones_bd)],
        out_specs=[tile(B_W)] * 6,
        out_shape=[out] * 6,
        compiler_params=_params("parallel", "arbitrary"),
        name="rwkv_prep",
    )(xb3, xb3, shift_prev, mu, w0, a0, k_k, k_a, lora, ones_bd)


def _stack(p, lo):
    return jnp.concatenate([jnp.where(lo, p, 0.0), jnp.where(lo, 0.0, p)], axis=0)


def _rwkv_chunk(R, L, A, B, K, V, BD):
    c = RWKV_CHUNK
    lane = lax.broadcasted_iota(i32, (c, PAIR), 1)
    lo = lane < B_HEAD_DIM
    t = lax.broadcasted_iota(i32, (c, PAIR), 0)
    j = lane & (B_HEAD_DIM - 1)
    strict, incl = t > j, t >= j
    tri = jnp.where(lax.broadcasted_iota(i32, (c, c), 0) >= lax.broadcasted_iota(i32, (c, c), 1), 1.0, 0.0).astype(bf16)
    l1, l2 = _split(L)
    l3 = (L - l1.astype(f32) - l2.astype(f32)).astype(bf16)
    cum = _dot(tri, l1) + _dot(tri, l2) + _dot(tri, l3)
    last = cum[c - 1:c, :]
    At, Rt = A * jnp.exp(cum - L), R * jnp.exp(cum)
    einv, ew = jnp.exp(-cum), jnp.exp(last - cum)
    Bt, Kt, Bw, Kw = B * einv, K * einv, B * ew, K * ew
    st = lambda p: _stack(p, lo).astype(bf16)
    ar = jnp.concatenate([At, Rt], axis=0).astype(bf16)
    pb = _dot(ar, st(Bt), NT)
    pk = _dot(ar, st(Kt), NT)
    pab, prb = jnp.where(strict, pb[:c], 0.0), jnp.where(incl, pb[c:], 0.0)
    pak, prk = jnp.where(strict, pk[:c], 0.0), jnp.where(incl, pk[c:], 0.0)
    x = jnp.where(t == j, 1.0, 0.0) + pab
    qp = pab
    for _ in range(int(math.log2(c)) - 1):
        qp = _dot(qp.astype(bf16), st(qp))
        x = x + _dot(x.astype(bf16), st(qp))
    xb = x.astype(bf16)
    akv = _dot(pak.astype(bf16), st(V))
    ua = _dot(xb, jnp.concatenate([st(akv), st(At)], axis=1))
    u0, ah = ua[:, :PAIR], ua[:, PAIR:]
    ry = _dot(prb.astype(bf16), jnp.concatenate([st(ah), st(u0)], axis=1))
    rh = Rt + ry[:, :PAIR]
    y0 = ry[:, PAIR:] + _dot(prk.astype(bf16), st(V))
    z1 = _dot(Bw.astype(bf16), jnp.concatenate([ah, u0], axis=1).astype(bf16), TN)
    z2 = _dot(Kw.astype(bf16), V.astype(bf16), TN)
    rowi = lax.broadcasted_iota(i32, (PAIR, PAIR), 0)
    coli = lax.broadcasted_iota(i32, (PAIR, PAIR), 1)
    same_head = (rowi < B_HEAD_DIM) == (coli < B_HEAD_DIM)
    wc = jnp.exp(last)
    mt = jnp.where(rowi == coli, wc, 0.0) + jnp.where(same_head, z1[:, :PAIR], 0.0)
    gt = jnp.where(same_head, z1[:, PAIR:] + z2, 0.0)
    y = _dot3(rh, BD) + y0
    return y, _dot3(mt, BD) + gt


def _rwkv_scan_body(npair, r_ref, l_ref, a_ref, b_ref, k_ref, v_ref, bd0_ref, rk_ref, gg_ref, gb_ref, y_ref, bd_ref):
    @pl.when(pl.program_id(2) == 0)
    def _():
        bd_ref[...] = bd0_ref[...]

    rowi = lax.broadcasted_iota(i32, (PAIR, PAIR), 0)
    coli = lax.broadcasted_iota(i32, (PAIR, PAIR), 1)
    same_head = (rowi < B_HEAD_DIM) == (coli < B_HEAD_DIM)
    head_sum = jnp.where(same_head, 1.0, 0.0).astype(bf16)
    head_mean = jnp.where(same_head, 1.0 / B_HEAD_DIM, 0.0).astype(bf16)
    for p in range(npair):
        cols = slice(p * PAIR, (p + 1) * PAIR)
        R, K, V = r_ref[0, :, cols], k_ref[0, :, cols], v_ref[0, :, cols]
        y, bd = _rwkv_chunk(R, l_ref[0, :, cols], a_ref[0, :, cols], b_ref[0, :, cols], K, V, bd_ref[0, p])
        bd_ref[0, p] = bd
        d = y - _dot2l(y, head_mean)
        var = _dot2l(d * d, head_mean)
        yn = d * lax.rsqrt(var + RWKV_GN_EPS) * gg_ref[:, cols] + gb_ref[:, cols]
        y_ref[0, :, cols] = yn + _dot2l(R * K * rk_ref[:, cols], head_sum) * V


def _rwkv_scan(prep, bd0, r_k, gn_g, gn_b, npair=2):
    bn, tp, _ = prep[0].shape
    npg = B_W // (PAIR * npair)
    tile = pl.BlockSpec((1, RWKV_CHUNK, PAIR * npair), lambda b, g, c: (b, c, g))
    state = pl.BlockSpec((1, npair, PAIR, PAIR), lambda b, g, c: (b, g, 0, 0))
    vec = pl.BlockSpec((1, PAIR * npair), lambda b, g, c: (0, g))
    return pl.pallas_call(
        functools.partial(_rwkv_scan_body, npair),
        grid=(bn, npg, tp // RWKV_CHUNK),
        in_specs=[tile] * 6 + [state, vec, vec, vec],
        out_specs=[tile, state],
        out_shape=[jax.ShapeDtypeStruct((bn, tp, B_W), f32), jax.ShapeDtypeStruct(bd0.shape, f32)],
        compiler_params=_params("parallel", "parallel", "arbitrary"),
        name="rwkv_scan",
    )(*prep, bd0, r_k, gn_g, gn_b)


def _state_to_bd(s):
    bn = s.shape[0]
    st = jnp.swapaxes(s, -1, -2).reshape(bn, B_HEADS // 2, 2, B_HEAD_DIM, B_HEAD_DIM)
    return jnp.einsum('bphkv,hg->bphkgv', st, jnp.eye(2, dtype=s.dtype)).reshape(bn, B_HEADS // 2, PAIR, PAIR)


def _bd_to_state(bd):
    bn = bd.shape[0]
    bd6 = bd.reshape(bn, B_HEADS // 2, 2, B_HEAD_DIM, 2, B_HEAD_DIM)
    return jnp.einsum('bphkhv->bphvk', bd6).reshape(bn, B_HEADS, B_HEAD_DIM, B_HEAD_DIM)


def _pool_body(pos0, x_ref, halo_ref, buf_ref, pw_ref, sc_ref, o_ref):
    i = pl.program_id(1)
    x = x_ref[0]
    tt = x.shape[0]
    halo = jnp.where(i == 0, buf_ref[0], halo_ref[0])
    xe = jnp.concatenate([halo, x], axis=0)
    row = lax.broadcasted_iota(i32, (tt, 1), 0)
    pos = pos0 + i * tt + row
    outs = []
    for g, w in enumerate(POOL_WINDOWS):
        cols = slice(g * POOL_GW, (g + 1) * POOL_GW)
        s = xe[:, cols]
        sh = 1
        while sh < w:
            s = s + pltpu.roll(s, sh, 0)
            sh *= 2
        cnt = jnp.minimum(w, pos + 1).astype(f32)
        pooled = s[POOL_BUF + 1:] / cnt - x[:, cols]
        outs.append(_dot(pooled.astype(bf16), pw_ref[g]))
    o_ref[0] = jnp.concatenate(outs, axis=1) * sc_ref[...]


def _pool(xc3, buf16, pos0, pool_w, scale):
    bn, tp, _ = xc3.shape
    tt = min(tp, 256)
    hb = POOL_BUF + 1
    return pl.pallas_call(
        functools.partial(_pool_body, pos0),
        grid=(bn, tp // tt),
        in_specs=[pl.BlockSpec((1, tt, C_W), lambda b, i: (b, i, 0)),
                  pl.BlockSpec((1, hb, C_W), lambda b, i: (b, jnp.maximum(i * (tt // hb) - 1, 0), 0)),
                  pl.BlockSpec((1, hb, C_W), lambda b, i: (b, 0, 0)),
                  pl.BlockSpec(pool_w.shape, lambda b, i: (0, 0, 0)),
                  pl.BlockSpec(scale.shape, lambda b, i: (0, 0))],
        out_specs=pl.BlockSpec((1, tt, C_W), lambda b, i: (b, i, 0)),
        out_shape=jax.ShapeDtypeStruct((bn, tp, C_W), f32),
        compiler_params=_params("parallel", "arbitrary"),
        name="pool_mix",
    )(xc3, xc3, buf16, pool_w, scale)


def _xattn_body(q_ref, k_ref, v_ref, o_ref):
    q = (q_ref[...] * (1.0 / math.sqrt(X_HEAD_DIM))).astype(bf16)
    s = _dot(q, k_ref[0].astype(bf16), NT)
    p = jnp.exp(s - jnp.max(s, axis=-1, keepdims=True))
    o_ref[...] = _dot(p.astype(bf16), v_ref[0].astype(bf16)) / jnp.sum(p, axis=-1, keepdims=True)


def _xattn(q, mk, mv, batch, tq):
    rows = q.shape[0] // batch
    nt = rows // tq
    qspec = pl.BlockSpec((tq, X_HEAD_DIM), lambda b, h, i: (b * nt + i, h))
    mspec = pl.BlockSpec((1, MEM_LEN, X_HEAD_DIM), lambda b, h, i: (b, 0, h))
    return pl.pallas_call(
        _xattn_body,
        grid=(batch, X_HEADS, nt),
        in_specs=[qspec, mspec, mspec],
        out_specs=qspec,
        out_shape=jax.ShapeDtypeStruct(q.shape, f32),
        compiler_params=_params("parallel", "parallel", "parallel"),
        name="mem_xattn",
    )(q, mk, mv)


def _pad_rows(x3, rows):
    return jnp.pad(x3, ((0, 0), (0, rows - x3.shape[1]), (0, 0)))


def kernel(x_prompt, x_sample, cache_moba_k, cache_moba_v, page_table, state_rwkv, state_shift, state_pool, cache_mem_k, cache_mem_v, mem_prompt, w_in_even, w_out_even, rwkv_mu, rwkv_w0, rwkv_w_up, rwkv_a0, rwkv_a_up, rwkv_k_k, rwkv_k_a, rwkv_r_k, rwkv_gn_g, rwkv_gn_b, t5_bias, w_in_odd, pool_w, pool_scale, w_out_odd, xattn_w_q, xattn_w_k, xattn_w_v, xattn_w_o, ln_mix_g, ln_mix_b, ln_x_g, ln_x_b):
    bp, tp, _ = x_prompt.shape
    bs, ts, _ = x_sample.shape
    xp = x_prompt.reshape(bp * tp, D_MODEL)
    xs = x_sample.reshape(bs * ts, D_MODEL)
    mem = mem_prompt.reshape(bp * MEM_LEN, D_MODEL)
    pt = page_table.reshape(-1)
    t5_t = t5_bias.T
    bias_tiles = _t5_tiles(t5_t)
    head_ones = jnp.asarray(np.kron(np.eye(B_HEADS), np.ones((B_HEAD_DIM, B_HEAD_DIM))), bf16)
    row2 = lambda a: a.reshape(1, -1)
    cast = lambda a: a.astype(bf16)

    kp_l, vp_l, sp_l, shp_l, poolp_l, mkp_l, mvp_l = [], [], [], [], [], [], []
    ks_l, vs_l, ss_l, shs_l, pools_l = [], [], [], [], []
    for l in range(DEPTH):
        if l % 2 == 0:
            e = l // 2
            w_in = w_in_even[e]
            wq, wk, wv = cast(w_in[:, :A_W]), cast(w_in[:, A_W:2 * A_W]), cast(w_in[:, 2 * A_W:3 * A_W])
            wxb, wz = cast(w_in[:, 3 * A_W:3 * A_W + B_COLS]), cast(w_in[:, 3 * A_W + B_COLS:])
            w_out = cast(w_out_even[e])
            lora = jnp.concatenate([rwkv_w_up[e], rwkv_a_up[e]], axis=0)
            rw = (row2(rwkv_mu[e]), row2(rwkv_w0[e]), row2(rwkv_a0[e]), row2(rwkv_k_k[e]), row2(rwkv_k_a[e]),
                  lora, head_ones)
            gn = (row2(rwkv_r_k[e]), row2(rwkv_gn_g[e]), row2(rwkv_gn_b[e]))

            def rwkv(xb, batch, t_real, t_pad, shift_prev, s0):
                xb3 = xb.reshape(batch, t_real, B_COLS)
                if t_pad != t_real:
                    xb3 = _pad_rows(xb3, t_pad)
                prep = _rwkv_prep(xb3, shift_prev, t_real, *rw)
                y, bd = _rwkv_scan(prep, _state_to_bd(s0), *gn)
                return y[:, :t_real].reshape(batch * t_real, B_W), xb3[:, t_real - 1], _bd_to_state(bd)

            q, k, v = _matmul(xp, wq), _matmul(xp, wk), _matmul(xp, wv)
            xb, z = _matmul(xp, wxb, tn=640), _matmul(xp, wz)
            a_out = _moba_prompt(q, k, v, bias_tiles, t5_t, bp, tp)
            b_out, shp, s_p = rwkv(xb, bp, tp, tp, jnp.zeros((bp, 1, B_COLS), f32),
                                   jnp.zeros((bp, B_HEADS, B_HEAD_DIM, B_HEAD_DIM), f32))
            mp_args = ([a_out, b_out], z, w_out)
            kp_l.append(k.reshape(bp, tp, A_HEADS, A_HEAD_DIM))
            vp_l.append(v.reshape(bp, tp, A_HEADS, A_HEAD_DIM))
            sp_l.append(s_p)
            shp_l.append(shp)

            q, k, v = _matmul(xs, wq), _matmul(xs, wk), _matmul(xs, wv)
            xb, z = _matmul(xs, wxb, tn=640), _matmul(xs, wz)
            pad8 = lambda a: _pad_rows(a.reshape(bs, ts, A_W), SUBLANES)
            q8, k8, v8 = pad8(q), pad8(k), pad8(v)
            top = _moba_topk(q8, _moba_kmean(pt, cache_moba_k, e, bs), bs)
            top = top.reshape(bs, A_HEADS, SUBLANES, LANES)[:, :, :ts, :MOBA_TOPK].reshape(-1)
            a_out = _moba_sample(pt, top, t5_t, q8, k8, v8, cache_moba_k, cache_moba_v, e, bs, ts)
            a_out = a_out[:, :ts].reshape(bs * ts, A_W)
            b_out, shs, s_s = rwkv(xb, bs, ts, RWKV_CHUNK, state_shift[e].reshape(bs, 1, B_COLS), state_rwkv[e])
            ms_args = ([a_out, b_out], z, w_out)
            ks_l.append(k.reshape(bs, ts, A_HEADS, A_HEAD_DIM))
            vs_l.append(v.reshape(bs, ts, A_HEADS, A_HEAD_DIM))
            ss_l.append(s_s)
            shs_l.append(shs)
        else:
            o = l // 2
            w_in = w_in_odd[o]
            wc, wz = cast(w_in[:, :C_W]), cast(w_in[:, C_W:])
            w_out = cast(w_out_odd[o])
            pw, sc = cast(pool_w[o]), row2(pool_scale[o])

            xc, z = _matmul(xp, wc), _matmul(xp, wz)
            xc3 = xc.reshape(bp, tp, C_W)
            y = _pool(xc3, jnp.zeros((bp, POOL_BUF + 1, C_W), f32), 0, pw, sc)
            mp_args = ([y.reshape(bp * tp, C_W)], z, w_out)
            poolp_l.append(xc3[:, tp - POOL_BUF:])

            xc, z = _matmul(xs, wc), _matmul(xs, wz)
            xc3 = xc.reshape(bs, ts, C_W)
            buf16 = jnp.pad(state_pool[o], ((0, 0), (1, 0), (0, 0)))
            y = _pool(_pad_rows(xc3, POOL_BUF + 1), buf16, PAST_LEN, pw, sc)[:, :ts]
            ms_args = ([y.reshape(bs * ts, C_W)], z, w_out)
            pools_l.append(jnp.concatenate([state_pool[o], xc3], axis=1)[:, -POOL_BUF:])

        g, b = row2(ln_mix_g[l]), row2(ln_mix_b[l])
        xp = _proj_ln(*mp_args, xp, g, b)
        xs = _proj_ln(*ms_args, xs, g, b)

        wq, wk, wv, wo = cast(xattn_w_q[l]), cast(xattn_w_k[l]), cast(xattn_w_v[l]), cast(xattn_w_o[l])
        g, b = row2(ln_x_g[l]), row2(ln_x_b[l])
        mk, mv = _matmul(mem, wk), _matmul(mem, wv)
        mkp_l.append(mk.reshape(bp, MEM_LEN, X_HEADS, X_HEAD_DIM))
        mvp_l.append(mv.reshape(bp, MEM_LEN, X_HEADS, X_HEAD_DIM))
        att = _xattn(_matmul(xp, wq), mk.reshape(bp, MEM_LEN, D_MODEL), mv.reshape(bp, MEM_LEN, D_MODEL), bp, 256)
        xp = _proj_ln([att], None, wo, xp, g, b)
        qs8 = _pad_rows(_matmul(xs, wq).reshape(bs, ts, D_MODEL), SUBLANES).reshape(bs * SUBLANES, D_MODEL)
        att = _xattn(qs8, cache_mem_k[l].reshape(bs, MEM_LEN, D_MODEL), cache_mem_v[l].reshape(bs, MEM_LEN, D_MODEL),
                     bs, SUBLANES)
        att = att.reshape(bs, SUBLANES, D_MODEL)[:, :ts].reshape(bs * ts, D_MODEL)
        xs = _proj_ln([att], None, wo, xs, g, b)

    return (xp.reshape(bp, tp, D_MODEL), xs.reshape(bs, ts, D_MODEL),
            jnp.stack(kp_l), jnp.stack(vp_l), jnp.stack(sp_l), jnp.stack(shp_l), jnp.stack(poolp_l),
            jnp.stack(mkp_l), jnp.stack(mvp_l),
            jnp.stack(ks_l), jnp.stack(vs_l), jnp.stack(ss_l), jnp.stack(shs_l), jnp.stack(pools_l))
```

```python
import functools
import math

import numpy as np
import jax
import jax.numpy as jnp
from jax import lax
from jax.experimental import pallas as pl
from jax.experimental.pallas import tpu as pltpu

f32, bf16, i32 = jnp.float32, jnp.bfloat16, jnp.int32

D_MODEL = 2048
DEPTH = 4
PAST_LEN = 16384
PAGE_SIZE = 128
A_HEAD_DIM = 128
A_W = 1024
A_HEADS = 8
MOBA_BLOCK = 256
MOBA_TOPK = 3
B_HEAD_DIM = 64
B_W = 1024
B_HEADS = 16
LORA_W = 64
B_COLS = 3 * B_W + 2 * LORA_W
RWKV_GN_EPS = 64e-5
C_W = 2048
POOL_WINDOWS = (2, 4, 8, 16)
POOL_GW = C_W // len(POOL_WINDOWS)
POOL_BUF = max(POOL_WINDOWS) - 1
MEM_LEN = 256
X_HEADS = 4
X_HEAD_DIM = D_MODEL // X_HEADS
T5_BUCKETS = 32
T5_MAX_DIST = 128
LN_EPS = 1e-5
ALPHA = (2 * DEPTH) ** 0.25

NEG = -1e30
LANES = 128
SUBLANES = 8
RWKV_CHUNK = 64
PAIR = 2 * B_HEAD_DIM
VMEM_LIMIT = 48 * 1024 * 1024

NT = (((1,), (1,)), ((), ()))
TN = (((0,), (0,)), ((), ()))


def _t5_thresholds():
    exact = T5_BUCKETS // 2
    rel = np.arange(0, 4 * T5_MAX_DIST)
    relf = np.maximum(rel, exact).astype(np.float32)
    large = exact + (np.log(relf / np.float32(exact)) / np.float32(math.log(T5_MAX_DIST / exact))
                     * np.float32(T5_BUCKETS - exact)).astype(np.int32)
    bucket = np.where(rel < exact, rel, np.minimum(large, T5_BUCKETS - 1))
    assert (np.diff(bucket) >= 0).all() and bucket[-1] == T5_BUCKETS - 1
    return [int(np.argmax(bucket >= b)) for b in range(T5_BUCKETS)]


T5_THR = _t5_thresholds()


def _dot(a, b, dn=None):
    if dn is None:
        return jnp.dot(a, b, preferred_element_type=f32)
    return lax.dot_general(a, b, dn, preferred_element_type=f32)


def _split(x):
    hi = x.astype(bf16)
    lo = (x - hi.astype(f32)).astype(bf16)
    return hi, lo


def _dot3(a, b, dn=None):
    ah, al = _split(a)
    bh, bl = _split(b)
    return _dot(ah, bh, dn) + _dot(ah, bl, dn) + _dot(al, bh, dn)


def _dot2l(a, b_exact, dn=None):
    ah, al = _split(a)
    return _dot(ah, b_exact, dn) + _dot(al, b_exact, dn)


def _t5_bias_of(rel, tab):
    bias = jnp.full(rel.shape, tab(0), f32)
    for b in range(1, T5_BUCKETS):
        bias = jnp.where(rel >= T5_THR[b], tab(b), bias)
    return bias


def _params(*sem):
    return pltpu.CompilerParams(dimension_semantics=sem, vmem_limit_bytes=VMEM_LIMIT)


def _mm_body(x_ref, w_ref, o_ref):
    o_ref[...] = _dot(x_ref[...], w_ref[...])


def _matmul(x, w, layer, col0, n, tn=512):
    m, k = x.shape
    tm = min(m, 2048)
    assert col0 % tn == 0 and n % tn == 0 and m % tm == 0
    return pl.pallas_call(
        _mm_body,
        grid=(m // tm, n // tn),
        in_specs=[pl.BlockSpec((tm, k), lambda i, j: (i, 0)),
                  pl.BlockSpec((None, k, tn), lambda i, j: (layer, 0, col0 // tn + j))],
        out_specs=pl.BlockSpec((tm, tn), lambda i, j: (i, j)),
        out_shape=jax.ShapeDtypeStruct((m, n), f32),
        compiler_params=_params("parallel", "arbitrary"),
        name="proj_matmul",
    )(x, w)


def _proj_ln_body(n_lhs, gated, *refs):
    lhs = [r[...] for r in refs[:n_lhs]]
    refs = refs[n_lhs:]
    y = lhs[0] if n_lhs == 1 else jnp.concatenate(lhs, axis=1)
    if gated:
        z = refs[0][...]
        refs = refs[1:]
        y = y * (z * jax.nn.sigmoid(z))
    w_ref, res_ref, g_ref, b_ref, o_ref, ob_ref = refs
    u = ALPHA * res_ref[...] + _dot(y.astype(bf16), w_ref[...])
    mu = jnp.mean(u, axis=-1, keepdims=True)
    d = u - mu
    var = jnp.mean(d * d, axis=-1, keepdims=True)
    out = d * lax.rsqrt(var + LN_EPS) * g_ref[...] + b_ref[...]
    o_ref[...] = out
    ob_ref[...] = out.astype(bf16)


def _proj_ln(lhs, z, w, layer, res, g, b):
    m = res.shape[0]
    tm = min(m, 256)
    row = lambda width: pl.BlockSpec((tm, width), lambda i: (i, 0))
    full = lambda a: pl.BlockSpec(a.shape, lambda i: (0, 0))
    args = list(lhs) + ([z] if z is not None else []) + [w, res, g, b]
    specs = [row(a.shape[1]) for a in lhs] + ([row(z.shape[1])] if z is not None else []) + [
        pl.BlockSpec((None,) + w.shape[1:], lambda i: (layer, 0, 0)), row(D_MODEL), full(g), full(b)]
    return pl.pallas_call(
        functools.partial(_proj_ln_body, len(lhs), z is not None),
        grid=(m // tm,),
        in_specs=specs,
        out_specs=[row(D_MODEL), row(D_MODEL)],
        out_shape=[jax.ShapeDtypeStruct((m, D_MODEL), f32), jax.ShapeDtypeStruct((m, D_MODEL), bf16)],
        compiler_params=_params("parallel"),
        name="proj_ln",
    )(*args)


def _t5_tiles_body(t5_ref, o_ref):
    h = pl.program_id(0)
    key = lax.broadcasted_iota(i32, (MOBA_BLOCK, MOBA_BLOCK), 0)
    qry = lax.broadcasted_iota(i32, (MOBA_BLOCK, MOBA_BLOCK), 1)
    tab = lambda b: t5_ref[h, b]
    rel = qry - key
    o_ref[0, 0] = jnp.where(rel >= 0, _t5_bias_of(jnp.maximum(rel, 0), tab), NEG)
    o_ref[0, 1] = _t5_bias_of(rel + MOBA_BLOCK, tab)


def _t5_tiles(t5_t):
    return pl.pallas_call(
        _t5_tiles_body,
        grid=(A_HEADS,),
        in_specs=[pl.BlockSpec(memory_space=pltpu.SMEM)],
        out_specs=pl.BlockSpec((1, 2, MOBA_BLOCK, MOBA_BLOCK), lambda h: (h, 0, 0, 0)),
        out_shape=jax.ShapeDtypeStruct((A_HEADS, 2, MOBA_BLOCK, MOBA_BLOCK), f32),
        compiler_params=_params("parallel"),
        name="t5_tiles",
    )(t5_t)


def _moba_prompt_body(nb, t5_ref, q_ref, k_ref, v_ref, bias_ref, o_ref):
    blkw = MOBA_BLOCK
    far_bias = t5_ref[pl.program_id(1), T5_BUCKETS - 1]
    k = k_ref[...]
    q = q_ref[...]
    kb = k.astype(bf16)
    vt = v_ref[...].T.astype(bf16)
    km = jnp.concatenate([jnp.mean(k[n * blkw:(n + 1) * blkw], axis=0, keepdims=True) for n in range(nb)], axis=0)
    gate_all = _dot3(km, q, NT)
    qs = (q * (1.0 / math.sqrt(A_HEAD_DIM))).astype(bf16)
    blk = lax.broadcasted_iota(i32, (nb, blkw), 0)
    for qi in range(nb):
        rows = slice(qi * blkw, (qi + 1) * blkw)
        gate = gate_all[:, rows]
        beaten = jnp.zeros(gate.shape, f32)
        for m in range(qi):
            gm = gate[m:m + 1, :]
            beaten = beaten + jnp.where((gm > gate) | ((gm == gate) & (m < blk)), 1.0, 0.0)
        sel = jnp.where((blk < qi) & (beaten < MOBA_TOPK), 0.0, NEG)
        s_all = _dot(kb[:(qi + 1) * blkw], qs[rows], NT)
        tiles = []
        for n in range(qi + 1):
            s = s_all[n * blkw:(n + 1) * blkw]
            if n == qi:
                s = s + bias_ref[0, 0]
            else:
                s = s + sel[n:n + 1, :] + (bias_ref[0, 1] if n == qi - 1 else far_bias)
            tiles.append(s)
        mx = functools.reduce(jnp.maximum, [jnp.max(s, axis=0, keepdims=True) for s in tiles])
        ps = [jnp.exp(s - mx) for s in tiles]
        den = functools.reduce(jnp.add, [jnp.sum(p, axis=0, keepdims=True) for p in ps])
        pcat = jnp.concatenate([p.astype(bf16) for p in ps], axis=0)
        acc = _dot(vt[:, :(qi + 1) * blkw], pcat)
        o_ref[rows, :] = (acc / den).T


def _moba_prompt(q, k, v, bias_tiles, t5_t, batch, seq):
    assert 2 * MOBA_BLOCK - (MOBA_BLOCK - 1) >= T5_THR[-1]
    nb = seq // MOBA_BLOCK
    spec = pl.BlockSpec((seq, A_HEAD_DIM), lambda b, h: (b, h))
    return pl.pallas_call(
        functools.partial(_moba_prompt_body, nb),
        grid=(batch, A_HEADS),
        in_specs=[pl.BlockSpec(memory_space=pltpu.SMEM), spec, spec, spec,
                  pl.BlockSpec((1, 2, MOBA_BLOCK, MOBA_BLOCK), lambda b, h: (h, 0, 0, 0))],
        out_specs=spec,
        out_shape=jax.ShapeDtypeStruct((batch * seq, A_W), f32),
        compiler_params=_params("parallel", "parallel"),
        name="moba_prompt",
    )(t5_t, q, k, v, bias_tiles)


PAGES_PER_SEQ = PAST_LEN // PAGE_SIZE
PAST_BLOCKS = PAST_LEN // MOBA_BLOCK
PAGES_PER_BLOCK = MOBA_BLOCK // PAGE_SIZE
KMEAN_BLOCKS_PER_STEP = SUBLANES


def _kmean_body(pt_ref, *refs):
    pages, o_ref = refs[:-1], refs[-1]
    for jj in range(KMEAN_BLOCKS_PER_STEP):
        s = functools.reduce(jnp.add, [jnp.sum(pages[PAGES_PER_BLOCK * jj + i][0, 0], axis=0)
                                       for i in range(PAGES_PER_BLOCK)])
        s = s * (1.0 / MOBA_BLOCK)
        for h in range(A_HEADS):
            o_ref[0, h, jj:jj + 1, :] = s[h:h + 1, :]


def _moba_kmean(pt, cache_k, layer, batch):
    per_step = PAGES_PER_BLOCK * KMEAN_BLOCKS_PER_STEP
    page = lambda i: pl.BlockSpec((1, 1, PAGE_SIZE, A_HEADS, A_HEAD_DIM),
                                  lambda b, g, pt: (layer, pt[b * PAGES_PER_SEQ + g * per_step + i], 0, 0, 0))
    return pl.pallas_call(
        _kmean_body,
        grid_spec=pltpu.PrefetchScalarGridSpec(
            num_scalar_prefetch=1, grid=(batch, PAST_BLOCKS // KMEAN_BLOCKS_PER_STEP),
            in_specs=[page(i) for i in range(per_step)],
            out_specs=pl.BlockSpec((1, A_HEADS, KMEAN_BLOCKS_PER_STEP, A_HEAD_DIM), lambda b, g, pt: (b, 0, g, 0))),
        out_shape=jax.ShapeDtypeStruct((batch, A_HEADS, PAST_BLOCKS, A_HEAD_DIM), f32),
        compiler_params=_params("parallel", "arbitrary"),
        name="moba_kmean",
    )(pt, *([cache_k] * per_step))


def _moba_topk_body(q_ref, km_ref, o_ref):
    lane = lax.broadcasted_iota(i32, (SUBLANES, LANES), 1)
    for h in range(A_HEADS):
        cols = slice(h * A_HEAD_DIM, (h + 1) * A_HEAD_DIM)
        gate = _dot3(q_ref[0, :, cols], km_ref[0, h], NT)
        blk = lax.broadcasted_iota(i32, gate.shape, 1)
        out = jnp.zeros((SUBLANES, LANES), i32)
        for j in range(MOBA_TOPK):
            best = jnp.max(gate, axis=1, keepdims=True)
            idx = jnp.min(jnp.where(gate == best, blk, PAST_BLOCKS), axis=1, keepdims=True)
            out = jnp.where(lane == j, idx, out)
            gate = jnp.where(blk == idx, -jnp.inf, gate)
        o_ref[0, h * SUBLANES:(h + 1) * SUBLANES, :] = out


def _moba_topk(q8, kmean, batch):
    return pl.pallas_call(
        _moba_topk_body,
        grid=(batch,),
        in_specs=[pl.BlockSpec((1, SUBLANES, A_W), lambda b: (b, 0, 0)),
                  pl.BlockSpec((1, A_HEADS, PAST_BLOCKS, A_HEAD_DIM), lambda b: (b, 0, 0, 0))],
        out_specs=pl.BlockSpec((1, A_HEADS * SUBLANES, LANES), lambda b: (b, 0, 0)),
        out_shape=jax.ShapeDtypeStruct((batch, A_HEADS * SUBLANES, LANES), i32),
        compiler_params=_params("parallel"),
        name="moba_topk",
    )(q8, kmean)


def _moba_sample_body(t_new, layer, pt_ref, top_ref, t5_ref, q_ref, kn_ref, vn_ref, kc_hbm, vc_hbm, o_ref,
                      kbuf, vbuf, sem):
    b, h = pl.program_id(0), pl.program_id(1)
    nh = pl.num_programs(1)
    step = b * nh + h
    nstep = pl.num_programs(0) * nh
    per_q = MOBA_TOPK * PAGES_PER_BLOCK
    top_at = lambda bb, hh, qq, j: top_ref[((bb * A_HEADS + hh) * t_new + qq) * MOBA_TOPK + j]

    def copies(bb, hh, slot):
        out = []
        for qq in range(t_new):
            for j in range(MOBA_TOPK):
                blk = top_at(bb, hh, qq, j)
                for half in range(PAGES_PER_BLOCK):
                    pg = pt_ref[bb * PAGES_PER_SEQ + PAGES_PER_BLOCK * blk + half]
                    i = qq * per_q + j * PAGES_PER_BLOCK + half
                    out.append(pltpu.make_async_copy(kc_hbm.at[layer, pg, :, hh, :], kbuf.at[slot, i], sem.at[slot, 0]))
                    out.append(pltpu.make_async_copy(vc_hbm.at[layer, pg, :, hh, :], vbuf.at[slot, i], sem.at[slot, 1]))
        return out

    slot = step % 2

    @pl.when(step == 0)
    def _():
        for c in copies(b, h, slot):
            c.start()

    @pl.when(step + 1 < nstep)
    def _():
        nxt = step + 1
        for c in copies(nxt // nh, nxt % nh, 1 - slot):
            c.start()

    tab = lambda bkt: t5_ref[h, bkt]
    q = q_ref[0] * (1.0 / math.sqrt(A_HEAD_DIM))
    qb = q.astype(bf16)
    row = lax.broadcasted_iota(i32, (SUBLANES, 1), 0)
    kn, vn = kn_ref[0], vn_ref[0]
    own = []
    for t in range(t_new):
        sc = jnp.sum(q * kn[t:t + 1, :], axis=1, keepdims=True) + _t5_bias_of(jnp.maximum(row - t, 0), tab)
        own.append(jnp.where(row >= t, sc, NEG))
    own_max = functools.reduce(jnp.maximum, own)

    for c in copies(b, h, slot):
        c.wait()

    lane = lax.broadcasted_iota(i32, (1, per_q * PAGE_SIZE), 1)
    l_sum = jnp.zeros((SUBLANES, 1), f32)
    acc = jnp.zeros((SUBLANES, A_HEAD_DIM), f32)
    for qq in range(t_new):
        kq = kbuf[slot, qq * per_q:(qq + 1) * per_q].reshape(per_q * PAGE_SIZE, A_HEAD_DIM)
        vq = vbuf[slot, qq * per_q:(qq + 1) * per_q].reshape(per_q * PAGE_SIZE, A_HEAD_DIM)
        keypos = lane - (MOBA_TOPK - 1) * MOBA_BLOCK + top_at(b, h, qq, MOBA_TOPK - 1) * MOBA_BLOCK
        for j in range(MOBA_TOPK - 2, -1, -1):
            keypos = jnp.where(lane < (j + 1) * MOBA_BLOCK, lane - j * MOBA_BLOCK + top_at(b, h, qq, j) * MOBA_BLOCK,
                               keypos)
        lg = _dot(qb, kq.astype(bf16), NT) + _t5_bias_of(PAST_LEN + qq - keypos, tab)
        mine = row == qq
        m = jnp.maximum(own_max, jnp.max(lg, axis=1, keepdims=True))
        p = jnp.where(mine, jnp.exp(lg - m), 0.0)
        l_sum = l_sum + jnp.sum(p, axis=1, keepdims=True)
        acc = acc + _dot(p.astype(bf16), vq.astype(bf16))
        for t in range(t_new):
            po = jnp.where(mine, jnp.exp(own[t] - m), 0.0)
            l_sum = l_sum + po
            acc = acc + po * vn[t:t + 1, :]
    o_ref[0] = acc / jnp.where(row < t_new, l_sum, 1.0)


def _moba_sample(pt, top, t5_t, q8, kn8, vn8, cache_k, cache_v, layer, batch, t_new):
    npage = t_new * MOBA_TOPK * PAGES_PER_BLOCK
    rows = pl.BlockSpec((1, SUBLANES, A_HEAD_DIM), lambda b, h, pt, top: (b, 0, h))
    hbm = pl.BlockSpec(memory_space=pl.ANY)
    return pl.pallas_call(
        functools.partial(_moba_sample_body, t_new, layer),
        grid_spec=pltpu.PrefetchScalarGridSpec(
            num_scalar_prefetch=2, grid=(batch, A_HEADS),
            in_specs=[pl.BlockSpec(memory_space=pltpu.SMEM), rows, rows, rows, hbm, hbm],
            out_specs=rows,
            scratch_shapes=[pltpu.VMEM((2, npage, PAGE_SIZE, A_HEAD_DIM), f32),
                            pltpu.VMEM((2, npage, PAGE_SIZE, A_HEAD_DIM), f32),
                            pltpu.SemaphoreType.DMA((2, 2))]),
        out_shape=jax.ShapeDtypeStruct((batch, SUBLANES, A_W), f32),
        compiler_params=_params("arbitrary", "arbitrary"),
        name="moba_sample",
    )(pt, top, t5_t, q8, kn8, vn8, cache_k, cache_v)


def _rwkv_prep_body(t_valid, t_total, xb_ref, halo_ref, sp_ref, mu_ref, w0_ref, a0_ref, kk_ref, ka_ref, lora_ref, ones_ref,
                    r_ref, l_ref, a_ref, b_ref, k_ref, v_ref):
    i = pl.program_id(1)
    x = xb_ref[0]
    tt = x.shape[0]
    row = lax.broadcasted_iota(i32, (tt, 1), 0)
    first = jnp.where(i == 0, sp_ref[0], halo_ref[0, SUBLANES - 1:SUBLANES, :])
    prev = jnp.where(row == 0, first, pltpu.roll(x, 1, 0))
    xm = x + (prev - x) * mu_ref[...]
    r, k, v = xm[:, :B_W], xm[:, B_W:2 * B_W], xm[:, 2 * B_W:3 * B_W]
    wa = xm[:, 3 * B_W:]
    lane = lax.broadcasted_iota(i32, wa.shape, 1)
    lw = _dot3(jnp.where(lane < LORA_W, jnp.tanh(wa), 0.0), lora_ref[...])
    la = _dot3(jnp.where(lane < LORA_W, 0.0, wa), lora_ref[...])
    wlog = -math.exp(-0.5) * jax.nn.sigmoid(w0_ref[...] + lw)
    a = jax.nn.sigmoid(a0_ref[...] + la)
    kk = k * kk_ref[...]
    ss = _dot2l(kk * kk, ones_ref[...])
    kk = kk / jnp.maximum(jnp.sqrt(ss), 1e-12)
    kp = k * (1.0 + (a - 1.0) * ka_ref[...])
    outs = (r, wlog, -kk, kk * a, kp, v)
    if t_valid < t_total:
        ok = (i * tt + row) < t_valid
        outs = tuple(jnp.where(ok, o, 0.0) for o in outs)
    for ref, o in zip((r_ref, l_ref, a_ref, b_ref, k_ref, v_ref), outs):
        ref[0] = o


def _rwkv_prep(xb3, shift_prev, t_valid, mu, w0, a0, k_k, k_a, lora, ones_bd):
    bn, tp, _ = xb3.shape
    tt = min(tp, 256)
    halo_blocks = tt // SUBLANES
    tile = lambda width: pl.BlockSpec((1, tt, width), lambda b, i: (b, i, 0))
    vec = lambda a: pl.BlockSpec(a.shape, lambda b, i: (0,) * a.ndim)
    out = jax.ShapeDtypeStruct((bn, tp, B_W), f32)
    return pl.pallas_call(
        functools.partial(_rwkv_prep_body, t_valid, tp),
        grid=(bn, tp // tt),
        in_specs=[tile(B_COLS),
                  pl.BlockSpec((1, SUBLANES, B_COLS), lambda b, i: (b, jnp.maximum(i * halo_blocks - 1, 0), 0)),
                  pl.BlockSpec((1, 1, B_COLS), lambda b, i: (b, 0, 0)),
                  vec(mu), vec(w0), vec(a0), vec(k_k), vec(k_a), vec(lora), vec(ones_bd)],
        out_specs=[tile(B_W)] * 6,
        out_shape=[out] * 6,
        compiler_params=_params("parallel", "arbitrary"),
        name="rwkv_prep",
    )(xb3, xb3, shift_prev, mu, w0, a0, k_k, k_a, lora, ones_bd)


def _stack(p, lo):
    return jnp.concatenate([jnp.where(lo, p, 0.0), jnp.where(lo, 0.0, p)], axis=0)


def _head_reduce(parts, mat):
    n, c = len(parts), parts[0].shape[0]
    halves = [_split(a) for a in parts]
    out = _dot(jnp.concatenate([h for h, _ in halves] + [l for _, l in halves], axis=0), mat)
    return [out[i * c:(i + 1) * c] + out[(n + i) * c:(n + i + 1) * c] for i in range(n)]


def _rwkv_scan_body(npair, r_ref, l_ref, a_ref, b_ref, k_ref, v_ref, bd0_ref, rk_ref, gg_ref, gb_ref, y_ref, bd_ref):
    @pl.when(pl.program_id(2) == 0)
    def _():
        bd_ref[...] = bd0_ref[...]

    c = RWKV_CHUNK
    pairs = range(npair)
    col = lambda x, p: x[:, p * PAIR:(p + 1) * PAIR]
    lane = lax.broadcasted_iota(i32, (c, PAIR), 1)
    lo = lane < B_HEAD_DIM
    t = lax.broadcasted_iota(i32, (c, PAIR), 0)
    j = lane & (B_HEAD_DIM - 1)
    strict, incl = t > j, t >= j
    eye_c = jnp.where(t == j, 1.0, 0.0)
    rowi = lax.broadcasted_iota(i32, (PAIR, PAIR), 0)
    coli = lax.broadcasted_iota(i32, (PAIR, PAIR), 1)
    same_head = (rowi < B_HEAD_DIM) == (coli < B_HEAD_DIM)
    head_sum = jnp.where(same_head, 1.0, 0.0).astype(bf16)
    head_mean = jnp.where(same_head, 1.0 / B_HEAD_DIM, 0.0).astype(bf16)
    st = lambda x: _stack(x, lo).astype(bf16)
    cast = lambda x: x.astype(bf16)

    R, L, A, B, K, V = (ref[0] for ref in (r_ref, l_ref, a_ref, b_ref, k_ref, v_ref))
    tri = jnp.where(lax.broadcasted_iota(i32, (c, c), 0) >= lax.broadcasted_iota(i32, (c, c), 1), 1.0, 0.0).astype(bf16)
    l1, l2 = _split(L)
    l3 = (L - l1.astype(f32) - l2.astype(f32)).astype(bf16)
    cum = _dot(tri, l1) + _dot(tri, l2) + _dot(tri, l3)
    last = cum[c - 1:c, :]
    wc = jnp.exp(last)
    At, Rt = A * jnp.exp(cum - L), R * jnp.exp(cum)
    einv, ew = jnp.exp(-cum), jnp.exp(last - cum)
    Bt, Kt, Bw, Kw = B * einv, K * einv, B * ew, K * ew

    ar = [cast(jnp.concatenate([col(At, p), col(Rt, p)], axis=0)) for p in pairs]
    pb = [_dot(ar[p], st(col(Bt, p)), NT) for p in pairs]
    pk = [_dot(ar[p], st(col(Kt, p)), NT) for p in pairs]
    pab = [jnp.where(strict, x[:c], 0.0) for x in pb]
    prb = [jnp.where(incl, x[c:], 0.0) for x in pb]
    pak = [jnp.where(strict, x[:c], 0.0) for x in pk]
    prk = [jnp.where(incl, x[c:], 0.0) for x in pk]
    x = [eye_c + m for m in pab]
    qp = pab
    for _ in range(int(math.log2(c)) - 1):
        qp = [_dot(cast(m), st(m)) for m in qp]
        x = [xi + _dot(cast(xi), st(m)) for xi, m in zip(x, qp)]
    akv = [_dot(cast(pak[p]), st(col(V, p))) for p in pairs]
    ua = [_dot(cast(x[p]), jnp.concatenate([st(akv[p]), st(col(At, p))], axis=1)) for p in pairs]
    u0 = [m[:, :PAIR] for m in ua]
    ah = [m[:, PAIR:] for m in ua]
    ry = [_dot(cast(prb[p]), jnp.concatenate([st(ah[p]), st(u0[p])], axis=1)) for p in pairs]
    rh = [col(Rt, p) + ry[p][:, :PAIR] for p in pairs]
    y0 = [ry[p][:, PAIR:] + _dot(cast(prk[p]), st(col(V, p))) for p in pairs]
    z1 = [_dot(cast(col(Bw, p)), cast(jnp.concatenate([ah[p], u0[p]], axis=1)), TN) for p in pairs]
    z2 = [_dot(cast(col(Kw, p)), cast(col(V, p)), TN) for p in pairs]
    mt = [jnp.where(rowi == coli, col(wc, p), 0.0) + jnp.where(same_head, z1[p][:, :PAIR], 0.0) for p in pairs]
    gt = [jnp.where(same_head, z1[p][:, PAIR:] + z2[p], 0.0) for p in pairs]
    nl = c + PAIR
    lhs = [_split(jnp.concatenate([rh[p], mt[p]], axis=0)) for p in pairs]
    bd = [_split(bd_ref[0, p]) for p in pairs]
    main = [_dot(jnp.concatenate(lhs[p], axis=0), bd[p][0]) for p in pairs]
    corr = [_dot(lhs[p][0], bd[p][1]) for p in pairs]
    prod = [main[p][:nl] + main[p][nl:] + corr[p] for p in pairs]
    for p in pairs:
        bd_ref[0, p] = prod[p][c:] + gt[p]
    y = [prod[p][:c] + y0[p] for p in pairs]
    mean = _head_reduce(y, head_mean)
    d = [yi - m for yi, m in zip(y, mean)]
    var = _head_reduce([di * di for di in d], head_mean)
    bonus = _head_reduce([col(R, p) * col(K, p) * col(rk_ref[...], p) for p in pairs], head_sum)
    for p in pairs:
        yn = d[p] * lax.rsqrt(var[p] + RWKV_GN_EPS) * col(gg_ref[...], p) + col(gb_ref[...], p)
        y_ref[0, :, p * PAIR:(p + 1) * PAIR] = yn + bonus[p] * col(V, p)


def _rwkv_scan(prep, bd0, r_k, gn_g, gn_b, npair=B_HEADS // 2):
    bn, tp, _ = prep[0].shape
    npg = B_W // (PAIR * npair)
    tile = pl.BlockSpec((1, RWKV_CHUNK, PAIR * npair), lambda b, g, c: (b, c, g))
    state = pl.BlockSpec((1, npair, PAIR, PAIR), lambda b, g, c: (b, g, 0, 0))
    vec = pl.BlockSpec((1, PAIR * npair), lambda b, g, c: (0, g))
    return pl.pallas_call(
        functools.partial(_rwkv_scan_body, npair),
        grid=(bn, npg, tp // RWKV_CHUNK),
        in_specs=[tile] * 6 + [state, vec, vec, vec],
        out_specs=[tile, state],
        out_shape=[jax.ShapeDtypeStruct((bn, tp, B_W), f32), jax.ShapeDtypeStruct(bd0.shape, f32)],
        compiler_params=_params("parallel", "parallel", "arbitrary"),
        name="rwkv_scan",
    )(*prep, bd0, r_k, gn_g, gn_b)


def _state_to_bd(s):
    bn = s.shape[0]
    st = jnp.swapaxes(s, -1, -2).reshape(bn, B_HEADS // 2, 2, B_HEAD_DIM, B_HEAD_DIM)
    return jnp.einsum('bphkv,hg->bphkgv', st, jnp.eye(2, dtype=s.dtype)).reshape(bn, B_HEADS // 2, PAIR, PAIR)


def _bd_to_state(bd):
    bn = bd.shape[0]
    bd6 = bd.reshape(bn, B_HEADS // 2, 2, B_HEAD_DIM, 2, B_HEAD_DIM)
    return jnp.einsum('bphkhv->bphvk', bd6).reshape(bn, B_HEADS, B_HEAD_DIM, B_HEAD_DIM)


def _pool_body(pos0, x_ref, halo_ref, buf_ref, pw_ref, sc_ref, o_ref):
    i = pl.program_id(1)
    x = x_ref[0]
    tt = x.shape[0]
    halo = jnp.where(i == 0, buf_ref[0], halo_ref[0])
    xe = jnp.concatenate([halo, x], axis=0)
    row = lax.broadcasted_iota(i32, (tt, 1), 0)
    pos = pos0 + i * tt + row
    outs = []
    for g, w in enumerate(POOL_WINDOWS):
        cols = slice(g * POOL_GW, (g + 1) * POOL_GW)
        s = xe[:, cols]
        sh = 1
        while sh < w:
            s = s + pltpu.roll(s, sh, 0)
            sh *= 2
        cnt = jnp.minimum(w, pos + 1).astype(f32)
        pooled = s[POOL_BUF + 1:] / cnt - x[:, cols]
        outs.append(_dot(pooled.astype(bf16), pw_ref[g]))
    o_ref[0] = jnp.concatenate(outs, axis=1) * sc_ref[...]


def _pool(xc3, buf16, pos0, pool_w, layer, scale):
    bn, tp, _ = xc3.shape
    tt = min(tp, 256)
    hb = POOL_BUF + 1
    return pl.pallas_call(
        functools.partial(_pool_body, pos0),
        grid=(bn, tp // tt),
        in_specs=[pl.BlockSpec((1, tt, C_W), lambda b, i: (b, i, 0)),
                  pl.BlockSpec((1, hb, C_W), lambda b, i: (b, jnp.maximum(i * (tt // hb) - 1, 0), 0)),
                  pl.BlockSpec((1, hb, C_W), lambda b, i: (b, 0, 0)),
                  pl.BlockSpec((None,) + pool_w.shape[1:], lambda b, i: (layer, 0, 0, 0)),
                  pl.BlockSpec(scale.shape, lambda b, i: (0, 0))],
        out_specs=pl.BlockSpec((1, tt, C_W), lambda b, i: (b, i, 0)),
        out_shape=jax.ShapeDtypeStruct((bn, tp, C_W), f32),
        compiler_params=_params("parallel", "arbitrary"),
        name="pool_mix",
    )(xc3, xc3, buf16, pool_w, scale)


def _xattn_body(q_ref, k_ref, v_ref, o_ref):
    q = (q_ref[...] * (1.0 / math.sqrt(X_HEAD_DIM))).astype(bf16)
    s = _dot(q, k_ref[0].astype(bf16), NT)
    p = jnp.exp(s - jnp.max(s, axis=-1, keepdims=True))
    o_ref[...] = _dot(p.astype(bf16), v_ref[0].astype(bf16)) / jnp.sum(p, axis=-1, keepdims=True)


def _xattn(q, mk, mv, batch, tq):
    rows = q.shape[0] // batch
    nt = rows // tq
    qspec = pl.BlockSpec((tq, X_HEAD_DIM), lambda b, h, i: (b * nt + i, h))
    mspec = pl.BlockSpec((1, MEM_LEN, X_HEAD_DIM), lambda b, h, i: (b, 0, h))
    return pl.pallas_call(
        _xattn_body,
        grid=(batch, X_HEADS, nt),
        in_specs=[qspec, mspec, mspec],
        out_specs=qspec,
        out_shape=jax.ShapeDtypeStruct(q.shape, f32),
        compiler_params=_params("parallel", "parallel", "parallel"),
        name="mem_xattn",
    )(q, mk, mv)


def _pad_rows(x3, rows):
    return jnp.pad(x3, ((0, 0), (0, rows - x3.shape[1]), (0, 0)))


def kernel(x_prompt, x_sample, cache_moba_k, cache_moba_v, page_table, state_rwkv, state_shift, state_pool, cache_mem_k, cache_mem_v, mem_prompt, w_in_even, w_out_even, rwkv_mu, rwkv_w0, rwkv_w_up, rwkv_a0, rwkv_a_up, rwkv_k_k, rwkv_k_a, rwkv_r_k, rwkv_gn_g, rwkv_gn_b, t5_bias, w_in_odd, pool_w, pool_scale, w_out_odd, xattn_w_q, xattn_w_k, xattn_w_v, xattn_w_o, ln_mix_g, ln_mix_b, ln_x_g, ln_x_b):
    bp, tp, _ = x_prompt.shape
    bs, ts, _ = x_sample.shape
    xp = x_prompt.reshape(bp * tp, D_MODEL)
    xs = x_sample.reshape(bs * ts, D_MODEL)
    mem = mem_prompt.reshape(bp * MEM_LEN, D_MODEL)
    pt = page_table.reshape(-1)
    t5_t = t5_bias.T
    bias_tiles = _t5_tiles(t5_t)
    head_ones = jnp.asarray(np.kron(np.eye(B_HEADS), np.ones((B_HEAD_DIM, B_HEAD_DIM))), bf16)
    row2 = lambda a: a.reshape(1, -1)
    cast = lambda a: a.astype(bf16)
    gate0 = 3 * A_W + B_COLS
    w_even = cast(jnp.concatenate([w_in_even[:, :, gate0:], w_in_even[:, :, :gate0]], axis=-1))
    col_z, col_q, col_k, col_v, col_xb = 0, D_MODEL, D_MODEL + A_W, D_MODEL + 2 * A_W, D_MODEL + 3 * A_W
    w_odd, w_oe, w_oo, pw = cast(w_in_odd), cast(w_out_even), cast(w_out_odd), cast(pool_w)
    xw_q, xw_k, xw_v, xw_o = cast(xattn_w_q), cast(xattn_w_k), cast(xattn_w_v), cast(xattn_w_o)
    xpb, xsb, memb = cast(xp), cast(xs), cast(mem)

    kp_l, vp_l, sp_l, shp_l, poolp_l, mkp_l, mvp_l = [], [], [], [], [], [], []
    ks_l, vs_l, ss_l, shs_l, pools_l = [], [], [], [], []
    for l in range(DEPTH):
        if l % 2 == 0:
            e = l // 2
            in_proj = lambda x: (_matmul(x, w_even, e, col_q, A_W), _matmul(x, w_even, e, col_k, A_W),
                                 _matmul(x, w_even, e, col_v, A_W), _matmul(x, w_even, e, col_xb, B_COLS, tn=640),
                                 _matmul(x, w_even, e, col_z, D_MODEL))
            lora = jnp.concatenate([rwkv_w_up[e], rwkv_a_up[e]], axis=0)
            rw = (row2(rwkv_mu[e]), row2(rwkv_w0[e]), row2(rwkv_a0[e]), row2(rwkv_k_k[e]), row2(rwkv_k_a[e]),
                  lora, head_ones)
            gn = (row2(rwkv_r_k[e]), row2(rwkv_gn_g[e]), row2(rwkv_gn_b[e]))

            def rwkv(xb, batch, t_real, t_pad, shift_prev, s0):
                xb3 = xb.reshape(batch, t_real, B_COLS)
                if t_pad != t_real:
                    xb3 = _pad_rows(xb3, t_pad)
                prep = _rwkv_prep(xb3, shift_prev, t_real, *rw)
                y, bd = _rwkv_scan(prep, _state_to_bd(s0), *gn)
                return y[:, :t_real].reshape(batch * t_real, B_W), xb3[:, t_real - 1], _bd_to_state(bd)

            q, k, v, xb, z = in_proj(xpb)
            a_out = _moba_prompt(q, k, v, bias_tiles, t5_t, bp, tp)
            b_out, shp, s_p = rwkv(xb, bp, tp, tp, jnp.zeros((bp, 1, B_COLS), f32),
                                   jnp.zeros((bp, B_HEADS, B_HEAD_DIM, B_HEAD_DIM), f32))
            mp_args = ([a_out, b_out], z, w_oe, e)
            kp_l.append(k.reshape(bp, tp, A_HEADS, A_HEAD_DIM))
            vp_l.append(v.reshape(bp, tp, A_HEADS, A_HEAD_DIM))
            sp_l.append(s_p)
            shp_l.append(shp)

            q, k, v, xb, z = in_proj(xsb)
            pad8 = lambda a: _pad_rows(a.reshape(bs, ts, A_W), SUBLANES)
            q8, k8, v8 = pad8(q), pad8(k), pad8(v)
            top = _moba_topk(q8, _moba_kmean(pt, cache_moba_k, e, bs), bs)
            top = top.reshape(bs, A_HEADS, SUBLANES, LANES)[:, :, :ts, :MOBA_TOPK].reshape(-1)
            a_out = _moba_sample(pt, top, t5_t, q8, k8, v8, cache_moba_k, cache_moba_v, e, bs, ts)
            a_out = a_out[:, :ts].reshape(bs * ts, A_W)
            b_out, shs, s_s = rwkv(xb, bs, ts, RWKV_CHUNK, state_shift[e].reshape(bs, 1, B_COLS), state_rwkv[e])
            ms_args = ([a_out, b_out], z, w_oe, e)
            ks_l.append(k.reshape(bs, ts, A_HEADS, A_HEAD_DIM))
            vs_l.append(v.reshape(bs, ts, A_HEADS, A_HEAD_DIM))
            ss_l.append(s_s)
            shs_l.append(shs)
        else:
            o = l // 2
            sc = row2(pool_scale[o])
            in_proj = lambda x: (_matmul(x, w_odd, o, 0, C_W), _matmul(x, w_odd, o, C_W, D_MODEL))

            xc, z = in_proj(xpb)
            xc3 = xc.reshape(bp, tp, C_W)
            y = _pool(xc3, jnp.zeros((bp, POOL_BUF + 1, C_W), f32), 0, pw, o, sc)
            mp_args = ([y.reshape(bp * tp, C_W)], z, w_oo, o)
            poolp_l.append(xc3[:, tp - POOL_BUF:])

            xc, z = in_proj(xsb)
            xc3 = xc.reshape(bs, ts, C_W)
            buf16 = jnp.pad(state_pool[o], ((0, 0), (1, 0), (0, 0)))
            y = _pool(_pad_rows(xc3, POOL_BUF + 1), buf16, PAST_LEN, pw, o, sc)[:, :ts]
            ms_args = ([y.reshape(bs * ts, C_W)], z, w_oo, o)
            pools_l.append(jnp.concatenate([state_pool[o], xc3], axis=1)[:, -POOL_BUF:])

        g, b = row2(ln_mix_g[l]), row2(ln_mix_b[l])
        xp, xpb = _proj_ln(*mp_args, xp, g, b)
        xs, xsb = _proj_ln(*ms_args, xs, g, b)

        g, b = row2(ln_x_g[l]), row2(ln_x_b[l])
        mk, mv = _matmul(memb, xw_k, l, 0, D_MODEL), _matmul(memb, xw_v, l, 0, D_MODEL)
        mkp_l.append(mk.reshape(bp, MEM_LEN, X_HEADS, X_HEAD_DIM))
        mvp_l.append(mv.reshape(bp, MEM_LEN, X_HEADS, X_HEAD_DIM))
        att = _xattn(_matmul(xpb, xw_q, l, 0, D_MODEL), mk.reshape(bp, MEM_LEN, D_MODEL),
                     mv.reshape(bp, MEM_LEN, D_MODEL), bp, 1024)
        xp, xpb = _proj_ln([att], None, xw_o, l, xp, g, b)
        qs8 = _pad_rows(_matmul(xsb, xw_q, l, 0, D_MODEL).reshape(bs, ts, D_MODEL), SUBLANES)
        att = _xattn(qs8.reshape(bs * SUBLANES, D_MODEL), cache_mem_k[l].reshape(bs, MEM_LEN, D_MODEL),
                     cache_mem_v[l].reshape(bs, MEM_LEN, D_MODEL), bs, SUBLANES)
        att = att.reshape(bs, SUBLANES, D_MODEL)[:, :ts].reshape(bs * ts, D_MODEL)
        xs, xsb = _proj_ln([att], None, xw_o, l, xs, g, b)

    return (xp.reshape(bp, tp, D_MODEL), xs.reshape(bs, ts, D_MODEL),
            jnp.stack(kp_l), jnp.stack(vp_l), jnp.stack(sp_l), jnp.stack(shp_l), jnp.stack(poolp_l),
            jnp.stack(mkp_l), jnp.stack(mvp_l),
            jnp.stack(ks_l), jnp.stack(vs_l), jnp.stack(ss_l), jnp.stack(shs_l), jnp.stack(pools_l))
```

```python
import functools
import math

import numpy as np
import jax
import jax.numpy as jnp
from jax import lax
from jax.experimental import pallas as pl
from jax.experimental.pallas import tpu as pltpu

f32, bf16, i32 = jnp.float32, jnp.bfloat16, jnp.int32

D_MODEL = 2048
DEPTH = 4
PAST_LEN = 16384
PAGE_SIZE = 128
A_HEAD_DIM = 128
A_W = 1024
A_HEADS = 8
MOBA_BLOCK = 256
MOBA_TOPK = 3
B_HEAD_DIM = 64
B_W = 1024
B_HEADS = 16
LORA_W = 64
B_COLS = 3 * B_W + 2 * LORA_W
RWKV_GN_EPS = 64e-5
C_W = 2048
POOL_WINDOWS = (2, 4, 8, 16)
POOL_GW = C_W // len(POOL_WINDOWS)
POOL_BUF = max(POOL_WINDOWS) - 1
MEM_LEN = 256
X_HEADS = 4
X_HEAD_DIM = D_MODEL // X_HEADS
T5_BUCKETS = 32
T5_MAX_DIST = 128
LN_EPS = 1e-5
ALPHA = (2 * DEPTH) ** 0.25

NEG = -1e30
LANES = 128
SUBLANES = 8
RWKV_CHUNK = 64
PAIR = 2 * B_HEAD_DIM
VMEM_LIMIT = 48 * 1024 * 1024

NT = (((1,), (1,)), ((), ()))
TN = (((0,), (0,)), ((), ()))


def _t5_thresholds():
    exact = T5_BUCKETS // 2
    rel = np.arange(0, 4 * T5_MAX_DIST)
    relf = np.maximum(rel, exact).astype(np.float32)
    large = exact + (np.log(relf / np.float32(exact)) / np.float32(math.log(T5_MAX_DIST / exact))
                     * np.float32(T5_BUCKETS - exact)).astype(np.int32)
    bucket = np.where(rel < exact, rel, np.minimum(large, T5_BUCKETS - 1))
    assert (np.diff(bucket) >= 0).all() and bucket[-1] == T5_BUCKETS - 1
    return [int(np.argmax(bucket >= b)) for b in range(T5_BUCKETS)]


T5_THR = _t5_thresholds()


def _dot(a, b, dn=None):
    if dn is None:
        return jnp.dot(a, b, preferred_element_type=f32)
    return lax.dot_general(a, b, dn, preferred_element_type=f32)


def _split(x):
    hi = x.astype(bf16)
    lo = (x - hi.astype(f32)).astype(bf16)
    return hi, lo


def _dot3(a, b, dn=None):
    ah, al = _split(a)
    bh, bl = _split(b)
    return _dot(ah, bh, dn) + _dot(ah, bl, dn) + _dot(al, bh, dn)


def _dot2l(a, b_exact, dn=None):
    ah, al = _split(a)
    return _dot(ah, b_exact, dn) + _dot(al, b_exact, dn)


def _t5_bias_of(rel, tab):
    bias = jnp.full(rel.shape, tab(0), f32)
    for b in range(1, T5_BUCKETS):
        bias = jnp.where(rel >= T5_THR[b], tab(b), bias)
    return bias


def _params(*sem):
    return pltpu.CompilerParams(dimension_semantics=sem, vmem_limit_bytes=VMEM_LIMIT)


def _mm_body(x_ref, w_ref, o_ref):
    o_ref[...] = _dot(x_ref[...], w_ref[0])


def _matmul(x, w, layer, col0, n, tn=512):
    m, k = x.shape
    tm = min(m, 2048)
    assert col0 % LANES == 0 and n % tn == 0 and m % tm == 0
    if col0 % tn == 0:
        wspec = pl.BlockSpec((1, k, tn), lambda i, j: (layer, 0, col0 // tn + j))
    else:
        wspec = pl.BlockSpec((pl.Element(1), pl.Element(k), pl.Element(tn)),
                             lambda i, j: (layer, 0, pl.multiple_of(col0 + j * tn, LANES)))
    return pl.pallas_call(
        _mm_body,
        grid=(m // tm, n // tn),
        in_specs=[pl.BlockSpec((tm, k), lambda i, j: (i, 0)), wspec],
        out_specs=pl.BlockSpec((tm, tn), lambda i, j: (i, j)),
        out_shape=jax.ShapeDtypeStruct((m, n), f32),
        compiler_params=_params("parallel", "arbitrary"),
        name="proj_matmul",
    )(x, w)


def _proj_ln_body(n_lhs, gated, *refs):
    lhs = [r[...] for r in refs[:n_lhs]]
    refs = refs[n_lhs:]
    y = lhs[0] if n_lhs == 1 else jnp.concatenate(lhs, axis=1)
    if gated:
        z = refs[0][...]
        refs = refs[1:]
        y = y * (z * jax.nn.sigmoid(z))
    w_ref, res_ref, g_ref, b_ref, o_ref, ob_ref = refs
    u = ALPHA * res_ref[...] + _dot(y.astype(bf16), w_ref[...])
    mu = jnp.mean(u, axis=-1, keepdims=True)
    d = u - mu
    var = jnp.mean(d * d, axis=-1, keepdims=True)
    out = d * lax.rsqrt(var + LN_EPS) * g_ref[...] + b_ref[...]
    o_ref[...] = out
    ob_ref[...] = out.astype(bf16)


def _proj_ln(lhs, z, w, layer, res, g, b):
    m = res.shape[0]
    tm = min(m, 256)
    row = lambda width: pl.BlockSpec((tm, width), lambda i: (i, 0))
    full = lambda a: pl.BlockSpec(a.shape, lambda i: (0, 0))
    args = list(lhs) + ([z] if z is not None else []) + [w, res, g, b]
    specs = [row(a.shape[1]) for a in lhs] + ([row(z.shape[1])] if z is not None else []) + [
        pl.BlockSpec((None,) + w.shape[1:], lambda i: (layer, 0, 0)), row(D_MODEL), full(g), full(b)]
    return pl.pallas_call(
        functools.partial(_proj_ln_body, len(lhs), z is not None),
        grid=(m // tm,),
        in_specs=specs,
        out_specs=[row(D_MODEL), row(D_MODEL)],
        out_shape=[jax.ShapeDtypeStruct((m, D_MODEL), f32), jax.ShapeDtypeStruct((m, D_MODEL), bf16)],
        compiler_params=_params("parallel"),
        name="proj_ln",
    )(*args)


def _t5_tiles_body(t5_ref, o_ref):
    h = pl.program_id(0)
    key = lax.broadcasted_iota(i32, (MOBA_BLOCK, MOBA_BLOCK), 0)
    qry = lax.broadcasted_iota(i32, (MOBA_BLOCK, MOBA_BLOCK), 1)
    tab = lambda b: t5_ref[h, b]
    rel = qry - key
    o_ref[0, 0] = jnp.where(rel >= 0, _t5_bias_of(jnp.maximum(rel, 0), tab), NEG)
    o_ref[0, 1] = _t5_bias_of(rel + MOBA_BLOCK, tab)


def _t5_tiles(t5_t):
    return pl.pallas_call(
        _t5_tiles_body,
        grid=(A_HEADS,),
        in_specs=[pl.BlockSpec(memory_space=pltpu.SMEM)],
        out_specs=pl.BlockSpec((1, 2, MOBA_BLOCK, MOBA_BLOCK), lambda h: (h, 0, 0, 0)),
        out_shape=jax.ShapeDtypeStruct((A_HEADS, 2, MOBA_BLOCK, MOBA_BLOCK), f32),
        compiler_params=_params("parallel"),
        name="t5_tiles",
    )(t5_t)


def _moba_prompt_body(nb, t5_ref, q_ref, k_ref, v_ref, bias_ref, o_ref):
    blkw = MOBA_BLOCK
    far_bias = t5_ref[pl.program_id(1), T5_BUCKETS - 1]
    k = k_ref[...]
    q = q_ref[...]
    kb = k.astype(bf16)
    vt = v_ref[...].T.astype(bf16)
    km = jnp.concatenate([jnp.mean(k[n * blkw:(n + 1) * blkw], axis=0, keepdims=True) for n in range(nb)], axis=0)
    gate_all = _dot3(km, q, NT)
    qs = (q * (1.0 / math.sqrt(A_HEAD_DIM))).astype(bf16)
    blk = lax.broadcasted_iota(i32, (nb, blkw), 0)
    for qi in range(nb):
        rows = slice(qi * blkw, (qi + 1) * blkw)
        gate = gate_all[:, rows]
        beaten = jnp.zeros(gate.shape, f32)
        for m in range(qi):
            gm = gate[m:m + 1, :]
            beaten = beaten + jnp.where((gm > gate) | ((gm == gate) & (m < blk)), 1.0, 0.0)
        sel = jnp.where((blk < qi) & (beaten < MOBA_TOPK), 0.0, NEG)
        s_all = _dot(kb[:(qi + 1) * blkw], qs[rows], NT)
        tiles = []
        for n in range(qi + 1):
            s = s_all[n * blkw:(n + 1) * blkw]
            if n == qi:
                s = s + bias_ref[0, 0]
            else:
                s = s + sel[n:n + 1, :] + (bias_ref[0, 1] if n == qi - 1 else far_bias)
            tiles.append(s)
        mx = functools.reduce(jnp.maximum, [jnp.max(s, axis=0, keepdims=True) for s in tiles])
        ps = [jnp.exp(s - mx) for s in tiles]
        den = functools.reduce(jnp.add, [jnp.sum(p, axis=0, keepdims=True) for p in ps])
        pcat = jnp.concatenate([p.astype(bf16) for p in ps], axis=0)
        acc = _dot(vt[:, :(qi + 1) * blkw], pcat)
        o_ref[rows, :] = (acc / den).T


def _moba_prompt(q, k, v, bias_tiles, t5_t, batch, seq):
    assert 2 * MOBA_BLOCK - (MOBA_BLOCK - 1) >= T5_THR[-1]
    nb = seq // MOBA_BLOCK
    spec = pl.BlockSpec((seq, A_HEAD_DIM), lambda b, h: (b, h))
    return pl.pallas_call(
        functools.partial(_moba_prompt_body, nb),
        grid=(batch, A_HEADS),
        in_specs=[pl.BlockSpec(memory_space=pltpu.SMEM), spec, spec, spec,
                  pl.BlockSpec((1, 2, MOBA_BLOCK, MOBA_BLOCK), lambda b, h: (h, 0, 0, 0))],
        out_specs=spec,
        out_shape=jax.ShapeDtypeStruct((batch * seq, A_W), f32),
        compiler_params=_params("parallel", "parallel"),
        name="moba_prompt",
    )(t5_t, q, k, v, bias_tiles)


PAGES_PER_SEQ = PAST_LEN // PAGE_SIZE
PAST_BLOCKS = PAST_LEN // MOBA_BLOCK
PAGES_PER_BLOCK = MOBA_BLOCK // PAGE_SIZE
KMEAN_BLOCKS_PER_STEP = SUBLANES


def _kmean_body(pt_ref, *refs):
    pages, o_ref = refs[:-1], refs[-1]
    for jj in range(KMEAN_BLOCKS_PER_STEP):
        s = functools.reduce(jnp.add, [jnp.sum(pages[PAGES_PER_BLOCK * jj + i][0, 0], axis=0)
                                       for i in range(PAGES_PER_BLOCK)])
        s = s * (1.0 / MOBA_BLOCK)
        for h in range(A_HEADS):
            o_ref[0, h, jj:jj + 1, :] = s[h:h + 1, :]


def _moba_kmean(pt, cache_k, layer, batch):
    per_step = PAGES_PER_BLOCK * KMEAN_BLOCKS_PER_STEP
    page = lambda i: pl.BlockSpec((1, 1, PAGE_SIZE, A_HEADS, A_HEAD_DIM),
                                  lambda b, g, pt: (layer, pt[b * PAGES_PER_SEQ + g * per_step + i], 0, 0, 0))
    return pl.pallas_call(
        _kmean_body,
        grid_spec=pltpu.PrefetchScalarGridSpec(
            num_scalar_prefetch=1, grid=(batch, PAST_BLOCKS // KMEAN_BLOCKS_PER_STEP),
            in_specs=[page(i) for i in range(per_step)],
            out_specs=pl.BlockSpec((1, A_HEADS, KMEAN_BLOCKS_PER_STEP, A_HEAD_DIM), lambda b, g, pt: (b, 0, g, 0))),
        out_shape=jax.ShapeDtypeStruct((batch, A_HEADS, PAST_BLOCKS, A_HEAD_DIM), f32),
        compiler_params=_params("parallel", "arbitrary"),
        name="moba_kmean",
    )(pt, *([cache_k] * per_step))


def _moba_topk_body(q_ref, km_ref, o_ref):
    lane = lax.broadcasted_iota(i32, (SUBLANES, LANES), 1)
    for h in range(A_HEADS):
        cols = slice(h * A_HEAD_DIM, (h + 1) * A_HEAD_DIM)
        gate = _dot3(q_ref[0, :, cols], km_ref[0, h], NT)
        blk = lax.broadcasted_iota(i32, gate.shape, 1)
        out = jnp.zeros((SUBLANES, LANES), i32)
        for j in range(MOBA_TOPK):
            best = jnp.max(gate, axis=1, keepdims=True)
            idx = jnp.min(jnp.where(gate == best, blk, PAST_BLOCKS), axis=1, keepdims=True)
            out = jnp.where(lane == j, idx, out)
            gate = jnp.where(blk == idx, -jnp.inf, gate)
        o_ref[0, h * SUBLANES:(h + 1) * SUBLANES, :] = out


def _moba_topk(q8, kmean, batch):
    return pl.pallas_call(
        _moba_topk_body,
        grid=(batch,),
        in_specs=[pl.BlockSpec((1, SUBLANES, A_W), lambda b: (b, 0, 0)),
                  pl.BlockSpec((1, A_HEADS, PAST_BLOCKS, A_HEAD_DIM), lambda b: (b, 0, 0, 0))],
        out_specs=pl.BlockSpec((1, A_HEADS * SUBLANES, LANES), lambda b: (b, 0, 0)),
        out_shape=jax.ShapeDtypeStruct((batch, A_HEADS * SUBLANES, LANES), i32),
        compiler_params=_params("parallel"),
        name="moba_topk",
    )(q8, kmean)


def _moba_sample_body(t_new, layer, pt_ref, top_ref, t5_ref, q_ref, kn_ref, vn_ref, kc_hbm, vc_hbm, o_ref,
                      kbuf, vbuf, sem):
    b, h = pl.program_id(0), pl.program_id(1)
    nh = pl.num_programs(1)
    step = b * nh + h
    nstep = pl.num_programs(0) * nh
    per_q = MOBA_TOPK * PAGES_PER_BLOCK
    top_at = lambda bb, hh, qq, j: top_ref[((bb * A_HEADS + hh) * t_new + qq) * MOBA_TOPK + j]

    def copies(bb, hh, slot):
        out = []
        for qq in range(t_new):
            for j in range(MOBA_TOPK):
                blk = top_at(bb, hh, qq, j)
                for half in range(PAGES_PER_BLOCK):
                    pg = pt_ref[bb * PAGES_PER_SEQ + PAGES_PER_BLOCK * blk + half]
                    i = qq * per_q + j * PAGES_PER_BLOCK + half
                    out.append(pltpu.make_async_copy(kc_hbm.at[layer, pg, :, hh, :], kbuf.at[slot, i], sem.at[slot, 0]))
                    out.append(pltpu.make_async_copy(vc_hbm.at[layer, pg, :, hh, :], vbuf.at[slot, i], sem.at[slot, 1]))
        return out

    slot = step % 2

    @pl.when(step == 0)
    def _():
        for c in copies(b, h, slot):
            c.start()

    @pl.when(step + 1 < nstep)
    def _():
        nxt = step + 1
        for c in copies(nxt // nh, nxt % nh, 1 - slot):
            c.start()

    tab = lambda bkt: t5_ref[h, bkt]
    q = q_ref[0] * (1.0 / math.sqrt(A_HEAD_DIM))
    qb = q.astype(bf16)
    row = lax.broadcasted_iota(i32, (SUBLANES, 1), 0)
    kn, vn = kn_ref[0], vn_ref[0]
    own = []
    for t in range(t_new):
        sc = jnp.sum(q * kn[t:t + 1, :], axis=1, keepdims=True) + _t5_bias_of(jnp.maximum(row - t, 0), tab)
        own.append(jnp.where(row >= t, sc, NEG))
    own_max = functools.reduce(jnp.maximum, own)

    for c in copies(b, h, slot):
        c.wait()

    lane = lax.broadcasted_iota(i32, (1, per_q * PAGE_SIZE), 1)
    l_sum = jnp.zeros((SUBLANES, 1), f32)
    acc = jnp.zeros((SUBLANES, A_HEAD_DIM), f32)
    for qq in range(t_new):
        kq = kbuf[slot, qq * per_q:(qq + 1) * per_q].reshape(per_q * PAGE_SIZE, A_HEAD_DIM)
        vq = vbuf[slot, qq * per_q:(qq + 1) * per_q].reshape(per_q * PAGE_SIZE, A_HEAD_DIM)
        keypos = lane - (MOBA_TOPK - 1) * MOBA_BLOCK + top_at(b, h, qq, MOBA_TOPK - 1) * MOBA_BLOCK
        for j in range(MOBA_TOPK - 2, -1, -1):
            keypos = jnp.where(lane < (j + 1) * MOBA_BLOCK, lane - j * MOBA_BLOCK + top_at(b, h, qq, j) * MOBA_BLOCK,
                               keypos)
        lg = _dot(qb, kq.astype(bf16), NT) + _t5_bias_of(PAST_LEN + qq - keypos, tab)
        mine = row == qq
        m = jnp.maximum(own_max, jnp.max(lg, axis=1, keepdims=True))
        p = jnp.where(mine, jnp.exp(lg - m), 0.0)
        l_sum = l_sum + jnp.sum(p, axis=1, keepdims=True)
        acc = acc + _dot(p.astype(bf16), vq.astype(bf16))
        for t in range(t_new):
            po = jnp.where(mine, jnp.exp(own[t] - m), 0.0)
            l_sum = l_sum + po
            acc = acc + po * vn[t:t + 1, :]
    o_ref[0] = acc / jnp.where(row < t_new, l_sum, 1.0)


def _moba_sample(pt, top, t5_t, q8, kn8, vn8, cache_k, cache_v, layer, batch, t_new):
    npage = t_new * MOBA_TOPK * PAGES_PER_BLOCK
    rows = pl.BlockSpec((1, SUBLANES, A_HEAD_DIM), lambda b, h, pt, top: (b, 0, h))
    hbm = pl.BlockSpec(memory_space=pl.ANY)
    return pl.pallas_call(
        functools.partial(_moba_sample_body, t_new, layer),
        grid_spec=pltpu.PrefetchScalarGridSpec(
            num_scalar_prefetch=2, grid=(batch, A_HEADS),
            in_specs=[pl.BlockSpec(memory_space=pltpu.SMEM), rows, rows, rows, hbm, hbm],
            out_specs=rows,
            scratch_shapes=[pltpu.VMEM((2, npage, PAGE_SIZE, A_HEAD_DIM), f32),
                            pltpu.VMEM((2, npage, PAGE_SIZE, A_HEAD_DIM), f32),
                            pltpu.SemaphoreType.DMA((2, 2))]),
        out_shape=jax.ShapeDtypeStruct((batch, SUBLANES, A_W), f32),
        compiler_params=_params("arbitrary", "arbitrary"),
        name="moba_sample",
    )(pt, top, t5_t, q8, kn8, vn8, cache_k, cache_v)


def _rwkv_prep_body(t_valid, t_total, xb_ref, halo_ref, sp_ref, mu_ref, w0_ref, a0_ref, kk_ref, ka_ref, lora_ref,
                    r_ref, l_ref, a_ref, b_ref, k_ref, v_ref):
    i = pl.program_id(1)
    x = xb_ref[0]
    tt = x.shape[0]
    row = lax.broadcasted_iota(i32, (tt, 1), 0)
    first = jnp.where(i == 0, sp_ref[0], halo_ref[0, SUBLANES - 1:SUBLANES, :])
    prev = jnp.where(row == 0, first, pltpu.roll(x, 1, 0))
    xm = x + (prev - x) * mu_ref[...]
    r, k, v = xm[:, :B_W], xm[:, B_W:2 * B_W], xm[:, 2 * B_W:3 * B_W]
    wa = xm[:, 3 * B_W:]
    lane = lax.broadcasted_iota(i32, wa.shape, 1)
    lw = _dot3(jnp.where(lane < LORA_W, jnp.tanh(wa), 0.0), lora_ref[...])
    la = _dot3(jnp.where(lane < LORA_W, 0.0, wa), lora_ref[...])
    wlog = -math.exp(-0.5) * jax.nn.sigmoid(w0_ref[...] + lw)
    a = jax.nn.sigmoid(a0_ref[...] + la)
    kk = k * kk_ref[...]
    same_head = ((lax.broadcasted_iota(i32, (PAIR, PAIR), 0) < B_HEAD_DIM)
                 == (lax.broadcasted_iota(i32, (PAIR, PAIR), 1) < B_HEAD_DIM))
    sq = kk * kk
    ss = jnp.concatenate(_head_reduce([sq[:, p * PAIR:(p + 1) * PAIR] for p in range(B_W // PAIR)],
                                      jnp.where(same_head, 1.0, 0.0).astype(bf16)), axis=1)
    kk = kk / jnp.maximum(jnp.sqrt(ss), 1e-12)
    kp = k * (1.0 + (a - 1.0) * ka_ref[...])
    outs = (r, wlog, -kk, kk * a, kp, v)
    if t_valid < t_total:
        ok = (i * tt + row) < t_valid
        outs = tuple(jnp.where(ok, o, 0.0) for o in outs)
    for ref, o in zip((r_ref, l_ref, a_ref, b_ref, k_ref, v_ref), outs):
        ref[0] = o


def _rwkv_prep(xb3, shift_prev, t_valid, mu, w0, a0, k_k, k_a, lora):
    bn, tp, _ = xb3.shape
    tt = min(tp, 256)
    halo_blocks = tt // SUBLANES
    tile = lambda width: pl.BlockSpec((1, tt, width), lambda b, i: (b, i, 0))
    vec = lambda a: pl.BlockSpec(a.shape, lambda b, i: (0,) * a.ndim)
    out = jax.ShapeDtypeStruct((bn, tp, B_W), f32)
    return pl.pallas_call(
        functools.partial(_rwkv_prep_body, t_valid, tp),
        grid=(bn, tp // tt),
        in_specs=[tile(B_COLS),
                  pl.BlockSpec((1, SUBLANES, B_COLS), lambda b, i: (b, jnp.maximum(i * halo_blocks - 1, 0), 0)),
                  pl.BlockSpec((1, 1, B_COLS), lambda b, i: (b, 0, 0)),
                  vec(mu), vec(w0), vec(a0), vec(k_k), vec(k_a), vec(lora)],
        out_specs=[tile(B_W)] * 6,
        out_shape=[out] * 6,
        compiler_params=_params("parallel", "arbitrary"),
        name="rwkv_prep",
    )(xb3, xb3, shift_prev, mu, w0, a0, k_k, k_a, lora)


def _stack(p, lo):
    return jnp.concatenate([jnp.where(lo, p, 0.0), jnp.where(lo, 0.0, p)], axis=0)


def _head_reduce(parts, mat):
    n, c = len(parts), parts[0].shape[0]
    halves = [_split(a) for a in parts]
    out = _dot(jnp.concatenate([h for h, _ in halves] + [l for _, l in halves], axis=0), mat)
    return [out[i * c:(i + 1) * c] + out[(n + i) * c:(n + i + 1) * c] for i in range(n)]


def _rwkv_scan_body(npair, r_ref, l_ref, a_ref, b_ref, k_ref, v_ref, s0_ref, rk_ref, gg_ref, gb_ref, y_ref, s_ref, bd_ref):
    hd = B_HEAD_DIM

    @pl.when(pl.program_id(1) == 0)
    def _():
        zero = jnp.zeros((hd, hd), f32)
        for p in range(npair):
            top = jnp.concatenate([s0_ref[0, 2 * p].T, zero], axis=1)
            bot = jnp.concatenate([zero, s0_ref[0, 2 * p + 1].T], axis=1)
            bd_ref[p] = jnp.concatenate([top, bot], axis=0)

    c = RWKV_CHUNK
    pairs = range(npair)
    col = lambda x, p: x[:, p * PAIR:(p + 1) * PAIR]
    lane = lax.broadcasted_iota(i32, (c, PAIR), 1)
    lo = lane < B_HEAD_DIM
    t = lax.broadcasted_iota(i32, (c, PAIR), 0)
    j = lane & (B_HEAD_DIM - 1)
    strict, incl = t > j, t >= j
    eye_c = jnp.where(t == j, 1.0, 0.0)
    rowi = lax.broadcasted_iota(i32, (PAIR, PAIR), 0)
    coli = lax.broadcasted_iota(i32, (PAIR, PAIR), 1)
    same_head = (rowi < B_HEAD_DIM) == (coli < B_HEAD_DIM)
    head_sum = jnp.where(same_head, 1.0, 0.0).astype(bf16)
    head_mean = jnp.where(same_head, 1.0 / B_HEAD_DIM, 0.0).astype(bf16)
    st = lambda x: _stack(x, lo).astype(bf16)
    cast = lambda x: x.astype(bf16)

    R, L, A, B, K, V = (ref[0] for ref in (r_ref, l_ref, a_ref, b_ref, k_ref, v_ref))
    tri = jnp.where(lax.broadcasted_iota(i32, (c, c), 0) >= lax.broadcasted_iota(i32, (c, c), 1), 1.0, 0.0).astype(bf16)
    l1, l2 = _split(L)
    l3 = (L - l1.astype(f32) - l2.astype(f32)).astype(bf16)
    cum = _dot(tri, l1) + _dot(tri, l2) + _dot(tri, l3)
    last = cum[c - 1:c, :]
    wc = jnp.exp(last)
    At, Rt = A * jnp.exp(cum - L), R * jnp.exp(cum)
    einv, ew = jnp.exp(-cum), jnp.exp(last - cum)
    Bt, Kt, Bw, Kw = B * einv, K * einv, B * ew, K * ew

    ar = [cast(jnp.concatenate([col(At, p), col(Rt, p)], axis=0)) for p in pairs]
    pb = [_dot(ar[p], st(col(Bt, p)), NT) for p in pairs]
    pk = [_dot(ar[p], st(col(Kt, p)), NT) for p in pairs]
    pab = [jnp.where(strict, x[:c], 0.0) for x in pb]
    prb = [jnp.where(incl, x[c:], 0.0) for x in pb]
    pak = [jnp.where(strict, x[:c], 0.0) for x in pk]
    prk = [jnp.where(incl, x[c:], 0.0) for x in pk]
    x = [eye_c + m for m in pab]
    qp = pab
    for _ in range(int(math.log2(c)) - 1):
        qp = [_dot(cast(m), st(m)) for m in qp]
        x = [xi + _dot(cast(xi), st(m)) for xi, m in zip(x, qp)]
    akv = [_dot(cast(pak[p]), st(col(V, p))) for p in pairs]
    ua = [_dot(cast(x[p]), jnp.concatenate([st(akv[p]), st(col(At, p))], axis=1)) for p in pairs]
    u0 = [m[:, :PAIR] for m in ua]
    ah = [m[:, PAIR:] for m in ua]
    ry = [_dot(cast(prb[p]), jnp.concatenate([st(ah[p]), st(u0[p])], axis=1)) for p in pairs]
    rh = [col(Rt, p) + ry[p][:, :PAIR] for p in pairs]
    y0 = [ry[p][:, PAIR:] + _dot(cast(prk[p]), st(col(V, p))) for p in pairs]
    z1 = [_dot(cast(col(Bw, p)), cast(jnp.concatenate([ah[p], u0[p]], axis=1)), TN) for p in pairs]
    z2 = [_dot(cast(col(Kw, p)), cast(col(V, p)), TN) for p in pairs]
    mt = [jnp.where(rowi == coli, col(wc, p), 0.0) + jnp.where(same_head, z1[p][:, :PAIR], 0.0) for p in pairs]
    gt = [jnp.where(same_head, z1[p][:, PAIR:] + z2[p], 0.0) for p in pairs]
    nl = c + PAIR
    lhs = [_split(jnp.concatenate([rh[p], mt[p]], axis=0)) for p in pairs]
    bd = [_split(bd_ref[p]) for p in pairs]
    main = [_dot(jnp.concatenate(lhs[p], axis=0), bd[p][0]) for p in pairs]
    corr = [_dot(lhs[p][0], bd[p][1]) for p in pairs]
    prod = [main[p][:nl] + main[p][nl:] + corr[p] for p in pairs]
    new_bd = [prod[p][c:] + gt[p] for p in pairs]
    for p in pairs:
        bd_ref[p] = new_bd[p]

    @pl.when(pl.program_id(1) == pl.num_programs(1) - 1)
    def _():
        for p in pairs:
            s_ref[0, 2 * p] = new_bd[p][:hd, :hd].T
            s_ref[0, 2 * p + 1] = new_bd[p][hd:, hd:].T

    y = [prod[p][:c] + y0[p] for p in pairs]
    mean = _head_reduce(y, head_mean)
    d = [yi - m for yi, m in zip(y, mean)]
    var = _head_reduce([di * di for di in d], head_mean)
    bonus = _head_reduce([col(R, p) * col(K, p) * col(rk_ref[...], p) for p in pairs], head_sum)
    for p in pairs:
        yn = d[p] * lax.rsqrt(var[p] + RWKV_GN_EPS) * col(gg_ref[...], p) + col(gb_ref[...], p)
        y_ref[0, :, p * PAIR:(p + 1) * PAIR] = yn + bonus[p] * col(V, p)


def _rwkv_scan(prep, s0, r_k, gn_g, gn_b):
    bn, tp, _ = prep[0].shape
    npair = B_HEADS // 2
    tile = pl.BlockSpec((1, RWKV_CHUNK, B_W), lambda b, c: (b, c, 0))
    state = pl.BlockSpec((1, B_HEADS, B_HEAD_DIM, B_HEAD_DIM), lambda b, c: (b, 0, 0, 0))
    vec = pl.BlockSpec((1, B_W), lambda b, c: (0, 0))
    return pl.pallas_call(
        functools.partial(_rwkv_scan_body, npair),
        grid=(bn, tp // RWKV_CHUNK),
        in_specs=[tile] * 6 + [state, vec, vec, vec],
        out_specs=[tile, state],
        out_shape=[jax.ShapeDtypeStruct((bn, tp, B_W), f32), jax.ShapeDtypeStruct(s0.shape, f32)],
        scratch_shapes=[pltpu.VMEM((npair, PAIR, PAIR), f32)],
        compiler_params=_params("parallel", "arbitrary"),
        name="rwkv_scan",
    )(*prep, s0, r_k, gn_g, gn_b)


def _pool_body(pos0, x_ref, halo_ref, buf_ref, pw_ref, sc_ref, o_ref):
    i = pl.program_id(1)
    x = x_ref[0]
    tt = x.shape[0]
    halo = jnp.where(i == 0, buf_ref[0], halo_ref[0])
    xe = jnp.concatenate([halo, x], axis=0)
    row = lax.broadcasted_iota(i32, (tt, 1), 0)
    pos = pos0 + i * tt + row
    outs = []
    for g, w in enumerate(POOL_WINDOWS):
        cols = slice(g * POOL_GW, (g + 1) * POOL_GW)
        s = xe[:, cols]
        sh = 1
        while sh < w:
            s = s + pltpu.roll(s, sh, 0)
            sh *= 2
        cnt = jnp.minimum(w, pos + 1).astype(f32)
        pooled = s[POOL_BUF + 1:] / cnt - x[:, cols]
        outs.append(_dot(pooled.astype(bf16), pw_ref[g]))
    o_ref[0] = jnp.concatenate(outs, axis=1) * sc_ref[...]


def _pool(xc3, buf16, pos0, pool_w, layer, scale):
    bn, tp, _ = xc3.shape
    tt = min(tp, 256)
    hb = POOL_BUF + 1
    return pl.pallas_call(
        functools.partial(_pool_body, pos0),
        grid=(bn, tp // tt),
        in_specs=[pl.BlockSpec((1, tt, C_W), lambda b, i: (b, i, 0)),
                  pl.BlockSpec((1, hb, C_W), lambda b, i: (b, jnp.maximum(i * (tt // hb) - 1, 0), 0)),
                  pl.BlockSpec((1, hb, C_W), lambda b, i: (b, 0, 0)),
                  pl.BlockSpec((None,) + pool_w.shape[1:], lambda b, i: (layer, 0, 0, 0)),
                  pl.BlockSpec(scale.shape, lambda b, i: (0, 0))],
        out_specs=pl.BlockSpec((1, tt, C_W), lambda b, i: (b, i, 0)),
        out_shape=jax.ShapeDtypeStruct((bn, tp, C_W), f32),
        compiler_params=_params("parallel", "arbitrary"),
        name="pool_mix",
    )(xc3, xc3, buf16, pool_w, scale)


def _mem_proj_body(x_ref, w_ref, prev_ref, o_ref):
    del prev_ref
    y = _dot(x_ref[...], w_ref[0])
    for h in range(X_HEADS):
        o_ref[:, h, :] = y[:, h * X_HEAD_DIM:(h + 1) * X_HEAD_DIM]


def _mem_proj(x, w, layer, prev, batch):
    return pl.pallas_call(
        _mem_proj_body,
        grid=(batch,),
        in_specs=[pl.BlockSpec((MEM_LEN, D_MODEL), lambda b: (b, 0)),
                  pl.BlockSpec((1,) + w.shape[1:], lambda b: (layer, 0, 0)),
                  pl.BlockSpec(memory_space=pl.ANY)],
        out_specs=pl.BlockSpec((None, None, MEM_LEN, X_HEADS, X_HEAD_DIM), lambda b: (layer, b, 0, 0, 0)),
        out_shape=jax.ShapeDtypeStruct(prev.shape, f32),
        input_output_aliases={2: 0},
        compiler_params=_params("parallel"),
        name="mem_proj",
    )(x, w, prev)


def _xattn_body(q_ref, k_ref, v_ref, o_ref):
    for h in range(X_HEADS):
        cols = slice(h * X_HEAD_DIM, (h + 1) * X_HEAD_DIM)
        q = (q_ref[:, cols] * (1.0 / math.sqrt(X_HEAD_DIM))).astype(bf16)
        s = _dot(q, k_ref[:, h, :].astype(bf16), NT)
        p = jnp.exp(s - jnp.max(s, axis=-1, keepdims=True))
        o_ref[:, cols] = _dot(p.astype(bf16), v_ref[:, h, :].astype(bf16)) / jnp.sum(p, axis=-1, keepdims=True)


def _xattn(q, mk, mv, layer, batch, tq):
    rows = q.shape[0] // batch
    nt = rows // tq
    qspec = pl.BlockSpec((tq, D_MODEL), lambda b, i: (b * nt + i, 0))
    mspec = pl.BlockSpec((None, None, MEM_LEN, X_HEADS, X_HEAD_DIM), lambda b, i: (layer, b, 0, 0, 0))
    return pl.pallas_call(
        _xattn_body,
        grid=(batch, nt),
        in_specs=[qspec, mspec, mspec],
        out_specs=qspec,
        out_shape=jax.ShapeDtypeStruct(q.shape, f32),
        compiler_params=_params("parallel", "parallel"),
        name="mem_xattn",
    )(q, mk, mv)


def _pad_rows(x3, rows):
    return jnp.pad(x3, ((0, 0), (0, rows - x3.shape[1]), (0, 0)))


def kernel(x_prompt, x_sample, cache_moba_k, cache_moba_v, page_table, state_rwkv, state_shift, state_pool, cache_mem_k, cache_mem_v, mem_prompt, w_in_even, w_out_even, rwkv_mu, rwkv_w0, rwkv_w_up, rwkv_a0, rwkv_a_up, rwkv_k_k, rwkv_k_a, rwkv_r_k, rwkv_gn_g, rwkv_gn_b, t5_bias, w_in_odd, pool_w, pool_scale, w_out_odd, xattn_w_q, xattn_w_k, xattn_w_v, xattn_w_o, ln_mix_g, ln_mix_b, ln_x_g, ln_x_b):
    bp, tp, _ = x_prompt.shape
    bs, ts, _ = x_sample.shape
    xp = x_prompt.reshape(bp * tp, D_MODEL)
    xs = x_sample.reshape(bs * ts, D_MODEL)
    mem = mem_prompt.reshape(bp * MEM_LEN, D_MODEL)
    pt = page_table.reshape(-1)
    t5_t = t5_bias.T
    bias_tiles = _t5_tiles(t5_t)
    row2 = lambda a: a.reshape(1, -1)
    cast = lambda a: a.astype(bf16)
    col_q, col_k, col_v, col_xb, col_z = 0, A_W, 2 * A_W, 3 * A_W, 3 * A_W + B_COLS
    w_even, w_odd, w_oe, w_oo, pw = cast(w_in_even), cast(w_in_odd), cast(w_out_even), cast(w_out_odd), cast(pool_w)
    xw_q, xw_k, xw_v, xw_o = cast(xattn_w_q), cast(xattn_w_k), cast(xattn_w_v), cast(xattn_w_o)
    xpb, xsb, memb = cast(xp), cast(xs), cast(mem)

    kp_l, vp_l, sp_l, shp_l, poolp_l = [], [], [], [], []
    mem_k = mem_v = jnp.zeros((DEPTH, bp, MEM_LEN, X_HEADS, X_HEAD_DIM), f32)
    ks_l, vs_l, ss_l, shs_l, pools_l = [], [], [], [], []
    for l in range(DEPTH):
        if l % 2 == 0:
            e = l // 2
            in_proj = lambda x: (_matmul(x, w_even, e, col_q, A_W), _matmul(x, w_even, e, col_k, A_W),
                                 _matmul(x, w_even, e, col_v, A_W), _matmul(x, w_even, e, col_xb, B_COLS, tn=640),
                                 _matmul(x, w_even, e, col_z, D_MODEL))
            lora = jnp.concatenate([rwkv_w_up[e], rwkv_a_up[e]], axis=0)
            rw = (row2(rwkv_mu[e]), row2(rwkv_w0[e]), row2(rwkv_a0[e]), row2(rwkv_k_k[e]), row2(rwkv_k_a[e]),
                  lora)
            gn = (row2(rwkv_r_k[e]), row2(rwkv_gn_g[e]), row2(rwkv_gn_b[e]))

            def rwkv(xb, batch, t_real, t_pad, shift_prev, s0):
                xb3 = xb.reshape(batch, t_real, B_COLS)
                if t_pad != t_real:
                    xb3 = _pad_rows(xb3, t_pad)
                prep = _rwkv_prep(xb3, shift_prev, t_real, *rw)
                y, s_new = _rwkv_scan(prep, s0, *gn)
                return y[:, :t_real].reshape(batch * t_real, B_W), xb3[:, t_real - 1], s_new

            q, k, v, xb, z = in_proj(xpb)
            a_out = _moba_prompt(q, k, v, bias_tiles, t5_t, bp, tp)
            b_out, shp, s_p = rwkv(xb, bp, tp, tp, jnp.zeros((bp, 1, B_COLS), f32),
                                   jnp.zeros((bp, B_HEADS, B_HEAD_DIM, B_HEAD_DIM), f32))
            mp_args = ([a_out, b_out], z, w_oe, e)
            kp_l.append(k.reshape(bp, tp, A_HEADS, A_HEAD_DIM))
            vp_l.append(v.reshape(bp, tp, A_HEADS, A_HEAD_DIM))
            sp_l.append(s_p)
            shp_l.append(shp)

            q, k, v, xb, z = in_proj(xsb)
            pad8 = lambda a: _pad_rows(a.reshape(bs, ts, A_W), SUBLANES)
            q8, k8, v8 = pad8(q), pad8(k), pad8(v)
            top = _moba_topk(q8, _moba_kmean(pt, cache_moba_k, e, bs), bs)
            top = top.reshape(bs, A_HEADS, SUBLANES, LANES)[:, :, :ts, :MOBA_TOPK].reshape(-1)
            a_out = _moba_sample(pt, top, t5_t, q8, k8, v8, cache_moba_k, cache_moba_v, e, bs, ts)
            a_out = a_out[:, :ts].reshape(bs * ts, A_W)
            b_out, shs, s_s = rwkv(xb, bs, ts, RWKV_CHUNK, state_shift[e].reshape(bs, 1, B_COLS), state_rwkv[e])
            ms_args = ([a_out, b_out], z, w_oe, e)
            ks_l.append(k.reshape(bs, ts, A_HEADS, A_HEAD_DIM))
            vs_l.append(v.reshape(bs, ts, A_HEADS, A_HEAD_DIM))
            ss_l.append(s_s)
            shs_l.append(shs)
        else:
            o = l // 2
            sc = row2(pool_scale[o])
            in_proj = lambda x: (_matmul(x, w_odd, o, 0, C_W), _matmul(x, w_odd, o, C_W, D_MODEL))

            xc, z = in_proj(xpb)
            xc3 = xc.reshape(bp, tp, C_W)
            y = _pool(xc3, jnp.zeros((bp, POOL_BUF + 1, C_W), f32), 0, pw, o, sc)
            mp_args = ([y.reshape(bp * tp, C_W)], z, w_oo, o)
            poolp_l.append(xc3[:, tp - POOL_BUF:])

            xc, z = in_proj(xsb)
            xc3 = xc.reshape(bs, ts, C_W)
            buf16 = jnp.pad(state_pool[o], ((0, 0), (1, 0), (0, 0)))
            y = _pool(_pad_rows(xc3, POOL_BUF + 1), buf16, PAST_LEN, pw, o, sc)[:, :ts]
            ms_args = ([y.reshape(bs * ts, C_W)], z, w_oo, o)
            pools_l.append(jnp.concatenate([state_pool[o], xc3], axis=1)[:, -POOL_BUF:])

        g, b = row2(ln_mix_g[l]), row2(ln_mix_b[l])
        xp, xpb = _proj_ln(*mp_args, xp, g, b)
        xs, xsb = _proj_ln(*ms_args, xs, g, b)

        g, b = row2(ln_x_g[l]), row2(ln_x_b[l])
        mem_k, mem_v = _mem_proj(memb, xw_k, l, mem_k, bp), _mem_proj(memb, xw_v, l, mem_v, bp)
        att = _xattn(_matmul(xpb, xw_q, l, 0, D_MODEL), mem_k, mem_v, l, bp, 512)
        xp, xpb = _proj_ln([att], None, xw_o, l, xp, g, b)
        qs8 = _pad_rows(_matmul(xsb, xw_q, l, 0, D_MODEL).reshape(bs, ts, D_MODEL), SUBLANES)
        att = _xattn(qs8.reshape(bs * SUBLANES, D_MODEL), cache_mem_k, cache_mem_v, l, bs, SUBLANES)
        att = att.reshape(bs, SUBLANES, D_MODEL)[:, :ts].reshape(bs * ts, D_MODEL)
        xs, xsb = _proj_ln([att], None, xw_o, l, xs, g, b)

    return (xp.reshape(bp, tp, D_MODEL), xs.reshape(bs, ts, D_MODEL),
            jnp.stack(kp_l), jnp.stack(vp_l), jnp.stack(sp_l), jnp.stack(shp_l), jnp.stack(poolp_l),
            mem_k, mem_v,
            jnp.stack(ks_l), jnp.stack(vs_l), jnp.stack(ss_l), jnp.stack(shs_l), jnp.stack(pools_l))
```

```python
import functools
import math

import numpy as np
import jax
import jax.numpy as jnp
from jax import lax
from jax.experimental import pallas as pl
from jax.experimental.pallas import tpu as pltpu

f32, bf16, i32 = jnp.float32, jnp.bfloat16, jnp.int32

D_MODEL = 2048
DEPTH = 4
PAST_LEN = 16384
PAGE_SIZE = 128
A_HEAD_DIM = 128
A_W = 1024
A_HEADS = 8
MOBA_BLOCK = 256
MOBA_TOPK = 3
B_HEAD_DIM = 64
B_W = 1024
B_HEADS = 16
LORA_W = 64
B_COLS = 3 * B_W + 2 * LORA_W
RWKV_GN_EPS = 64e-5
C_W = 2048
POOL_WINDOWS = (2, 4, 8, 16)
POOL_GW = C_W // len(POOL_WINDOWS)
POOL_BUF = max(POOL_WINDOWS) - 1
MEM_LEN = 256
X_HEADS = 4
X_HEAD_DIM = D_MODEL // X_HEADS
T5_BUCKETS = 32
T5_MAX_DIST = 128
LN_EPS = 1e-5
ALPHA = (2 * DEPTH) ** 0.25

NEG = -1e30
LANES = 128
SUBLANES = 8
RWKV_CHUNK = 64
PAIR = 2 * B_HEAD_DIM
VMEM_LIMIT = 48 * 1024 * 1024

NT = (((1,), (1,)), ((), ()))
TN = (((0,), (0,)), ((), ()))


def _t5_thresholds():
    exact = T5_BUCKETS // 2
    rel = np.arange(0, 4 * T5_MAX_DIST)
    relf = np.maximum(rel, exact).astype(np.float32)
    large = exact + (np.log(relf / np.float32(exact)) / np.float32(math.log(T5_MAX_DIST / exact))
                     * np.float32(T5_BUCKETS - exact)).astype(np.int32)
    bucket = np.where(rel < exact, rel, np.minimum(large, T5_BUCKETS - 1))
    assert (np.diff(bucket) >= 0).all() and bucket[-1] == T5_BUCKETS - 1
    return [int(np.argmax(bucket >= b)) for b in range(T5_BUCKETS)]


T5_THR = _t5_thresholds()


def _dot(a, b, dn=None):
    if dn is None:
        return jnp.dot(a, b, preferred_element_type=f32)
    return lax.dot_general(a, b, dn, preferred_element_type=f32)


def _split(x):
    hi = x.astype(bf16)
    lo = (x - hi.astype(f32)).astype(bf16)
    return hi, lo


def _dot3(a, b, dn=None):
    ah, al = _split(a)
    bh, bl = _split(b)
    return _dot(ah, bh, dn) + _dot(ah, bl, dn) + _dot(al, bh, dn)


def _dot2l(a, b_exact, dn=None):
    ah, al = _split(a)
    return _dot(ah, b_exact, dn) + _dot(al, b_exact, dn)


def _t5_bias_of(rel, tab):
    bias = jnp.full(rel.shape, tab(0), f32)
    for b in range(1, T5_BUCKETS):
        bias = jnp.where(rel >= T5_THR[b], tab(b), bias)
    return bias


def _params(*sem):
    return pltpu.CompilerParams(dimension_semantics=sem, vmem_limit_bytes=VMEM_LIMIT)


def _mm_body(x_ref, w_ref, o_ref):
    o_ref[...] = _dot(x_ref[...], w_ref[0])


def _matmul(x, w, layer, col0, n, tn=512):
    m, k = x.shape
    tm = min(m, 2048)
    assert col0 % LANES == 0 and n % tn == 0 and m % tm == 0
    if col0 % tn == 0:
        wspec = pl.BlockSpec((1, k, tn), lambda i, j: (layer, 0, col0 // tn + j))
    else:
        wspec = pl.BlockSpec((pl.Element(1), pl.Element(k), pl.Element(tn)),
                             lambda i, j: (layer, 0, pl.multiple_of(col0 + j * tn, LANES)))
    return pl.pallas_call(
        _mm_body,
        grid=(m // tm, n // tn),
        in_specs=[pl.BlockSpec((tm, k), lambda i, j: (i, 0)), wspec],
        out_specs=pl.BlockSpec((tm, tn), lambda i, j: (i, j)),
        out_shape=jax.ShapeDtypeStruct((m, n), f32),
        compiler_params=_params("parallel", "arbitrary"),
        name="proj_matmul",
    )(x, w)


def _proj_ln_body(n_lhs, gated, *refs):
    lhs = [r[...] for r in refs[:n_lhs]]
    refs = refs[n_lhs:]
    y = lhs[0] if n_lhs == 1 else jnp.concatenate(lhs, axis=1)
    if gated:
        z = refs[0][...]
        refs = refs[1:]
        y = y * (z * jax.nn.sigmoid(z))
    w_ref, res_ref, g_ref, b_ref, o_ref, ob_ref = refs
    u = ALPHA * res_ref[...] + _dot(y.astype(bf16), w_ref[...])
    mu = jnp.mean(u, axis=-1, keepdims=True)
    d = u - mu
    var = jnp.mean(d * d, axis=-1, keepdims=True)
    out = d * lax.rsqrt(var + LN_EPS) * g_ref[...] + b_ref[...]
    o_ref[...] = out
    ob_ref[...] = out.astype(bf16)


def _proj_ln(lhs, z, w, layer, res, g, b):
    m = res.shape[0]
    tm = min(m, 256)
    row = lambda width: pl.BlockSpec((tm, width), lambda i: (i, 0))
    full = lambda a: pl.BlockSpec(a.shape, lambda i: (0, 0))
    args = list(lhs) + ([z] if z is not None else []) + [w, res, g, b]
    specs = [row(a.shape[1]) for a in lhs] + ([row(z.shape[1])] if z is not None else []) + [
        pl.BlockSpec((None,) + w.shape[1:], lambda i: (layer, 0, 0)), row(D_MODEL), full(g), full(b)]
    return pl.pallas_call(
        functools.partial(_proj_ln_body, len(lhs), z is not None),
        grid=(m // tm,),
        in_specs=specs,
        out_specs=[row(D_MODEL), row(D_MODEL)],
        out_shape=[jax.ShapeDtypeStruct((m, D_MODEL), f32), jax.ShapeDtypeStruct((m, D_MODEL), bf16)],
        compiler_params=_params("parallel"),
        name="proj_ln",
    )(*args)


def _t5_tiles_body(t5_ref, o_ref):
    h = pl.program_id(0)
    key = lax.broadcasted_iota(i32, (MOBA_BLOCK, MOBA_BLOCK), 0)
    qry = lax.broadcasted_iota(i32, (MOBA_BLOCK, MOBA_BLOCK), 1)
    tab = lambda b: t5_ref[h, b]
    rel = qry - key
    o_ref[0, 0] = jnp.where(rel >= 0, _t5_bias_of(jnp.maximum(rel, 0), tab), NEG)
    o_ref[0, 1] = _t5_bias_of(rel + MOBA_BLOCK, tab)


def _t5_tiles(t5_t):
    return pl.pallas_call(
        _t5_tiles_body,
        grid=(A_HEADS,),
        in_specs=[pl.BlockSpec(memory_space=pltpu.SMEM)],
        out_specs=pl.BlockSpec((1, 2, MOBA_BLOCK, MOBA_BLOCK), lambda h: (h, 0, 0, 0)),
        out_shape=jax.ShapeDtypeStruct((A_HEADS, 2, MOBA_BLOCK, MOBA_BLOCK), f32),
        compiler_params=_params("parallel"),
        name="t5_tiles",
    )(t5_t)


def _moba_prompt_body(nb, t5_ref, q_ref, k_ref, v_ref, bias_ref, o_ref):
    blkw = MOBA_BLOCK
    far_bias = t5_ref[pl.program_id(1), T5_BUCKETS - 1]
    k = k_ref[...]
    q = q_ref[...]
    kb = k.astype(bf16)
    vt = v_ref[...].T.astype(bf16)
    km = jnp.concatenate([jnp.mean(k[n * blkw:(n + 1) * blkw], axis=0, keepdims=True) for n in range(nb)], axis=0)
    gate_all = _dot3(km, q, NT)
    qs = (q * (1.0 / math.sqrt(A_HEAD_DIM))).astype(bf16)
    blk = lax.broadcasted_iota(i32, (nb, blkw), 0)

    def scores(qi):
        rows = slice(qi * blkw, (qi + 1) * blkw)
        gate = gate_all[:, rows]
        beaten = jnp.zeros(gate.shape, f32)
        for m in range(qi):
            gm = gate[m:m + 1, :]
            beaten = beaten + jnp.where((gm > gate) | ((gm == gate) & (m < blk)), 1.0, 0.0)
        sel = jnp.where((blk < qi) & (beaten < MOBA_TOPK), 0.0, NEG)
        s_all = _dot(kb[:(qi + 1) * blkw], qs[rows], NT)
        tiles = []
        for n in range(qi + 1):
            s = s_all[n * blkw:(n + 1) * blkw]
            if n == qi:
                s = s + bias_ref[0, 0]
            elif n == qi - 1:
                s = s + sel[n:n + 1, :] + bias_ref[0, 1]
            else:
                s = s + (sel[n:n + 1, :] + far_bias)
            tiles.append(s)
        return tiles

    def attend(qi, tiles):
        mx = functools.reduce(jnp.maximum, [jnp.max(s, axis=0, keepdims=True) for s in tiles])
        ps = [jnp.exp(s - mx) for s in tiles]
        den = functools.reduce(jnp.add, [jnp.sum(p, axis=0, keepdims=True) for p in ps])
        pcat = jnp.concatenate([p.astype(bf16) for p in ps], axis=0)
        acc = _dot(vt[:, :(qi + 1) * blkw], pcat)
        o_ref[qi * blkw:(qi + 1) * blkw, :] = (acc / den).T

    pending = scores(0)
    for qi in range(nb):
        ahead = scores(qi + 1) if qi + 1 < nb else None
        attend(qi, pending)
        pending = ahead


def _moba_prompt(q, k, v, bias_tiles, t5_t, batch, seq):
    assert 2 * MOBA_BLOCK - (MOBA_BLOCK - 1) >= T5_THR[-1]
    nb = seq // MOBA_BLOCK
    spec = pl.BlockSpec((seq, A_HEAD_DIM), lambda b, h: (b, h))
    return pl.pallas_call(
        functools.partial(_moba_prompt_body, nb),
        grid=(batch, A_HEADS),
        in_specs=[pl.BlockSpec(memory_space=pltpu.SMEM), spec, spec, spec,
                  pl.BlockSpec((1, 2, MOBA_BLOCK, MOBA_BLOCK), lambda b, h: (h, 0, 0, 0))],
        out_specs=spec,
        out_shape=jax.ShapeDtypeStruct((batch * seq, A_W), f32),
        compiler_params=_params("parallel", "parallel"),
        name="moba_prompt",
    )(t5_t, q, k, v, bias_tiles)


PAGES_PER_SEQ = PAST_LEN // PAGE_SIZE
PAST_BLOCKS = PAST_LEN // MOBA_BLOCK
PAGES_PER_BLOCK = MOBA_BLOCK // PAGE_SIZE
KMEAN_BLOCKS_PER_STEP = SUBLANES


def _kmean_body(pt_ref, *refs):
    pages, o_ref = refs[:-1], refs[-1]
    for jj in range(KMEAN_BLOCKS_PER_STEP):
        s = functools.reduce(jnp.add, [jnp.sum(pages[PAGES_PER_BLOCK * jj + i][0, 0], axis=0)
                                       for i in range(PAGES_PER_BLOCK)])
        s = s * (1.0 / MOBA_BLOCK)
        for h in range(A_HEADS):
            o_ref[0, h, jj:jj + 1, :] = s[h:h + 1, :]


def _moba_kmean(pt, cache_k, layer, batch):
    per_step = PAGES_PER_BLOCK * KMEAN_BLOCKS_PER_STEP
    page = lambda i: pl.BlockSpec((1, 1, PAGE_SIZE, A_HEADS, A_HEAD_DIM),
                                  lambda b, g, pt: (layer, pt[b * PAGES_PER_SEQ + g * per_step + i], 0, 0, 0))
    return pl.pallas_call(
        _kmean_body,
        grid_spec=pltpu.PrefetchScalarGridSpec(
            num_scalar_prefetch=1, grid=(batch, PAST_BLOCKS // KMEAN_BLOCKS_PER_STEP),
            in_specs=[page(i) for i in range(per_step)],
            out_specs=pl.BlockSpec((1, A_HEADS, KMEAN_BLOCKS_PER_STEP, A_HEAD_DIM), lambda b, g, pt: (b, 0, g, 0))),
        out_shape=jax.ShapeDtypeStruct((batch, A_HEADS, PAST_BLOCKS, A_HEAD_DIM), f32),
        compiler_params=_params("parallel", "arbitrary"),
        name="moba_kmean",
    )(pt, *([cache_k] * per_step))


def _moba_topk_body(q_ref, km_ref, o_ref):
    lane = lax.broadcasted_iota(i32, (SUBLANES, LANES), 1)
    for h in range(A_HEADS):
        cols = slice(h * A_HEAD_DIM, (h + 1) * A_HEAD_DIM)
        gate = _dot3(q_ref[0, :, cols], km_ref[0, h], NT)
        blk = lax.broadcasted_iota(i32, gate.shape, 1)
        out = jnp.zeros((SUBLANES, LANES), i32)
        for j in range(MOBA_TOPK):
            best = jnp.max(gate, axis=1, keepdims=True)
            idx = jnp.min(jnp.where(gate == best, blk, PAST_BLOCKS), axis=1, keepdims=True)
            out = jnp.where(lane == j, idx, out)
            gate = jnp.where(blk == idx, -jnp.inf, gate)
        o_ref[0, h * SUBLANES:(h + 1) * SUBLANES, :] = out


def _moba_topk(q8, kmean, batch):
    return pl.pallas_call(
        _moba_topk_body,
        grid=(batch,),
        in_specs=[pl.BlockSpec((1, SUBLANES, A_W), lambda b: (b, 0, 0)),
                  pl.BlockSpec((1, A_HEADS, PAST_BLOCKS, A_HEAD_DIM), lambda b: (b, 0, 0, 0))],
        out_specs=pl.BlockSpec((1, A_HEADS * SUBLANES, LANES), lambda b: (b, 0, 0)),
        out_shape=jax.ShapeDtypeStruct((batch, A_HEADS * SUBLANES, LANES), i32),
        compiler_params=_params("parallel"),
        name="moba_topk",
    )(q8, kmean)


def _moba_sample_body(t_new, layer, pt_ref, top_ref, t5_ref, q_ref, kn_ref, vn_ref, kc_hbm, vc_hbm, o_ref,
                      kbuf, vbuf, sem):
    b, h = pl.program_id(0), pl.program_id(1)
    nh = pl.num_programs(1)
    step = b * nh + h
    nstep = pl.num_programs(0) * nh
    per_q = MOBA_TOPK * PAGES_PER_BLOCK
    top_at = lambda bb, hh, qq, j: top_ref[((bb * A_HEADS + hh) * t_new + qq) * MOBA_TOPK + j]

    def copies(bb, hh, slot):
        out = []
        for qq in range(t_new):
            for j in range(MOBA_TOPK):
                blk = top_at(bb, hh, qq, j)
                for half in range(PAGES_PER_BLOCK):
                    pg = pt_ref[bb * PAGES_PER_SEQ + PAGES_PER_BLOCK * blk + half]
                    i = qq * per_q + j * PAGES_PER_BLOCK + half
                    out.append(pltpu.make_async_copy(kc_hbm.at[layer, pg, :, hh, :], kbuf.at[slot, i], sem.at[slot, 0]))
                    out.append(pltpu.make_async_copy(vc_hbm.at[layer, pg, :, hh, :], vbuf.at[slot, i], sem.at[slot, 1]))
        return out

    slot = step % 2

    @pl.when(step == 0)
    def _():
        for c in copies(b, h, slot):
            c.start()

    @pl.when(step + 1 < nstep)
    def _():
        nxt = step + 1
        for c in copies(nxt // nh, nxt % nh, 1 - slot):
            c.start()

    tab = lambda bkt: t5_ref[h, bkt]
    q = q_ref[0] * (1.0 / math.sqrt(A_HEAD_DIM))
    qb = q.astype(bf16)
    row = lax.broadcasted_iota(i32, (SUBLANES, 1), 0)
    kn, vn = kn_ref[0], vn_ref[0]
    own = []
    for t in range(t_new):
        sc = jnp.sum(q * kn[t:t + 1, :], axis=1, keepdims=True) + _t5_bias_of(jnp.maximum(row - t, 0), tab)
        own.append(jnp.where(row >= t, sc, NEG))
    own_max = functools.reduce(jnp.maximum, own)

    for c in copies(b, h, slot):
        c.wait()

    lane = lax.broadcasted_iota(i32, (1, per_q * PAGE_SIZE), 1)
    l_sum = jnp.zeros((SUBLANES, 1), f32)
    acc = jnp.zeros((SUBLANES, A_HEAD_DIM), f32)
    for qq in range(t_new):
        kq = kbuf[slot, qq * per_q:(qq + 1) * per_q].reshape(per_q * PAGE_SIZE, A_HEAD_DIM)
        vq = vbuf[slot, qq * per_q:(qq + 1) * per_q].reshape(per_q * PAGE_SIZE, A_HEAD_DIM)
        keypos = lane - (MOBA_TOPK - 1) * MOBA_BLOCK + top_at(b, h, qq, MOBA_TOPK - 1) * MOBA_BLOCK
        for j in range(MOBA_TOPK - 2, -1, -1):
            keypos = jnp.where(lane < (j + 1) * MOBA_BLOCK, lane - j * MOBA_BLOCK + top_at(b, h, qq, j) * MOBA_BLOCK,
                               keypos)
        lg = _dot(qb, kq.astype(bf16), NT) + _t5_bias_of(PAST_LEN + qq - keypos, tab)
        mine = row == qq
        m = jnp.maximum(own_max, jnp.max(lg, axis=1, keepdims=True))
        p = jnp.where(mine, jnp.exp(lg - m), 0.0)
        l_sum = l_sum + jnp.sum(p, axis=1, keepdims=True)
        acc = acc + _dot(p.astype(bf16), vq.astype(bf16))
        for t in range(t_new):
            po = jnp.where(mine, jnp.exp(own[t] - m), 0.0)
            l_sum = l_sum + po
            acc = acc + po * vn[t:t + 1, :]
    o_ref[0] = acc / jnp.where(row < t_new, l_sum, 1.0)


def _moba_sample(pt, top, t5_t, q8, kn8, vn8, cache_k, cache_v, layer, batch, t_new):
    npage = t_new * MOBA_TOPK * PAGES_PER_BLOCK
    rows = pl.BlockSpec((1, SUBLANES, A_HEAD_DIM), lambda b, h, pt, top: (b, 0, h))
    hbm = pl.BlockSpec(memory_space=pl.ANY)
    return pl.pallas_call(
        functools.partial(_moba_sample_body, t_new, layer),
        grid_spec=pltpu.PrefetchScalarGridSpec(
            num_scalar_prefetch=2, grid=(batch, A_HEADS),
            in_specs=[pl.BlockSpec(memory_space=pltpu.SMEM), rows, rows, rows, hbm, hbm],
            out_specs=rows,
            scratch_shapes=[pltpu.VMEM((2, npage, PAGE_SIZE, A_HEAD_DIM), f32),
                            pltpu.VMEM((2, npage, PAGE_SIZE, A_HEAD_DIM), f32),
                            pltpu.SemaphoreType.DMA((2, 2))]),
        out_shape=jax.ShapeDtypeStruct((batch, SUBLANES, A_W), f32),
        compiler_params=_params("arbitrary", "arbitrary"),
        name="moba_sample",
    )(pt, top, t5_t, q8, kn8, vn8, cache_k, cache_v)


def _rwkv_prep_body(t_valid, t_total, xb_ref, halo_ref, sp_ref, mu_ref, w0_ref, a0_ref, kk_ref, ka_ref, lora_ref,
                    r_ref, l_ref, a_ref, b_ref, k_ref, v_ref):
    i = pl.program_id(1)
    x = xb_ref[0]
    tt = x.shape[0]
    row = lax.broadcasted_iota(i32, (tt, 1), 0)
    first = jnp.where(i == 0, sp_ref[0], halo_ref[0, SUBLANES - 1:SUBLANES, :])
    prev = jnp.where(row == 0, first, pltpu.roll(x, 1, 0))
    xm = x + (prev - x) * mu_ref[...]
    r, k, v = xm[:, :B_W], xm[:, B_W:2 * B_W], xm[:, 2 * B_W:3 * B_W]
    wa = xm[:, 3 * B_W:]
    lane = lax.broadcasted_iota(i32, wa.shape, 1)
    lw = _dot3(jnp.where(lane < LORA_W, jnp.tanh(wa), 0.0), lora_ref[...])
    la = _dot3(jnp.where(lane < LORA_W, 0.0, wa), lora_ref[...])
    wlog = -math.exp(-0.5) * jax.nn.sigmoid(w0_ref[...] + lw)
    a = jax.nn.sigmoid(a0_ref[...] + la)
    kk = k * kk_ref[...]
    same_head = ((lax.broadcasted_iota(i32, (PAIR, PAIR), 0) < B_HEAD_DIM)
                 == (lax.broadcasted_iota(i32, (PAIR, PAIR), 1) < B_HEAD_DIM))
    sq = kk * kk
    ss = jnp.concatenate(_head_reduce([sq[:, p * PAIR:(p + 1) * PAIR] for p in range(B_W // PAIR)],
                                      jnp.where(same_head, 1.0, 0.0).astype(bf16)), axis=1)
    kk = kk / jnp.maximum(jnp.sqrt(ss), 1e-12)
    kp = k * (1.0 + (a - 1.0) * ka_ref[...])
    outs = (r, wlog, -kk, kk * a, kp, v)
    if t_valid < t_total:
        ok = (i * tt + row) < t_valid
        outs = tuple(jnp.where(ok, o, 0.0) for o in outs)
    for ref, o in zip((r_ref, l_ref, a_ref, b_ref, k_ref, v_ref), outs):
        ref[0] = o


def _rwkv_prep(xb3, shift_prev, t_valid, mu, w0, a0, k_k, k_a, lora):
    bn, tp, _ = xb3.shape
    tt = min(tp, 256)
    halo_blocks = tt // SUBLANES
    tile = lambda width: pl.BlockSpec((1, tt, width), lambda b, i: (b, i, 0))
    vec = lambda a: pl.BlockSpec(a.shape, lambda b, i: (0,) * a.ndim)
    out = jax.ShapeDtypeStruct((bn, tp, B_W), f32)
    return pl.pallas_call(
        functools.partial(_rwkv_prep_body, t_valid, tp),
        grid=(bn, tp // tt),
        in_specs=[tile(B_COLS),
                  pl.BlockSpec((1, SUBLANES, B_COLS), lambda b, i: (b, jnp.maximum(i * halo_blocks - 1, 0), 0)),
                  pl.BlockSpec((1, 1, B_COLS), lambda b, i: (b, 0, 0)),
                  vec(mu), vec(w0), vec(a0), vec(k_k), vec(k_a), vec(lora)],
        out_specs=[tile(B_W)] * 6,
        out_shape=[out] * 6,
        compiler_params=_params("parallel", "arbitrary"),
        name="rwkv_prep",
    )(xb3, xb3, shift_prev, mu, w0, a0, k_k, k_a, lora)


def _stack(p, lo):
    return jnp.concatenate([jnp.where(lo, p, 0.0), jnp.where(lo, 0.0, p)], axis=0)


def _head_reduce(parts, mat):
    n, c = len(parts), parts[0].shape[0]
    halves = [_split(a) for a in parts]
    out = _dot(jnp.concatenate([h for h, _ in halves] + [l for _, l in halves], axis=0), mat)
    return [out[i * c:(i + 1) * c] + out[(n + i) * c:(n + i + 1) * c] for i in range(n)]


def _rwkv_scan_body(npair, r_ref, l_ref, a_ref, b_ref, k_ref, v_ref, s0_ref, rk_ref, gg_ref, gb_ref, y_ref, s_ref, bd_ref):
    hd = B_HEAD_DIM

    @pl.when(pl.program_id(1) == 0)
    def _():
        zero = jnp.zeros((hd, hd), f32)
        for p in range(npair):
            top = jnp.concatenate([s0_ref[0, 2 * p].T, zero], axis=1)
            bot = jnp.concatenate([zero, s0_ref[0, 2 * p + 1].T], axis=1)
            bd_ref[p] = jnp.concatenate([top, bot], axis=0)

    c = RWKV_CHUNK
    pairs = range(npair)
    col = lambda x, p: x[:, p * PAIR:(p + 1) * PAIR]
    lane = lax.broadcasted_iota(i32, (c, PAIR), 1)
    lo = lane < B_HEAD_DIM
    t = lax.broadcasted_iota(i32, (c, PAIR), 0)
    j = lane & (B_HEAD_DIM - 1)
    strict, incl = t > j, t >= j
    eye_c = jnp.where(t == j, 1.0, 0.0)
    rowi = lax.broadcasted_iota(i32, (PAIR, PAIR), 0)
    coli = lax.broadcasted_iota(i32, (PAIR, PAIR), 1)
    same_head = (rowi < B_HEAD_DIM) == (coli < B_HEAD_DIM)
    head_sum = jnp.where(same_head, 1.0, 0.0).astype(bf16)
    head_mean = jnp.where(same_head, 1.0 / B_HEAD_DIM, 0.0).astype(bf16)
    st = lambda x: _stack(x, lo).astype(bf16)
    cast = lambda x: x.astype(bf16)

    R, L, A, B, K, V = (ref[0] for ref in (r_ref, l_ref, a_ref, b_ref, k_ref, v_ref))
    tri = jnp.where(lax.broadcasted_iota(i32, (c, c), 0) >= lax.broadcasted_iota(i32, (c, c), 1), 1.0, 0.0).astype(bf16)
    l1, l2 = _split(L)
    l3 = (L - l1.astype(f32) - l2.astype(f32)).astype(bf16)
    cum = _dot(tri, l1) + _dot(tri, l2) + _dot(tri, l3)
    last = cum[c - 1:c, :]
    wc = jnp.exp(last)
    At, Rt = A * jnp.exp(cum - L), R * jnp.exp(cum)
    einv, ew = jnp.exp(-cum), jnp.exp(last - cum)
    Bt, Kt, Bw, Kw = B * einv, K * einv, B * ew, K * ew

    ar = [cast(jnp.concatenate([col(At, p), col(Rt, p)], axis=0)) for p in pairs]
    pb = [_dot(ar[p], st(col(Bt, p)), NT) for p in pairs]
    pk = [_dot(ar[p], st(col(Kt, p)), NT) for p in pairs]
    pab = [jnp.where(strict, x[:c], 0.0) for x in pb]
    prb = [jnp.where(incl, x[c:], 0.0) for x in pb]
    pak = [jnp.where(strict, x[:c], 0.0) for x in pk]
    prk = [jnp.where(incl, x[c:], 0.0) for x in pk]
    x = [eye_c + m for m in pab]
    qp = pab
    for _ in range(int(math.log2(c)) - 1):
        qp = [_dot(cast(m), st(m)) for m in qp]
        x = [xi + _dot(cast(xi), st(m)) for xi, m in zip(x, qp)]
    akv = [_dot(cast(pak[p]), st(col(V, p))) for p in pairs]
    ua = [_dot(cast(x[p]), jnp.concatenate([st(akv[p]), st(col(At, p))], axis=1)) for p in pairs]
    u0 = [m[:, :PAIR] for m in ua]
    ah = [m[:, PAIR:] for m in ua]
    ry = [_dot(cast(prb[p]), jnp.concatenate([st(ah[p]), st(u0[p])], axis=1)) for p in pairs]
    rh = [col(Rt, p) + ry[p][:, :PAIR] for p in pairs]
    y0 = [ry[p][:, PAIR:] + _dot(cast(prk[p]), st(col(V, p))) for p in pairs]
    z1 = [_dot(cast(col(Bw, p)), cast(jnp.concatenate([ah[p], u0[p]], axis=1)), TN) for p in pairs]
    z2 = [_dot(cast(col(Kw, p)), cast(col(V, p)), TN) for p in pairs]
    mt = [jnp.where(rowi == coli, col(wc, p), 0.0) + jnp.where(same_head, z1[p][:, :PAIR], 0.0) for p in pairs]
    gt = [jnp.where(same_head, z1[p][:, PAIR:] + z2[p], 0.0) for p in pairs]
    nl = c + PAIR
    lhs = [_split(jnp.concatenate([rh[p], mt[p]], axis=0)) for p in pairs]
    bd = [_split(bd_ref[p]) for p in pairs]
    main = [_dot(jnp.concatenate(lhs[p], axis=0), bd[p][0]) for p in pairs]
    corr = [_dot(lhs[p][0], bd[p][1]) for p in pairs]
    prod = [main[p][:nl] + main[p][nl:] + corr[p] for p in pairs]
    new_bd = [prod[p][c:] + gt[p] for p in pairs]
    for p in pairs:
        bd_ref[p] = new_bd[p]

    @pl.when(pl.program_id(1) == pl.num_programs(1) - 1)
    def _():
        for p in pairs:
            s_ref[0, 2 * p] = new_bd[p][:hd, :hd].T
            s_ref[0, 2 * p + 1] = new_bd[p][hd:, hd:].T

    y = [prod[p][:c] + y0[p] for p in pairs]
    mean = _head_reduce(y, head_mean)
    d = [yi - m for yi, m in zip(y, mean)]
    var = _head_reduce([di * di for di in d], head_mean)
    bonus = _head_reduce([col(R, p) * col(K, p) * col(rk_ref[...], p) for p in pairs], head_sum)
    for p in pairs:
        yn = d[p] * lax.rsqrt(var[p] + RWKV_GN_EPS) * col(gg_ref[...], p) + col(gb_ref[...], p)
        y_ref[0, :, p * PAIR:(p + 1) * PAIR] = yn + bonus[p] * col(V, p)


def _rwkv_scan(prep, s0, r_k, gn_g, gn_b):
    bn, tp, _ = prep[0].shape
    npair = B_HEADS // 2
    tile = pl.BlockSpec((1, RWKV_CHUNK, B_W), lambda b, c: (b, c, 0))
    state = pl.BlockSpec((1, B_HEADS, B_HEAD_DIM, B_HEAD_DIM), lambda b, c: (b, 0, 0, 0))
    vec = pl.BlockSpec((1, B_W), lambda b, c: (0, 0))
    return pl.pallas_call(
        functools.partial(_rwkv_scan_body, npair),
        grid=(bn, tp // RWKV_CHUNK),
        in_specs=[tile] * 6 + [state, vec, vec, vec],
        out_specs=[tile, state],
        out_shape=[jax.ShapeDtypeStruct((bn, tp, B_W), f32), jax.ShapeDtypeStruct(s0.shape, f32)],
        scratch_shapes=[pltpu.VMEM((npair, PAIR, PAIR), f32)],
        compiler_params=_params("parallel", "arbitrary"),
        name="rwkv_scan",
    )(*prep, s0, r_k, gn_g, gn_b)


def _pool_body(pos0, x_ref, halo_ref, buf_ref, pw_ref, sc_ref, o_ref):
    i = pl.program_id(1)
    x = x_ref[0]
    tt = x.shape[0]
    halo = jnp.where(i == 0, buf_ref[0], halo_ref[0])
    xe = jnp.concatenate([halo, x], axis=0)
    row = lax.broadcasted_iota(i32, (tt, 1), 0)
    pos = pos0 + i * tt + row
    outs = []
    for g, w in enumerate(POOL_WINDOWS):
        cols = slice(g * POOL_GW, (g + 1) * POOL_GW)
        s = xe[:, cols]
        sh = 1
        while sh < w:
            s = s + pltpu.roll(s, sh, 0)
            sh *= 2
        cnt = jnp.minimum(w, pos + 1).astype(f32)
        pooled = s[POOL_BUF + 1:] / cnt - x[:, cols]
        outs.append(_dot(pooled.astype(bf16), pw_ref[g]))
    o_ref[0] = jnp.concatenate(outs, axis=1) * sc_ref[...]


def _pool(xc3, buf16, pos0, pool_w, layer, scale):
    bn, tp, _ = xc3.shape
    tt = min(tp, 256)
    hb = POOL_BUF + 1
    return pl.pallas_call(
        functools.partial(_pool_body, pos0),
        grid=(bn, tp // tt),
        in_specs=[pl.BlockSpec((1, tt, C_W), lambda b, i: (b, i, 0)),
                  pl.BlockSpec((1, hb, C_W), lambda b, i: (b, jnp.maximum(i * (tt // hb) - 1, 0), 0)),
                  pl.BlockSpec((1, hb, C_W), lambda b, i: (b, 0, 0)),
                  pl.BlockSpec((None,) + pool_w.shape[1:], lambda b, i: (layer, 0, 0, 0)),
                  pl.BlockSpec(scale.shape, lambda b, i: (0, 0))],
        out_specs=pl.BlockSpec((1, tt, C_W), lambda b, i: (b, i, 0)),
        out_shape=jax.ShapeDtypeStruct((bn, tp, C_W), f32),
        compiler_params=_params("parallel", "arbitrary"),
        name="pool_mix",
    )(xc3, xc3, buf16, pool_w, scale)


def _mem_proj_body(layer, x_ref, w_ref, prev_hbm, o_hbm, ybuf, sem):
    del prev_hbm
    y = _dot(x_ref[...], w_ref[0])
    for h in range(X_HEADS):
        ybuf[h] = y[:, h * X_HEAD_DIM:(h + 1) * X_HEAD_DIM]
    copies = [pltpu.make_async_copy(ybuf.at[h], o_hbm.at[layer, pl.program_id(0), :, h, :], sem.at[h])
              for h in range(X_HEADS)]
    for c in copies:
        c.start()
    for c in copies:
        c.wait()


def _mem_proj(x, w, layer, prev, batch):
    return pl.pallas_call(
        functools.partial(_mem_proj_body, layer),
        grid=(batch,),
        in_specs=[pl.BlockSpec((MEM_LEN, D_MODEL), lambda b: (b, 0)),
                  pl.BlockSpec((1,) + w.shape[1:], lambda b: (layer, 0, 0)),
                  pl.BlockSpec(memory_space=pl.ANY)],
        out_specs=pl.BlockSpec(memory_space=pl.ANY),
        out_shape=jax.ShapeDtypeStruct(prev.shape, f32),
        scratch_shapes=[pltpu.VMEM((X_HEADS, MEM_LEN, X_HEAD_DIM), f32), pltpu.SemaphoreType.DMA((X_HEADS,))],
        input_output_aliases={2: 0},
        compiler_params=_params("arbitrary"),
        name="mem_proj",
    )(x, w, prev)


def _xattn_body(layer, q_ref, k_hbm, v_hbm, o_ref, kbuf, vbuf, sem):
    b = pl.program_id(0)

    @pl.when(pl.program_id(1) == 0)
    def _():
        copies = [pltpu.make_async_copy(src.at[layer, b, :, h, :], dst.at[h], sem.at[i, h])
                  for i, (src, dst) in enumerate(((k_hbm, kbuf), (v_hbm, vbuf))) for h in range(X_HEADS)]
        for c in copies:
            c.start()
        for c in copies:
            c.wait()

    for h in range(X_HEADS):
        cols = slice(h * X_HEAD_DIM, (h + 1) * X_HEAD_DIM)
        q = (q_ref[:, cols] * (1.0 / math.sqrt(X_HEAD_DIM))).astype(bf16)
        s = _dot(q, kbuf[h].astype(bf16), NT)
        p = jnp.exp(s - jnp.max(s, axis=-1, keepdims=True))
        o_ref[:, cols] = _dot(p.astype(bf16), vbuf[h].astype(bf16)) / jnp.sum(p, axis=-1, keepdims=True)


def _xattn(q, mk, mv, layer, batch, tq):
    rows = q.shape[0] // batch
    nt = rows // tq
    qspec = pl.BlockSpec((tq, D_MODEL), lambda b, i: (b * nt + i, 0))
    hbm = pl.BlockSpec(memory_space=pl.ANY)
    return pl.pallas_call(
        functools.partial(_xattn_body, layer),
        grid=(batch, nt),
        in_specs=[qspec, hbm, hbm],
        out_specs=qspec,
        out_shape=jax.ShapeDtypeStruct(q.shape, f32),
        scratch_shapes=[pltpu.VMEM((X_HEADS, MEM_LEN, X_HEAD_DIM), f32), pltpu.VMEM((X_HEADS, MEM_LEN, X_HEAD_DIM), f32),
                        pltpu.SemaphoreType.DMA((2, X_HEADS))],
        compiler_params=_params("parallel", "arbitrary"),
        name="mem_xattn",
    )(q, mk, mv)


def _pad_rows(x3, rows):
    return jnp.pad(x3, ((0, 0), (0, rows - x3.shape[1]), (0, 0)))


def kernel(x_prompt, x_sample, cache_moba_k, cache_moba_v, page_table, state_rwkv, state_shift, state_pool, cache_mem_k, cache_mem_v, mem_prompt, w_in_even, w_out_even, rwkv_mu, rwkv_w0, rwkv_w_up, rwkv_a0, rwkv_a_up, rwkv_k_k, rwkv_k_a, rwkv_r_k, rwkv_gn_g, rwkv_gn_b, t5_bias, w_in_odd, pool_w, pool_scale, w_out_odd, xattn_w_q, xattn_w_k, xattn_w_v, xattn_w_o, ln_mix_g, ln_mix_b, ln_x_g, ln_x_b):
    bp, tp, _ = x_prompt.shape
    bs, ts, _ = x_sample.shape
    xp = x_prompt.reshape(bp * tp, D_MODEL)
    xs = x_sample.reshape(bs * ts, D_MODEL)
    mem = mem_prompt.reshape(bp * MEM_LEN, D_MODEL)
    pt = page_table.reshape(-1)
    t5_t = t5_bias.T
    bias_tiles = _t5_tiles(t5_t)
    row2 = lambda a: a.reshape(1, -1)
    cast = lambda a: a.astype(bf16)
    col_q, col_k, col_v, col_xb, col_z = 0, A_W, 2 * A_W, 3 * A_W, 3 * A_W + B_COLS
    w_even, w_odd, w_oe, w_oo, pw = cast(w_in_even), cast(w_in_odd), cast(w_out_even), cast(w_out_odd), cast(pool_w)
    xw_q, xw_k, xw_v, xw_o = cast(xattn_w_q), cast(xattn_w_k), cast(xattn_w_v), cast(xattn_w_o)
    xpb, xsb, memb = cast(xp), cast(xs), cast(mem)

    kp_l, vp_l, sp_l, shp_l, poolp_l = [], [], [], [], []
    mem_k = mem_v = jnp.zeros((DEPTH, bp, MEM_LEN, X_HEADS, X_HEAD_DIM), f32)
    ks_l, vs_l, ss_l, shs_l, pools_l = [], [], [], [], []
    for l in range(DEPTH):
        if l % 2 == 0:
            e = l // 2
            in_proj = lambda x: (_matmul(x, w_even, e, col_q, A_W), _matmul(x, w_even, e, col_k, A_W),
                                 _matmul(x, w_even, e, col_v, A_W), _matmul(x, w_even, e, col_xb, B_COLS, tn=640),
                                 _matmul(x, w_even, e, col_z, D_MODEL))
            lora = jnp.concatenate([rwkv_w_up[e], rwkv_a_up[e]], axis=0)
            rw = (row2(rwkv_mu[e]), row2(rwkv_w0[e]), row2(rwkv_a0[e]), row2(rwkv_k_k[e]), row2(rwkv_k_a[e]),
                  lora)
            gn = (row2(rwkv_r_k[e]), row2(rwkv_gn_g[e]), row2(rwkv_gn_b[e]))

            def rwkv(xb, batch, t_real, t_pad, shift_prev, s0):
                xb3 = xb.reshape(batch, t_real, B_COLS)
                if t_pad != t_real:
                    xb3 = _pad_rows(xb3, t_pad)
                prep = _rwkv_prep(xb3, shift_prev, t_real, *rw)
                y, s_new = _rwkv_scan(prep, s0, *gn)
                return y[:, :t_real].reshape(batch * t_real, B_W), xb3[:, t_real - 1], s_new

            q, k, v, xb, z = in_proj(xpb)
            a_out = _moba_prompt(q, k, v, bias_tiles, t5_t, bp, tp)
            b_out, shp, s_p = rwkv(xb, bp, tp, tp, jnp.zeros((bp, 1, B_COLS), f32),
                                   jnp.zeros((bp, B_HEADS, B_HEAD_DIM, B_HEAD_DIM), f32))
            mp_args = ([a_out, b_out], z, w_oe, e)
            kp_l.append(k.reshape(bp, tp, A_HEADS, A_HEAD_DIM))
            vp_l.append(v.reshape(bp, tp, A_HEADS, A_HEAD_DIM))
            sp_l.append(s_p)
            shp_l.append(shp)

            q, k, v, xb, z = in_proj(xsb)
            pad8 = lambda a: _pad_rows(a.reshape(bs, ts, A_W), SUBLANES)
            q8, k8, v8 = pad8(q), pad8(k), pad8(v)
            top = _moba_topk(q8, _moba_kmean(pt, cache_moba_k, e, bs), bs)
            top = top.reshape(bs, A_HEADS, SUBLANES, LANES)[:, :, :ts, :MOBA_TOPK].reshape(-1)
            a_out = _moba_sample(pt, top, t5_t, q8, k8, v8, cache_moba_k, cache_moba_v, e, bs, ts)
            a_out = a_out[:, :ts].reshape(bs * ts, A_W)
            b_out, shs, s_s = rwkv(xb, bs, ts, RWKV_CHUNK, state_shift[e].reshape(bs, 1, B_COLS), state_rwkv[e])
            ms_args = ([a_out, b_out], z, w_oe, e)
            ks_l.append(k.reshape(bs, ts, A_HEADS, A_HEAD_DIM))
            vs_l.append(v.reshape(bs, ts, A_HEADS, A_HEAD_DIM))
            ss_l.append(s_s)
            shs_l.append(shs)
        else:
            o = l // 2
            sc = row2(pool_scale[o])
            in_proj = lambda x: (_matmul(x, w_odd, o, 0, C_W), _matmul(x, w_odd, o, C_W, D_MODEL))

            xc, z = in_proj(xpb)
            xc3 = xc.reshape(bp, tp, C_W)
            y = _pool(xc3, jnp.zeros((bp, POOL_BUF + 1, C_W), f32), 0, pw, o, sc)
            mp_args = ([y.reshape(bp * tp, C_W)], z, w_oo, o)
            poolp_l.append(xc3[:, tp - POOL_BUF:])

            xc, z = in_proj(xsb)
            xc3 = xc.reshape(bs, ts, C_W)
            buf16 = jnp.pad(state_pool[o], ((0, 0), (1, 0), (0, 0)))
            y = _pool(_pad_rows(xc3, POOL_BUF + 1), buf16, PAST_LEN, pw, o, sc)[:, :ts]
            ms_args = ([y.reshape(bs * ts, C_W)], z, w_oo, o)
            pools_l.append(jnp.concatenate([state_pool[o], xc3], axis=1)[:, -POOL_BUF:])

        g, b = row2(ln_mix_g[l]), row2(ln_mix_b[l])
        xp, xpb = _proj_ln(*mp_args, xp, g, b)
        xs, xsb = _proj_ln(*ms_args, xs, g, b)

        g, b = row2(ln_x_g[l]), row2(ln_x_b[l])
        mem_k, mem_v = _mem_proj(memb, xw_k, l, mem_k, bp), _mem_proj(memb, xw_v, l, mem_v, bp)
        att = _xattn(_matmul(xpb, xw_q, l, 0, D_MODEL), mem_k, mem_v, l, bp, 512)
        xp, xpb = _proj_ln([att], None, xw_o, l, xp, g, b)
        qs8 = _pad_rows(_matmul(xsb, xw_q, l, 0, D_MODEL).reshape(bs, ts, D_MODEL), SUBLANES)
        att = _xattn(qs8.reshape(bs * SUBLANES, D_MODEL), cache_mem_k, cache_mem_v, l, bs, SUBLANES)
        att = att.reshape(bs, SUBLANES, D_MODEL)[:, :ts].reshape(bs * ts, D_MODEL)
        xs, xsb = _proj_ln([att], None, xw_o, l, xs, g, b)

    return (xp.reshape(bp, tp, D_MODEL), xs.reshape(bs, ts, D_MODEL),
            jnp.stack(kp_l), jnp.stack(vp_l), jnp.stack(sp_l), jnp.stack(shp_l), jnp.stack(poolp_l),
            mem_k, mem_v,
            jnp.stack(ks_l), jnp.stack(vs_l), jnp.stack(ss_l), jnp.stack(shs_l), jnp.stack(pools_l))
```

```python
import functools
import math

import numpy as np
import jax
import jax.numpy as jnp
from jax import lax
from jax.experimental import pallas as pl
from jax.experimental.pallas import tpu as pltpu

f32, bf16, i32 = jnp.float32, jnp.bfloat16, jnp.int32

D_MODEL = 2048
DEPTH = 4
PAST_LEN = 16384
PAGE_SIZE = 128
A_HEAD_DIM = 128
A_W = 1024
A_HEADS = 8
MOBA_BLOCK = 256
MOBA_TOPK = 3
B_HEAD_DIM = 64
B_W = 1024
B_HEADS = 16
LORA_W = 64
B_COLS = 3 * B_W + 2 * LORA_W
RWKV_GN_EPS = 64e-5
C_W = 2048
POOL_WINDOWS = (2, 4, 8, 16)
POOL_GW = C_W // len(POOL_WINDOWS)
POOL_BUF = max(POOL_WINDOWS) - 1
MEM_LEN = 256
X_HEADS = 4
X_HEAD_DIM = D_MODEL // X_HEADS
T5_BUCKETS = 32
T5_MAX_DIST = 128
LN_EPS = 1e-5
ALPHA = (2 * DEPTH) ** 0.25

NEG = -1e30
LANES = 128
SUBLANES = 8
BF16_ROWS = 2 * SUBLANES
RWKV_CHUNK = 64
GROUP_HEADS = LANES // B_HEAD_DIM
GROUP = LANES
VMEM_LIMIT = 48 * 1024 * 1024

NT = (((1,), (1,)), ((), ()))
TN = (((0,), (0,)), ((), ()))


def _t5_thresholds():
    exact = T5_BUCKETS // 2
    rel = np.arange(0, 4 * T5_MAX_DIST)
    relf = np.maximum(rel, exact).astype(np.float32)
    large = exact + (np.log(relf / np.float32(exact)) / np.float32(math.log(T5_MAX_DIST / exact))
                     * np.float32(T5_BUCKETS - exact)).astype(np.int32)
    bucket = np.where(rel < exact, rel, np.minimum(large, T5_BUCKETS - 1))
    assert (np.diff(bucket) >= 0).all() and bucket[-1] == T5_BUCKETS - 1
    return [int(np.argmax(bucket >= b)) for b in range(T5_BUCKETS)]


T5_THR = _t5_thresholds()


def _dot(a, b, dn=None):
    if dn is None:
        return jnp.dot(a, b, preferred_element_type=f32)
    return lax.dot_general(a, b, dn, preferred_element_type=f32)


def _split(x):
    hi = x.astype(bf16)
    lo = (x - hi.astype(f32)).astype(bf16)
    return hi, lo


def _dot3(a, b, dn=None):
    ah, al = _split(a)
    bh, bl = _split(b)
    return _dot(ah, bh, dn) + _dot(ah, bl, dn) + _dot(al, bh, dn)


def _dot2l(a, b_exact, dn=None):
    ah, al = _split(a)
    return _dot(ah, b_exact, dn) + _dot(al, b_exact, dn)


def _t5_bias_of(rel, tab):
    bias = jnp.full(rel.shape, tab(0), f32)
    for b in range(1, T5_BUCKETS):
        bias = jnp.where(rel >= T5_THR[b], tab(b), bias)
    return bias


def _params(*sem):
    return pltpu.CompilerParams(dimension_semantics=sem, vmem_limit_bytes=VMEM_LIMIT)


def _mm_body(x_ref, w_ref, o_ref):
    o_ref[...] = _dot(x_ref[...], w_ref[0].astype(bf16))


def _matmul(x, w, layer, col0, n, tn=512):
    m, k = x.shape
    tm = min(m, 2048)
    assert col0 % LANES == 0 and n % tn == 0 and m % tm == 0
    if col0 % tn == 0:
        wspec = pl.BlockSpec((1, k, tn), lambda i, j: (layer, 0, col0 // tn + j))
    else:
        wspec = pl.BlockSpec((pl.Element(1), pl.Element(k), pl.Element(tn)),
                             lambda i, j: (layer, 0, pl.multiple_of(col0 + j * tn, LANES)))
    return pl.pallas_call(
        _mm_body,
        grid=(m // tm, n // tn),
        in_specs=[pl.BlockSpec((tm, k), lambda i, j: (i, 0)), wspec],
        out_specs=pl.BlockSpec((tm, tn), lambda i, j: (i, j)),
        out_shape=jax.ShapeDtypeStruct((m, n), f32),
        compiler_params=_params("parallel", "arbitrary"),
        name="proj_matmul",
    )(x, w)


def _proj_ln_body(n_lhs, gated, *refs):
    lhs = [r[...] for r in refs[:n_lhs]]
    refs = refs[n_lhs:]
    y = lhs[0] if n_lhs == 1 else jnp.concatenate(lhs, axis=1)
    if gated:
        z = refs[0][...]
        refs = refs[1:]
        y = y * (z * jax.nn.sigmoid(z))
    w_ref, res_ref, g_ref, b_ref, o_ref, ob_ref = refs
    u = ALPHA * res_ref[...] + _dot(y.astype(bf16), w_ref[...])
    mu = jnp.mean(u, axis=-1, keepdims=True)
    d = u - mu
    var = jnp.mean(d * d, axis=-1, keepdims=True)
    out = d * lax.rsqrt(var + LN_EPS) * g_ref[...] + b_ref[...]
    o_ref[...] = out
    ob_ref[...] = out.astype(bf16)


def _proj_ln(lhs, z, w, layer, res, g, b):
    m = res.shape[0]
    tm = min(m, 256)
    row = lambda width: pl.BlockSpec((tm, width), lambda i: (i, 0))
    full = lambda a: pl.BlockSpec(a.shape, lambda i: (0, 0))
    args = list(lhs) + ([z] if z is not None else []) + [w, res, g, b]
    specs = [row(a.shape[1]) for a in lhs] + ([row(z.shape[1])] if z is not None else []) + [
        pl.BlockSpec((None,) + w.shape[1:], lambda i: (layer, 0, 0)), row(D_MODEL), full(g), full(b)]
    return pl.pallas_call(
        functools.partial(_proj_ln_body, len(lhs), z is not None),
        grid=(m // tm,),
        in_specs=specs,
        out_specs=[row(D_MODEL), row(D_MODEL)],
        out_shape=[jax.ShapeDtypeStruct((m, D_MODEL), f32), jax.ShapeDtypeStruct((m, D_MODEL), bf16)],
        compiler_params=_params("parallel"),
        name="proj_ln",
    )(*args)


def _t5_tiles_body(t5_ref, o_ref):
    h = pl.program_id(0)
    key = lax.broadcasted_iota(i32, (MOBA_BLOCK, MOBA_BLOCK), 0)
    qry = lax.broadcasted_iota(i32, (MOBA_BLOCK, MOBA_BLOCK), 1)
    tab = lambda b: t5_ref[h, b]
    rel = qry - key
    o_ref[0, 0] = jnp.where(rel >= 0, _t5_bias_of(jnp.maximum(rel, 0), tab), NEG)
    o_ref[0, 1] = _t5_bias_of(rel + MOBA_BLOCK, tab)


def _t5_tiles(t5_t):
    return pl.pallas_call(
        _t5_tiles_body,
        grid=(A_HEADS,),
        in_specs=[pl.BlockSpec(memory_space=pltpu.SMEM)],
        out_specs=pl.BlockSpec((1, 2, MOBA_BLOCK, MOBA_BLOCK), lambda h: (h, 0, 0, 0)),
        out_shape=jax.ShapeDtypeStruct((A_HEADS, 2, MOBA_BLOCK, MOBA_BLOCK), f32),
        compiler_params=_params("parallel"),
        name="t5_tiles",
    )(t5_t)


def _moba_prompt_body(nb, t5_ref, q_ref, k_ref, v_ref, bias_ref, o_ref):
    blkw = MOBA_BLOCK
    far_bias = t5_ref[pl.program_id(1), T5_BUCKETS - 1]
    k = k_ref[...]
    q = q_ref[...]
    kb = k.astype(bf16)
    vt = v_ref[...].T.astype(bf16)
    km = jnp.concatenate([jnp.mean(k[n * blkw:(n + 1) * blkw], axis=0, keepdims=True) for n in range(nb)], axis=0)
    gate_all = _dot3(km, q, NT)
    qs = (q * (1.0 / math.sqrt(A_HEAD_DIM))).astype(bf16)
    blk = lax.broadcasted_iota(i32, (nb, blkw), 0)

    def scores(qi):
        rows = slice(qi * blkw, (qi + 1) * blkw)
        gate = gate_all[:, rows]
        beaten = jnp.zeros(gate.shape, f32)
        for m in range(qi):
            gm = gate[m:m + 1, :]
            beaten = beaten + jnp.where((gm > gate) | ((gm == gate) & (m < blk)), 1.0, 0.0)
        sel = jnp.where((blk < qi) & (beaten < MOBA_TOPK), 0.0, NEG)
        s_all = _dot(kb[:(qi + 1) * blkw], qs[rows], NT)
        tiles = []
        for n in range(qi + 1):
            s = s_all[n * blkw:(n + 1) * blkw]
            if n == qi:
                s = s + bias_ref[0, 0]
            elif n == qi - 1:
                s = s + sel[n:n + 1, :] + bias_ref[0, 1]
            else:
                s = s + (sel[n:n + 1, :] + far_bias)
            tiles.append(s)
        return tiles

    def attend(qi, tiles):
        mx = functools.reduce(jnp.maximum, [jnp.max(s, axis=0, keepdims=True) for s in tiles])
        ps = [jnp.exp(s - mx) for s in tiles]
        den = functools.reduce(jnp.add, [jnp.sum(p, axis=0, keepdims=True) for p in ps])
        pcat = jnp.concatenate([p.astype(bf16) for p in ps], axis=0)
        acc = _dot(vt[:, :(qi + 1) * blkw], pcat)
        o_ref[qi * blkw:(qi + 1) * blkw, :] = (acc / den).T

    pending = scores(0)
    for qi in range(nb):
        ahead = scores(qi + 1) if qi + 1 < nb else None
        attend(qi, pending)
        pending = ahead


def _moba_prompt(q, k, v, bias_tiles, t5_t, batch, seq):
    assert 2 * MOBA_BLOCK - (MOBA_BLOCK - 1) >= T5_THR[-1]
    nb = seq // MOBA_BLOCK
    spec = pl.BlockSpec((seq, A_HEAD_DIM), lambda b, h: (b, h))
    return pl.pallas_call(
        functools.partial(_moba_prompt_body, nb),
        grid=(batch, A_HEADS),
        in_specs=[pl.BlockSpec(memory_space=pltpu.SMEM), spec, spec, spec,
                  pl.BlockSpec((1, 2, MOBA_BLOCK, MOBA_BLOCK), lambda b, h: (h, 0, 0, 0))],
        out_specs=spec,
        out_shape=jax.ShapeDtypeStruct((batch * seq, A_W), f32),
        compiler_params=_params("parallel", "parallel"),
        name="moba_prompt",
    )(t5_t, q, k, v, bias_tiles)


PAGES_PER_SEQ = PAST_LEN // PAGE_SIZE
PAST_BLOCKS = PAST_LEN // MOBA_BLOCK
PAGES_PER_BLOCK = MOBA_BLOCK // PAGE_SIZE
KMEAN_BLOCKS_PER_STEP = SUBLANES


def _kmean_body(pt_ref, *refs):
    pages, o_ref = refs[:-1], refs[-1]
    for jj in range(KMEAN_BLOCKS_PER_STEP):
        s = functools.reduce(jnp.add, [jnp.sum(pages[PAGES_PER_BLOCK * jj + i][0, 0], axis=0)
                                       for i in range(PAGES_PER_BLOCK)])
        s = s * (1.0 / MOBA_BLOCK)
        for h in range(A_HEADS):
            o_ref[0, h, jj:jj + 1, :] = s[h:h + 1, :]


def _moba_kmean(pt, cache_k, layer, batch):
    per_step = PAGES_PER_BLOCK * KMEAN_BLOCKS_PER_STEP
    page = lambda i: pl.BlockSpec((1, 1, PAGE_SIZE, A_HEADS, A_HEAD_DIM),
                                  lambda b, g, pt: (layer, pt[b * PAGES_PER_SEQ + g * per_step + i], 0, 0, 0))
    return pl.pallas_call(
        _kmean_body,
        grid_spec=pltpu.PrefetchScalarGridSpec(
            num_scalar_prefetch=1, grid=(batch, PAST_BLOCKS // KMEAN_BLOCKS_PER_STEP),
            in_specs=[page(i) for i in range(per_step)],
            out_specs=pl.BlockSpec((1, A_HEADS, KMEAN_BLOCKS_PER_STEP, A_HEAD_DIM), lambda b, g, pt: (b, 0, g, 0))),
        out_shape=jax.ShapeDtypeStruct((batch, A_HEADS, PAST_BLOCKS, A_HEAD_DIM), f32),
        compiler_params=_params("parallel", "arbitrary"),
        name="moba_kmean",
    )(pt, *([cache_k] * per_step))


def _moba_topk_body(q_ref, km_ref, o_ref):
    lane = lax.broadcasted_iota(i32, (SUBLANES, LANES), 1)
    for h in range(A_HEADS):
        cols = slice(h * A_HEAD_DIM, (h + 1) * A_HEAD_DIM)
        gate = _dot3(q_ref[0, :, cols], km_ref[0, h], NT)
        blk = lax.broadcasted_iota(i32, gate.shape, 1)
        out = jnp.zeros((SUBLANES, LANES), i32)
        for j in range(MOBA_TOPK):
            best = jnp.max(gate, axis=1, keepdims=True)
            idx = jnp.min(jnp.where(gate == best, blk, PAST_BLOCKS), axis=1, keepdims=True)
            out = jnp.where(lane == j, idx, out)
            gate = jnp.where(blk == idx, -jnp.inf, gate)
        o_ref[0, h * SUBLANES:(h + 1) * SUBLANES, :] = out


def _moba_topk(q8, kmean, batch):
    return pl.pallas_call(
        _moba_topk_body,
        grid=(batch,),
        in_specs=[pl.BlockSpec((1, SUBLANES, A_W), lambda b: (b, 0, 0)),
                  pl.BlockSpec((1, A_HEADS, PAST_BLOCKS, A_HEAD_DIM), lambda b: (b, 0, 0, 0))],
        out_specs=pl.BlockSpec((1, A_HEADS * SUBLANES, LANES), lambda b: (b, 0, 0)),
        out_shape=jax.ShapeDtypeStruct((batch, A_HEADS * SUBLANES, LANES), i32),
        compiler_params=_params("parallel"),
        name="moba_topk",
    )(q8, kmean)


def _moba_sample_body(t_new, layer, pt_ref, top_ref, t5_ref, q_ref, kn_ref, vn_ref, kc_hbm, vc_hbm, o_ref,
                      kbuf, vbuf, sem):
    b, h = pl.program_id(0), pl.program_id(1)
    nh = pl.num_programs(1)
    step = b * nh + h
    nstep = pl.num_programs(0) * nh
    per_q = MOBA_TOPK * PAGES_PER_BLOCK
    top_at = lambda bb, hh, qq, j: top_ref[((bb * A_HEADS + hh) * t_new + qq) * MOBA_TOPK + j]

    def copies(bb, hh, slot):
        out = []
        for qq in range(t_new):
            for j in range(MOBA_TOPK):
                blk = top_at(bb, hh, qq, j)
                for half in range(PAGES_PER_BLOCK):
                    pg = pt_ref[bb * PAGES_PER_SEQ + PAGES_PER_BLOCK * blk + half]
                    i = qq * per_q + j * PAGES_PER_BLOCK + half
                    out.append(pltpu.make_async_copy(kc_hbm.at[layer, pg, :, hh, :], kbuf.at[slot, i], sem.at[slot, 0]))
                    out.append(pltpu.make_async_copy(vc_hbm.at[layer, pg, :, hh, :], vbuf.at[slot, i], sem.at[slot, 1]))
        return out

    slot = step % 2

    @pl.when(step == 0)
    def _():
        for c in copies(b, h, slot):
            c.start()

    @pl.when(step + 1 < nstep)
    def _():
        nxt = step + 1
        for c in copies(nxt // nh, nxt % nh, 1 - slot):
            c.start()

    tab = lambda bkt: t5_ref[h, bkt]
    q = q_ref[0] * (1.0 / math.sqrt(A_HEAD_DIM))
    qb = q.astype(bf16)
    row = lax.broadcasted_iota(i32, (SUBLANES, 1), 0)
    kn, vn = kn_ref[0], vn_ref[0]
    own = []
    for t in range(t_new):
        sc = jnp.sum(q * kn[t:t + 1, :], axis=1, keepdims=True) + _t5_bias_of(jnp.maximum(row - t, 0), tab)
        own.append(jnp.where(row >= t, sc, NEG))
    own_max = functools.reduce(jnp.maximum, own)

    for c in copies(b, h, slot):
        c.wait()

    lane = lax.broadcasted_iota(i32, (1, per_q * PAGE_SIZE), 1)
    l_sum = jnp.zeros((SUBLANES, 1), f32)
    acc = jnp.zeros((SUBLANES, A_HEAD_DIM), f32)
    for qq in range(t_new):
        kq = kbuf[slot, qq * per_q:(qq + 1) * per_q].reshape(per_q * PAGE_SIZE, A_HEAD_DIM)
        vq = vbuf[slot, qq * per_q:(qq + 1) * per_q].reshape(per_q * PAGE_SIZE, A_HEAD_DIM)
        keypos = lane - (MOBA_TOPK - 1) * MOBA_BLOCK + top_at(b, h, qq, MOBA_TOPK - 1) * MOBA_BLOCK
        for j in range(MOBA_TOPK - 2, -1, -1):
            keypos = jnp.where(lane < (j + 1) * MOBA_BLOCK, lane - j * MOBA_BLOCK + top_at(b, h, qq, j) * MOBA_BLOCK,
                               keypos)
        lg = _dot(qb, kq.astype(bf16), NT) + _t5_bias_of(PAST_LEN + qq - keypos, tab)
        mine = row == qq
        m = jnp.maximum(own_max, jnp.max(lg, axis=1, keepdims=True))
        p = jnp.where(mine, jnp.exp(lg - m), 0.0)
        l_sum = l_sum + jnp.sum(p, axis=1, keepdims=True)
        acc = acc + _dot(p.astype(bf16), vq.astype(bf16))
        for t in range(t_new):
            po = jnp.where(mine, jnp.exp(own[t] - m), 0.0)
            l_sum = l_sum + po
            acc = acc + po * vn[t:t + 1, :]
    o_ref[0] = acc / jnp.where(row < t_new, l_sum, 1.0)


def _moba_sample(pt, top, t5_t, q8, kn8, vn8, cache_k, cache_v, layer, batch, t_new):
    npage = t_new * MOBA_TOPK * PAGES_PER_BLOCK
    rows = pl.BlockSpec((1, SUBLANES, A_HEAD_DIM), lambda b, h, pt, top: (b, 0, h))
    hbm = pl.BlockSpec(memory_space=pl.ANY)
    return pl.pallas_call(
        functools.partial(_moba_sample_body, t_new, layer),
        grid_spec=pltpu.PrefetchScalarGridSpec(
            num_scalar_prefetch=2, grid=(batch, A_HEADS),
            in_specs=[pl.BlockSpec(memory_space=pltpu.SMEM), rows, rows, rows, hbm, hbm],
            out_specs=rows,
            scratch_shapes=[pltpu.VMEM((2, npage, PAGE_SIZE, A_HEAD_DIM), f32),
                            pltpu.VMEM((2, npage, PAGE_SIZE, A_HEAD_DIM), f32),
                            pltpu.SemaphoreType.DMA((2, 2))]),
        out_shape=jax.ShapeDtypeStruct((batch, SUBLANES, A_W), f32),
        compiler_params=_params("arbitrary", "arbitrary"),
        name="moba_sample",
    )(pt, top, t5_t, q8, kn8, vn8, cache_k, cache_v)


def _rwkv_prep_body(t_valid, t_total, xb_ref, halo_ref, sp_ref, mu_ref, w0_ref, a0_ref, kk_ref, ka_ref, lora_ref,
                    r_ref, l_ref, a_ref, b_ref, k_ref, v_ref):
    i = pl.program_id(1)
    x = xb_ref[0]
    tt = x.shape[0]
    row = lax.broadcasted_iota(i32, (tt, 1), 0)
    first = jnp.where(i == 0, sp_ref[0], halo_ref[0, SUBLANES - 1:SUBLANES, :])
    prev = jnp.where(row == 0, first, pltpu.roll(x, 1, 0))
    xm = x + (prev - x) * mu_ref[...]
    r, k, v = xm[:, :B_W], xm[:, B_W:2 * B_W], xm[:, 2 * B_W:3 * B_W]
    wa = xm[:, 3 * B_W:]
    lane = lax.broadcasted_iota(i32, wa.shape, 1)
    lw = _dot3(jnp.where(lane < LORA_W, jnp.tanh(wa), 0.0), lora_ref[...])
    la = _dot3(jnp.where(lane < LORA_W, 0.0, wa), lora_ref[...])
    wlog = -math.exp(-0.5) * jax.nn.sigmoid(w0_ref[...] + lw)
    a = jax.nn.sigmoid(a0_ref[...] + la)
    kk = k * kk_ref[...]
    sq = kk * kk
    ss = jnp.concatenate(_head_reduce([sq[:, p * GROUP:(p + 1) * GROUP] for p in range(B_W // GROUP)],
                                      jnp.where(_same_head(), 1.0, 0.0).astype(bf16)), axis=1)
    kk = kk * lax.rsqrt(jnp.maximum(ss, 1e-24))
    kp = k * (1.0 + (a - 1.0) * ka_ref[...])
    outs = (r, wlog, -kk, kk * a, kp, v)
    if t_valid < t_total:
        ok = (i * tt + row) < t_valid
        outs = tuple(jnp.where(ok, o, 0.0) for o in outs)
    for ref, o in zip((r_ref, l_ref, a_ref, b_ref, k_ref, v_ref), outs):
        ref[0] = o


def _rwkv_prep(xb3, shift_prev, t_valid, mu, w0, a0, k_k, k_a, lora):
    bn, tp, _ = xb3.shape
    tt = min(tp, 256)
    halo_blocks = tt // SUBLANES
    tile = lambda width: pl.BlockSpec((1, tt, width), lambda b, i: (b, i, 0))
    vec = lambda a: pl.BlockSpec(a.shape, lambda b, i: (0,) * a.ndim)
    out = jax.ShapeDtypeStruct((bn, tp, B_W), f32)
    return pl.pallas_call(
        functools.partial(_rwkv_prep_body, t_valid, tp),
        grid=(bn, tp // tt),
        in_specs=[tile(B_COLS),
                  pl.BlockSpec((1, SUBLANES, B_COLS), lambda b, i: (b, jnp.maximum(i * halo_blocks - 1, 0), 0)),
                  pl.BlockSpec((1, 1, B_COLS), lambda b, i: (b, 0, 0)),
                  vec(mu), vec(w0), vec(a0), vec(k_k), vec(k_a), vec(lora)],
        out_specs=[tile(B_W)] * 6,
        out_shape=[out] * 6,
        compiler_params=_params("parallel", "arbitrary"),
        name="rwkv_prep",
    )(xb3, xb3, shift_prev, mu, w0, a0, k_k, k_a, lora)


def _head_of(idx):
    return idx // B_HEAD_DIM


def _same_head():
    return (_head_of(lax.broadcasted_iota(i32, (GROUP, GROUP), 0))
            == _head_of(lax.broadcasted_iota(i32, (GROUP, GROUP), 1)))


def _stack(p, head):
    return jnp.concatenate([jnp.where(head == h, p, 0.0) for h in range(GROUP_HEADS)], axis=0)


def _head_reduce(parts, mat):
    n, c = len(parts), parts[0].shape[0]
    halves = [_split(a) for a in parts]
    out = _dot(jnp.concatenate([h for h, _ in halves] + [l for _, l in halves], axis=0), mat)
    return [out[i * c:(i + 1) * c] + out[(n + i) * c:(n + i + 1) * c] for i in range(n)]


def _rwkv_scan_body(npair, r_ref, l_ref, a_ref, b_ref, k_ref, v_ref, s0_ref, rk_ref, gg_ref, gb_ref, y_ref, s_ref, bd_ref):
    hd = B_HEAD_DIM
    heads = range(GROUP_HEADS)

    @pl.when(pl.program_id(1) == 0)
    def _():
        zero = jnp.zeros((hd, hd), f32)
        for p in range(npair):
            bd_ref[p] = jnp.concatenate(
                [jnp.concatenate([s0_ref[0, GROUP_HEADS * p + h].T if h == g else zero for g in heads], axis=1)
                 for h in heads], axis=0)

    c = RWKV_CHUNK
    pairs = range(npair)
    col = lambda x, p: x[:, p * GROUP:(p + 1) * GROUP]
    lane = lax.broadcasted_iota(i32, (c, GROUP), 1)
    t = lax.broadcasted_iota(i32, (c, GROUP), 0)
    j = lane & (B_HEAD_DIM - 1)
    strict, incl = t > j, t >= j
    eye_c = jnp.where(t == j, 1.0, 0.0)
    rowi = lax.broadcasted_iota(i32, (GROUP, GROUP), 0)
    coli = lax.broadcasted_iota(i32, (GROUP, GROUP), 1)
    same_head = _same_head()
    head_sum = jnp.where(same_head, 1.0, 0.0).astype(bf16)
    head_mean = jnp.where(same_head, 1.0 / B_HEAD_DIM, 0.0).astype(bf16)
    head_lane = _head_of(lane)
    st = lambda x: _stack(x, head_lane).astype(bf16)
    cast = lambda x: x.astype(bf16)

    R, L, A, B, K, V = (ref[0] for ref in (r_ref, l_ref, a_ref, b_ref, k_ref, v_ref))
    tri = jnp.where(lax.broadcasted_iota(i32, (c, c), 0) >= lax.broadcasted_iota(i32, (c, c), 1), 1.0, 0.0).astype(bf16)
    l1, l2 = _split(L)
    l3 = (L - l1.astype(f32) - l2.astype(f32)).astype(bf16)
    cum = _dot(tri, l1) + _dot(tri, l2) + _dot(tri, l3)
    last = cum[c - 1:c, :]
    wc = jnp.exp(last)
    At, Rt = A * jnp.exp(cum - L), R * jnp.exp(cum)
    einv, ew = jnp.exp(-cum), jnp.exp(last - cum)
    Bt, Kt, Bw, Kw = B * einv, K * einv, B * ew, K * ew

    ar = [cast(jnp.concatenate([col(At, p), col(Rt, p)], axis=0)) for p in pairs]
    pb = [_dot(ar[p], st(col(Bt, p)), NT) for p in pairs]
    pk = [_dot(ar[p], st(col(Kt, p)), NT) for p in pairs]
    pab = [jnp.where(strict, x[:c], 0.0) for x in pb]
    prb = [jnp.where(incl, x[c:], 0.0) for x in pb]
    pak = [jnp.where(strict, x[:c], 0.0) for x in pk]
    prk = [jnp.where(incl, x[c:], 0.0) for x in pk]
    x = [eye_c + m for m in pab]
    qp = pab
    for _ in range(int(math.log2(c)) - 1):
        qp = [_dot(cast(m), st(m)) for m in qp]
        x = [xi + _dot(cast(xi), st(m)) for xi, m in zip(x, qp)]
    akv = [_dot(cast(pak[p]), st(col(V, p))) for p in pairs]
    ua = [_dot(cast(x[p]), jnp.concatenate([st(akv[p]), st(col(At, p))], axis=1)) for p in pairs]
    u0 = [m[:, :GROUP] for m in ua]
    ah = [m[:, GROUP:] for m in ua]
    ry = [_dot(cast(prb[p]), jnp.concatenate([st(ah[p]), st(u0[p])], axis=1)) for p in pairs]
    rh = [col(Rt, p) + ry[p][:, :GROUP] for p in pairs]
    y0 = [ry[p][:, GROUP:] + _dot(cast(prk[p]), st(col(V, p))) for p in pairs]
    z1 = [_dot(cast(col(Bw, p)), cast(jnp.concatenate([ah[p], u0[p]], axis=1)), TN) for p in pairs]
    z2 = [_dot(cast(col(Kw, p)), cast(col(V, p)), TN) for p in pairs]
    mt = [jnp.where(rowi == coli, col(wc, p), 0.0) + jnp.where(same_head, z1[p][:, :GROUP], 0.0) for p in pairs]
    gt = [jnp.where(same_head, z1[p][:, GROUP:] + z2[p], 0.0) for p in pairs]
    nl = c + GROUP
    lhs = [_split(jnp.concatenate([rh[p], mt[p]], axis=0)) for p in pairs]
    bd = [_split(bd_ref[p]) for p in pairs]
    main = [_dot(jnp.concatenate(lhs[p], axis=0), bd[p][0]) for p in pairs]
    corr = [_dot(lhs[p][0], bd[p][1]) for p in pairs]
    prod = [main[p][:nl] + main[p][nl:] + corr[p] for p in pairs]
    new_bd = [prod[p][c:] + gt[p] for p in pairs]
    for p in pairs:
        bd_ref[p] = new_bd[p]

    @pl.when(pl.program_id(1) == pl.num_programs(1) - 1)
    def _():
        for p in pairs:
            for h in heads:
                s_ref[0, GROUP_HEADS * p + h] = new_bd[p][h * hd:(h + 1) * hd, h * hd:(h + 1) * hd].T

    y = [prod[p][:c] + y0[p] for p in pairs]
    mean = _head_reduce(y, head_mean)
    d = [yi - m for yi, m in zip(y, mean)]
    var = _head_reduce([di * di for di in d], head_mean)
    bonus = _head_reduce([col(R, p) * col(K, p) * col(rk_ref[...], p) for p in pairs], head_sum)
    for p in pairs:
        yn = d[p] * lax.rsqrt(var[p] + RWKV_GN_EPS) * col(gg_ref[...], p) + col(gb_ref[...], p)
        y_ref[0, :, p * GROUP:(p + 1) * GROUP] = yn + bonus[p] * col(V, p)


def _rwkv_scan(prep, s0, r_k, gn_g, gn_b):
    bn, tp, _ = prep[0].shape
    npair = B_HEADS // GROUP_HEADS
    tile = pl.BlockSpec((1, RWKV_CHUNK, B_W), lambda b, c: (b, c, 0))
    state = pl.BlockSpec((1, B_HEADS, B_HEAD_DIM, B_HEAD_DIM), lambda b, c: (b, 0, 0, 0))
    vec = pl.BlockSpec((1, B_W), lambda b, c: (0, 0))
    return pl.pallas_call(
        functools.partial(_rwkv_scan_body, npair),
        grid=(bn, tp // RWKV_CHUNK),
        in_specs=[tile] * 6 + [state, vec, vec, vec],
        out_specs=[tile, state],
        out_shape=[jax.ShapeDtypeStruct((bn, tp, B_W), f32), jax.ShapeDtypeStruct(s0.shape, f32)],
        scratch_shapes=[pltpu.VMEM((npair, GROUP, GROUP), f32)],
        compiler_params=_params("parallel", "arbitrary"),
        name="rwkv_scan",
    )(*prep, s0, r_k, gn_g, gn_b)


def _pool_body(pos0, x_ref, halo_ref, buf_ref, pw_ref, sc_ref, o_ref):
    i = pl.program_id(1)
    x = x_ref[0]
    tt = x.shape[0]
    halo = jnp.where(i == 0, buf_ref[0], halo_ref[0])
    xe = jnp.concatenate([halo, x], axis=0)
    row = lax.broadcasted_iota(i32, (tt, 1), 0)
    pos = pos0 + i * tt + row
    outs = []
    for g, w in enumerate(POOL_WINDOWS):
        cols = slice(g * POOL_GW, (g + 1) * POOL_GW)
        s = xe[:, cols]
        sh = 1
        while sh < w:
            s = s + pltpu.roll(s, sh, 0)
            sh *= 2
        cnt = jnp.minimum(w, pos + 1).astype(f32)
        pooled = s[POOL_BUF + 1:] / cnt - x[:, cols]
        outs.append(_dot(pooled.astype(bf16), pw_ref[g]))
    o_ref[0] = jnp.concatenate(outs, axis=1) * sc_ref[...]


def _pool(xc3, buf16, pos0, pool_w, layer, scale):
    bn, tp, _ = xc3.shape
    tt = min(tp, 256)
    hb = POOL_BUF + 1
    return pl.pallas_call(
        functools.partial(_pool_body, pos0),
        grid=(bn, tp // tt),
        in_specs=[pl.BlockSpec((1, tt, C_W), lambda b, i: (b, i, 0)),
                  pl.BlockSpec((1, hb, C_W), lambda b, i: (b, jnp.maximum(i * (tt // hb) - 1, 0), 0)),
                  pl.BlockSpec((1, hb, C_W), lambda b, i: (b, 0, 0)),
                  pl.BlockSpec((None,) + pool_w.shape[1:], lambda b, i: (layer, 0, 0, 0)),
                  pl.BlockSpec(scale.shape, lambda b, i: (0, 0))],
        out_specs=pl.BlockSpec((1, tt, C_W), lambda b, i: (b, i, 0)),
        out_shape=jax.ShapeDtypeStruct((bn, tp, C_W), f32),
        compiler_params=_params("parallel", "arbitrary"),
        name="pool_mix",
    )(xc3, xc3, buf16, pool_w, scale)


def _mem_proj_body(layer, x_ref, w_ref, prev_hbm, o_hbm, ybuf, sem):
    del prev_hbm
    y = _dot(x_ref[...], w_ref[0].astype(bf16))
    for h in range(X_HEADS):
        ybuf[h] = y[:, h * X_HEAD_DIM:(h + 1) * X_HEAD_DIM]
    copies = [pltpu.make_async_copy(ybuf.at[h], o_hbm.at[layer, pl.program_id(0), :, h, :], sem.at[h])
              for h in range(X_HEADS)]
    for c in copies:
        c.start()
    for c in copies:
        c.wait()


def _mem_proj(x, w, layer, prev, batch):
    return pl.pallas_call(
        functools.partial(_mem_proj_body, layer),
        grid=(batch,),
        in_specs=[pl.BlockSpec((MEM_LEN, D_MODEL), lambda b: (b, 0)),
                  pl.BlockSpec((1,) + w.shape[1:], lambda b: (layer, 0, 0)),
                  pl.BlockSpec(memory_space=pl.ANY)],
        out_specs=pl.BlockSpec(memory_space=pl.ANY),
        out_shape=jax.ShapeDtypeStruct(prev.shape, f32),
        scratch_shapes=[pltpu.VMEM((X_HEADS, MEM_LEN, X_HEAD_DIM), f32), pltpu.SemaphoreType.DMA((X_HEADS,))],
        input_output_aliases={2: 0},
        compiler_params=_params("arbitrary"),
        name="mem_proj",
    )(x, w, prev)


def _xattn_block_body(layer, x_ref, res_ref, wq_hbm, wo_hbm, k_hbm, v_hbm, g_ref, b_ref, o_ref, ob_ref,
                      wq, wo, kstage, vstage, kb, vb, wsem, sem):
    b, i = pl.program_id(0), pl.program_id(1)

    @pl.when((b == 0) & (i == 0))
    def _():
        copies = [pltpu.make_async_copy(wq_hbm.at[layer], wq, wsem.at[0]),
                  pltpu.make_async_copy(wo_hbm.at[layer], wo, wsem.at[1])]
        for c in copies:
            c.start()
        for c in copies:
            c.wait()

    @pl.when(i == 0)
    def _():
        copies = [pltpu.make_async_copy(src.at[layer, b, :, h, :], dst.at[h], sem.at[n, h])
                  for n, (src, dst) in enumerate(((k_hbm, kstage), (v_hbm, vstage))) for h in range(X_HEADS)]
        for c in copies:
            c.start()
        for c in copies:
            c.wait()
        kb[...] = kstage[...].astype(bf16)
        vb[...] = vstage[...].astype(bf16)

    q = _dot(x_ref[...], wq[...])
    heads = []
    for h in range(X_HEADS):
        qh = (q[:, h * X_HEAD_DIM:(h + 1) * X_HEAD_DIM] * (1.0 / math.sqrt(X_HEAD_DIM))).astype(bf16)
        s = _dot(qh, kb[h], NT)
        p = jnp.exp(s - jnp.max(s, axis=-1, keepdims=True))
        heads.append((_dot(p.astype(bf16), vb[h]) / jnp.sum(p, axis=-1, keepdims=True)).astype(bf16))
    u = ALPHA * res_ref[...] + _dot(jnp.concatenate(heads, axis=1), wo[...])
    mu = jnp.mean(u, axis=-1, keepdims=True)
    d = u - mu
    var = jnp.mean(d * d, axis=-1, keepdims=True)
    out = d * lax.rsqrt(var + LN_EPS) * g_ref[...] + b_ref[...]
    o_ref[...] = out
    ob_ref[...] = out.astype(bf16)


def _xattn_block(xb16, res, w_q, w_o, mk, mv, layer, g, b, batch, tq):
    m = res.shape[0]
    nt = m // batch // tq
    row = pl.BlockSpec((tq, D_MODEL), lambda bb, i: (bb * nt + i, 0))
    vec = pl.BlockSpec((1, D_MODEL), lambda bb, i: (0, 0))
    hbm = pl.BlockSpec(memory_space=pl.ANY)
    mem_tile = (X_HEADS, MEM_LEN, X_HEAD_DIM)
    return pl.pallas_call(
        functools.partial(_xattn_block_body, layer),
        grid=(batch, nt),
        in_specs=[row, row, hbm, hbm, hbm, hbm, vec, vec],
        out_specs=[row, row],
        out_shape=[jax.ShapeDtypeStruct((m, D_MODEL), f32), jax.ShapeDtypeStruct((m, D_MODEL), bf16)],
        scratch_shapes=[pltpu.VMEM((D_MODEL, D_MODEL), bf16), pltpu.VMEM((D_MODEL, D_MODEL), bf16),
                        pltpu.VMEM(mem_tile, f32), pltpu.VMEM(mem_tile, f32),
                        pltpu.VMEM(mem_tile, bf16), pltpu.VMEM(mem_tile, bf16),
                        pltpu.SemaphoreType.DMA((2,)), pltpu.SemaphoreType.DMA((2, X_HEADS))],
        compiler_params=_params("arbitrary", "arbitrary"),
        name="xattn_block",
    )(xb16, res, w_q, w_o, mk, mv, g, b)


def _pad_rows(x3, rows):
    return jnp.pad(x3, ((0, 0), (0, rows - x3.shape[1]), (0, 0)))


def kernel(x_prompt, x_sample, cache_moba_k, cache_moba_v, page_table, state_rwkv, state_shift, state_pool, cache_mem_k, cache_mem_v, mem_prompt, w_in_even, w_out_even, rwkv_mu, rwkv_w0, rwkv_w_up, rwkv_a0, rwkv_a_up, rwkv_k_k, rwkv_k_a, rwkv_r_k, rwkv_gn_g, rwkv_gn_b, t5_bias, w_in_odd, pool_w, pool_scale, w_out_odd, xattn_w_q, xattn_w_k, xattn_w_v, xattn_w_o, ln_mix_g, ln_mix_b, ln_x_g, ln_x_b):
    bp, tp, _ = x_prompt.shape
    bs, ts, _ = x_sample.shape
    xp = x_prompt.reshape(bp * tp, D_MODEL)
    xs = x_sample.reshape(bs * ts, D_MODEL)
    mem = mem_prompt.reshape(bp * MEM_LEN, D_MODEL)
    pt = page_table.reshape(-1)
    t5_t = t5_bias.T
    bias_tiles = _t5_tiles(t5_t)
    row2 = lambda a: a.reshape(1, -1)
    cast = lambda a: a.astype(bf16)
    col_q, col_k, col_v, col_xb, col_z = 0, A_W, 2 * A_W, 3 * A_W, 3 * A_W + B_COLS
    w_even, w_odd, xw_k, xw_v = w_in_even, w_in_odd, xattn_w_k, xattn_w_v
    w_oe, w_oo, pw, xw_q, xw_o = cast(w_out_even), cast(w_out_odd), cast(pool_w), cast(xattn_w_q), cast(xattn_w_o)
    xpb, xsb, memb = cast(xp), cast(xs), cast(mem)

    kp_l, vp_l, sp_l, shp_l, poolp_l = [], [], [], [], []
    mem_k = mem_v = jnp.zeros((DEPTH, bp, MEM_LEN, X_HEADS, X_HEAD_DIM), f32)
    ks_l, vs_l, ss_l, shs_l, pools_l = [], [], [], [], []
    for l in range(DEPTH):
        if l % 2 == 0:
            e = l // 2
            in_proj = lambda x: (_matmul(x, w_even, e, col_q, A_W), _matmul(x, w_even, e, col_k, A_W),
                                 _matmul(x, w_even, e, col_v, A_W), _matmul(x, w_even, e, col_xb, B_COLS, tn=640),
                                 _matmul(x, w_even, e, col_z, D_MODEL))
            lora = jnp.concatenate([rwkv_w_up[e], rwkv_a_up[e]], axis=0)
            rw = (row2(rwkv_mu[e]), row2(rwkv_w0[e]), row2(rwkv_a0[e]), row2(rwkv_k_k[e]), row2(rwkv_k_a[e]),
                  lora)
            gn = (row2(rwkv_r_k[e]), row2(rwkv_gn_g[e]), row2(rwkv_gn_b[e]))

            def rwkv(xb, batch, t_real, t_pad, shift_prev, s0):
                xb3 = xb.reshape(batch, t_real, B_COLS)
                if t_pad != t_real:
                    xb3 = _pad_rows(xb3, t_pad)
                prep = _rwkv_prep(xb3, shift_prev, t_real, *rw)
                y, s_new = _rwkv_scan(prep, s0, *gn)
                return y[:, :t_real].reshape(batch * t_real, B_W), xb3[:, t_real - 1], s_new

            q, k, v, xb, z = in_proj(xpb)
            a_out = _moba_prompt(q, k, v, bias_tiles, t5_t, bp, tp)
            b_out, shp, s_p = rwkv(xb, bp, tp, tp, jnp.zeros((bp, 1, B_COLS), f32),
                                   jnp.zeros((bp, B_HEADS, B_HEAD_DIM, B_HEAD_DIM), f32))
            mp_args = ([a_out, b_out], z, w_oe, e)
            kp_l.append(k.reshape(bp, tp, A_HEADS, A_HEAD_DIM))
            vp_l.append(v.reshape(bp, tp, A_HEADS, A_HEAD_DIM))
            sp_l.append(s_p)
            shp_l.append(shp)

            q, k, v, xb, z = in_proj(xsb)
            pad8 = lambda a: _pad_rows(a.reshape(bs, ts, A_W), SUBLANES)
            q8, k8, v8 = pad8(q), pad8(k), pad8(v)
            top = _moba_topk(q8, _moba_kmean(pt, cache_moba_k, e, bs), bs)
            top = top.reshape(bs, A_HEADS, SUBLANES, LANES)[:, :, :ts, :MOBA_TOPK].reshape(-1)
            a_out = _moba_sample(pt, top, t5_t, q8, k8, v8, cache_moba_k, cache_moba_v, e, bs, ts)
            a_out = a_out[:, :ts].reshape(bs * ts, A_W)
            b_out, shs, s_s = rwkv(xb, bs, ts, RWKV_CHUNK, state_shift[e].reshape(bs, 1, B_COLS), state_rwkv[e])
            ms_args = ([a_out, b_out], z, w_oe, e)
            ks_l.append(k.reshape(bs, ts, A_HEADS, A_HEAD_DIM))
            vs_l.append(v.reshape(bs, ts, A_HEADS, A_HEAD_DIM))
            ss_l.append(s_s)
            shs_l.append(shs)
        else:
            o = l // 2
            sc = row2(pool_scale[o])
            in_proj = lambda x: (_matmul(x, w_odd, o, 0, C_W), _matmul(x, w_odd, o, C_W, D_MODEL))

            xc, z = in_proj(xpb)
            xc3 = xc.reshape(bp, tp, C_W)
            y = _pool(xc3, jnp.zeros((bp, POOL_BUF + 1, C_W), f32), 0, pw, o, sc)
            mp_args = ([y.reshape(bp * tp, C_W)], z, w_oo, o)
            poolp_l.append(xc3[:, tp - POOL_BUF:])

            xc, z = in_proj(xsb)
            xc3 = xc.reshape(bs, ts, C_W)
            buf16 = jnp.pad(state_pool[o], ((0, 0), (1, 0), (0, 0)))
            y = _pool(_pad_rows(xc3, POOL_BUF + 1), buf16, PAST_LEN, pw, o, sc)[:, :ts]
            ms_args = ([y.reshape(bs * ts, C_W)], z, w_oo, o)
            pools_l.append(jnp.concatenate([state_pool[o], xc3], axis=1)[:, -POOL_BUF:])

        g, b = row2(ln_mix_g[l]), row2(ln_mix_b[l])
        xp, xpb = _proj_ln(*mp_args, xp, g, b)
        xs, xsb = _proj_ln(*ms_args, xs, g, b)

        g, b = row2(ln_x_g[l]), row2(ln_x_b[l])
        mem_k, mem_v = _mem_proj(memb, xw_k, l, mem_k, bp), _mem_proj(memb, xw_v, l, mem_v, bp)
        xp, xpb = _xattn_block(xpb, xp, xw_q, xw_o, mem_k, mem_v, l, g, b, bp, 256)
        pad = lambda a: _pad_rows(a.reshape(bs, ts, D_MODEL), BF16_ROWS).reshape(bs * BF16_ROWS, D_MODEL)
        unpad = lambda a: a.reshape(bs, BF16_ROWS, D_MODEL)[:, :ts].reshape(bs * ts, D_MODEL)
        xs, xsb = map(unpad, _xattn_block(pad(xsb), pad(xs), xw_q, xw_o, cache_mem_k, cache_mem_v, l, g, b,
                                          bs, BF16_ROWS))

    return (xp.reshape(bp, tp, D_MODEL), xs.reshape(bs, ts, D_MODEL),
            jnp.stack(kp_l), jnp.stack(vp_l), jnp.stack(sp_l), jnp.stack(shp_l), jnp.stack(poolp_l),
            mem_k, mem_v,
            jnp.stack(ks_l), jnp.stack(vs_l), jnp.stack(ss_l), jnp.stack(shs_l), jnp.stack(pools_l))
```

```python
import functools
import math

import numpy as np
import jax
import jax.numpy as jnp
from jax import lax
from jax.experimental import pallas as pl
from jax.experimental.pallas import tpu as pltpu

f32, bf16, i32 = jnp.float32, jnp.bfloat16, jnp.int32

D_MODEL = 2048
DEPTH = 4
PAST_LEN = 16384
PAGE_SIZE = 128
A_HEAD_DIM = 128
A_W = 1024
A_HEADS = 8
MOBA_BLOCK = 256
MOBA_TOPK = 3
B_HEAD_DIM = 64
B_W = 1024
B_HEADS = 16
LORA_W = 64
B_COLS = 3 * B_W + 2 * LORA_W
RWKV_GN_EPS = 64e-5
C_W = 2048
POOL_WINDOWS = (2, 4, 8, 16)
POOL_GW = C_W // len(POOL_WINDOWS)
POOL_BUF = max(POOL_WINDOWS) - 1
MEM_LEN = 256
X_HEADS = 4
X_HEAD_DIM = D_MODEL // X_HEADS
T5_BUCKETS = 32
T5_MAX_DIST = 128
LN_EPS = 1e-5
ALPHA = (2 * DEPTH) ** 0.25

NEG = -1e30
LANES = 128
SUBLANES = 8
BF16_ROWS = 2 * SUBLANES
RWKV_CHUNK = 64
GROUP_HEADS = LANES // B_HEAD_DIM
GROUP = LANES
VMEM_LIMIT = 48 * 1024 * 1024

NT = (((1,), (1,)), ((), ()))
TN = (((0,), (0,)), ((), ()))


def _t5_thresholds():
    exact = T5_BUCKETS // 2
    rel = np.arange(0, 4 * T5_MAX_DIST)
    relf = np.maximum(rel, exact).astype(np.float32)
    large = exact + (np.log(relf / np.float32(exact)) / np.float32(math.log(T5_MAX_DIST / exact))
                     * np.float32(T5_BUCKETS - exact)).astype(np.int32)
    bucket = np.where(rel < exact, rel, np.minimum(large, T5_BUCKETS - 1))
    assert (np.diff(bucket) >= 0).all() and bucket[-1] == T5_BUCKETS - 1
    return [int(np.argmax(bucket >= b)) for b in range(T5_BUCKETS)]


T5_THR = _t5_thresholds()


def _dot(a, b, dn=None):
    if dn is None:
        return jnp.dot(a, b, preferred_element_type=f32)
    return lax.dot_general(a, b, dn, preferred_element_type=f32)


def _split(x):
    hi = x.astype(bf16)
    lo = (x - hi.astype(f32)).astype(bf16)
    return hi, lo


def _dot3(a, b, dn=None):
    ah, al = _split(a)
    bh, bl = _split(b)
    return _dot(ah, bh, dn) + _dot(ah, bl, dn) + _dot(al, bh, dn)


def _dot2l(a, b_exact, dn=None):
    ah, al = _split(a)
    return _dot(ah, b_exact, dn) + _dot(al, b_exact, dn)


def _t5_bias_of(rel, tab):
    bias = jnp.full(rel.shape, tab(0), f32)
    for b in range(1, T5_BUCKETS):
        bias = jnp.where(rel >= T5_THR[b], tab(b), bias)
    return bias


def _params(*sem):
    return pltpu.CompilerParams(dimension_semantics=sem, vmem_limit_bytes=VMEM_LIMIT)


def _mm_body(x_ref, w_ref, o_ref):
    o_ref[...] = _dot(x_ref[...], w_ref[0].astype(bf16))


def _matmul(x, w, layer, col0, n, tn=512):
    m, k = x.shape
    tm = min(m, 2048)
    assert col0 % LANES == 0 and n % tn == 0 and m % tm == 0
    if col0 % tn == 0:
        wspec = pl.BlockSpec((1, k, tn), lambda i, j: (layer, 0, col0 // tn + j))
    else:
        wspec = pl.BlockSpec((pl.Element(1), pl.Element(k), pl.Element(tn)),
                             lambda i, j: (layer, 0, pl.multiple_of(col0 + j * tn, LANES)))
    return pl.pallas_call(
        _mm_body,
        grid=(m // tm, n // tn),
        in_specs=[pl.BlockSpec((tm, k), lambda i, j: (i, 0)), wspec],
        out_specs=pl.BlockSpec((tm, tn), lambda i, j: (i, j)),
        out_shape=jax.ShapeDtypeStruct((m, n), f32),
        compiler_params=_params("parallel", "arbitrary"),
        name="proj_matmul",
    )(x, w)


def _proj_ln_body(n_lhs, gated, *refs):
    lhs = [r[...] for r in refs[:n_lhs]]
    refs = refs[n_lhs:]
    y = lhs[0] if n_lhs == 1 else jnp.concatenate(lhs, axis=1)
    if gated:
        z = refs[0][...]
        refs = refs[1:]
        y = y * (z * jax.nn.sigmoid(z))
    w_ref, res_ref, g_ref, b_ref, o_ref, ob_ref = refs
    u = ALPHA * res_ref[...] + _dot(y.astype(bf16), w_ref[...])
    mu = jnp.mean(u, axis=-1, keepdims=True)
    d = u - mu
    var = jnp.mean(d * d, axis=-1, keepdims=True)
    out = d * lax.rsqrt(var + LN_EPS) * g_ref[...] + b_ref[...]
    o_ref[...] = out
    ob_ref[...] = out.astype(bf16)


def _proj_ln(lhs, z, w, layer, res, g, b):
    m = res.shape[0]
    tm = min(m, 256)
    row = lambda width: pl.BlockSpec((tm, width), lambda i: (i, 0))
    full = lambda a: pl.BlockSpec(a.shape, lambda i: (0, 0))
    args = list(lhs) + ([z] if z is not None else []) + [w, res, g, b]
    specs = [row(a.shape[1]) for a in lhs] + ([row(z.shape[1])] if z is not None else []) + [
        pl.BlockSpec((None,) + w.shape[1:], lambda i: (layer, 0, 0)), row(D_MODEL), full(g), full(b)]
    return pl.pallas_call(
        functools.partial(_proj_ln_body, len(lhs), z is not None),
        grid=(m // tm,),
        in_specs=specs,
        out_specs=[row(D_MODEL), row(D_MODEL)],
        out_shape=[jax.ShapeDtypeStruct((m, D_MODEL), f32), jax.ShapeDtypeStruct((m, D_MODEL), bf16)],
        compiler_params=_params("parallel"),
        name="proj_ln",
    )(*args)


def _t5_tiles_body(t5_ref, o_ref):
    h = pl.program_id(0)
    key = lax.broadcasted_iota(i32, (MOBA_BLOCK, MOBA_BLOCK), 0)
    qry = lax.broadcasted_iota(i32, (MOBA_BLOCK, MOBA_BLOCK), 1)
    tab = lambda b: t5_ref[h, b]
    rel = qry - key
    o_ref[0, 0] = jnp.where(rel >= 0, _t5_bias_of(jnp.maximum(rel, 0), tab), NEG)
    o_ref[0, 1] = _t5_bias_of(rel + MOBA_BLOCK, tab)


def _t5_tiles(t5_t):
    return pl.pallas_call(
        _t5_tiles_body,
        grid=(A_HEADS,),
        in_specs=[pl.BlockSpec(memory_space=pltpu.SMEM)],
        out_specs=pl.BlockSpec((1, 2, MOBA_BLOCK, MOBA_BLOCK), lambda h: (h, 0, 0, 0)),
        out_shape=jax.ShapeDtypeStruct((A_HEADS, 2, MOBA_BLOCK, MOBA_BLOCK), f32),
        compiler_params=_params("parallel"),
        name="t5_tiles",
    )(t5_t)


def _moba_prompt_body(nb, t5_ref, q_ref, k_ref, v_ref, bias_ref, o_ref):
    blkw = MOBA_BLOCK
    far_bias = t5_ref[pl.program_id(1), T5_BUCKETS - 1]
    k = k_ref[...]
    q = q_ref[...]
    kb = k.astype(bf16)
    vt = v_ref[...].T.astype(bf16)
    km = jnp.concatenate([jnp.mean(k[n * blkw:(n + 1) * blkw], axis=0, keepdims=True) for n in range(nb)], axis=0)
    gate_all = _dot3(km, q, NT)
    qs = (q * (1.0 / math.sqrt(A_HEAD_DIM))).astype(bf16)
    blk = lax.broadcasted_iota(i32, (nb, blkw), 0)

    def scores(qi):
        rows = slice(qi * blkw, (qi + 1) * blkw)
        gate = gate_all[:, rows]
        beaten = jnp.zeros(gate.shape, f32)
        for m in range(qi):
            gm = gate[m:m + 1, :]
            beaten = beaten + jnp.where((gm > gate) | ((gm == gate) & (m < blk)), 1.0, 0.0)
        sel = jnp.where((blk < qi) & (beaten < MOBA_TOPK), 0.0, NEG)
        s_all = _dot(kb[:(qi + 1) * blkw], qs[rows], NT)
        tiles = []
        for n in range(qi + 1):
            s = s_all[n * blkw:(n + 1) * blkw]
            if n == qi:
                s = s + bias_ref[0, 0]
            elif n == qi - 1:
                s = s + sel[n:n + 1, :] + bias_ref[0, 1]
            else:
                s = s + (sel[n:n + 1, :] + far_bias)
            tiles.append(s)
        return tiles

    def attend(qi, tiles):
        mx = functools.reduce(jnp.maximum, [jnp.max(s, axis=0, keepdims=True) for s in tiles])
        ps = [jnp.exp(s - mx) for s in tiles]
        den = functools.reduce(jnp.add, [jnp.sum(p, axis=0, keepdims=True) for p in ps])
        pcat = jnp.concatenate([p.astype(bf16) for p in ps], axis=0)
        acc = _dot(vt[:, :(qi + 1) * blkw], pcat)
        o_ref[qi * blkw:(qi + 1) * blkw, :] = (acc / den).T

    pending = scores(0)
    for qi in range(nb):
        ahead = scores(qi + 1) if qi + 1 < nb else None
        attend(qi, pending)
        pending = ahead


def _moba_prompt(q, k, v, bias_tiles, t5_t, batch, seq):
    assert 2 * MOBA_BLOCK - (MOBA_BLOCK - 1) >= T5_THR[-1]
    nb = seq // MOBA_BLOCK
    spec = pl.BlockSpec((seq, A_HEAD_DIM), lambda b, h: (b, h))
    return pl.pallas_call(
        functools.partial(_moba_prompt_body, nb),
        grid=(batch, A_HEADS),
        in_specs=[pl.BlockSpec(memory_space=pltpu.SMEM), spec, spec, spec,
                  pl.BlockSpec((1, 2, MOBA_BLOCK, MOBA_BLOCK), lambda b, h: (h, 0, 0, 0))],
        out_specs=spec,
        out_shape=jax.ShapeDtypeStruct((batch * seq, A_W), f32),
        compiler_params=_params("parallel", "parallel"),
        name="moba_prompt",
    )(t5_t, q, k, v, bias_tiles)


PAGES_PER_SEQ = PAST_LEN // PAGE_SIZE
PAST_BLOCKS = PAST_LEN // MOBA_BLOCK
PAGES_PER_BLOCK = MOBA_BLOCK // PAGE_SIZE
KMEAN_BLOCKS_PER_STEP = SUBLANES


def _kmean_body(pt_ref, *refs):
    pages, o_ref = refs[:-1], refs[-1]
    for jj in range(KMEAN_BLOCKS_PER_STEP):
        s = functools.reduce(jnp.add, [jnp.sum(pages[PAGES_PER_BLOCK * jj + i][0, 0], axis=0)
                                       for i in range(PAGES_PER_BLOCK)])
        s = s * (1.0 / MOBA_BLOCK)
        for h in range(A_HEADS):
            o_ref[0, h, jj:jj + 1, :] = s[h:h + 1, :]


def _moba_kmean(pt, cache_k, layer, batch):
    per_step = PAGES_PER_BLOCK * KMEAN_BLOCKS_PER_STEP
    page = lambda i: pl.BlockSpec((1, 1, PAGE_SIZE, A_HEADS, A_HEAD_DIM),
                                  lambda b, g, pt: (layer, pt[b * PAGES_PER_SEQ + g * per_step + i], 0, 0, 0))
    return pl.pallas_call(
        _kmean_body,
        grid_spec=pltpu.PrefetchScalarGridSpec(
            num_scalar_prefetch=1, grid=(batch, PAST_BLOCKS // KMEAN_BLOCKS_PER_STEP),
            in_specs=[page(i) for i in range(per_step)],
            out_specs=pl.BlockSpec((1, A_HEADS, KMEAN_BLOCKS_PER_STEP, A_HEAD_DIM), lambda b, g, pt: (b, 0, g, 0))),
        out_shape=jax.ShapeDtypeStruct((batch, A_HEADS, PAST_BLOCKS, A_HEAD_DIM), f32),
        compiler_params=_params("parallel", "arbitrary"),
        name="moba_kmean",
    )(pt, *([cache_k] * per_step))


def _moba_topk_body(q_ref, km_ref, o_ref):
    lane = lax.broadcasted_iota(i32, (SUBLANES, LANES), 1)
    for h in range(A_HEADS):
        cols = slice(h * A_HEAD_DIM, (h + 1) * A_HEAD_DIM)
        gate = _dot3(q_ref[0, :, cols], km_ref[0, h], NT)
        blk = lax.broadcasted_iota(i32, gate.shape, 1)
        out = jnp.zeros((SUBLANES, LANES), i32)
        for j in range(MOBA_TOPK):
            best = jnp.max(gate, axis=1, keepdims=True)
            idx = jnp.min(jnp.where(gate == best, blk, PAST_BLOCKS), axis=1, keepdims=True)
            out = jnp.where(lane == j, idx, out)
            gate = jnp.where(blk == idx, -jnp.inf, gate)
        o_ref[0, h * SUBLANES:(h + 1) * SUBLANES, :] = out


def _moba_topk(q8, kmean, batch):
    return pl.pallas_call(
        _moba_topk_body,
        grid=(batch,),
        in_specs=[pl.BlockSpec((1, SUBLANES, A_W), lambda b: (b, 0, 0)),
                  pl.BlockSpec((1, A_HEADS, PAST_BLOCKS, A_HEAD_DIM), lambda b: (b, 0, 0, 0))],
        out_specs=pl.BlockSpec((1, A_HEADS * SUBLANES, LANES), lambda b: (b, 0, 0)),
        out_shape=jax.ShapeDtypeStruct((batch, A_HEADS * SUBLANES, LANES), i32),
        compiler_params=_params("parallel"),
        name="moba_topk",
    )(q8, kmean)


def _moba_sample_body(t_new, layer, pt_ref, top_ref, t5_ref, q_ref, kn_ref, vn_ref, kc_hbm, vc_hbm, o_ref,
                      kbuf, vbuf, sem):
    b, h = pl.program_id(0), pl.program_id(1)
    nh = pl.num_programs(1)
    step = b * nh + h
    nstep = pl.num_programs(0) * nh
    per_q = MOBA_TOPK * PAGES_PER_BLOCK
    top_at = lambda bb, hh, qq, j: top_ref[((bb * A_HEADS + hh) * t_new + qq) * MOBA_TOPK + j]

    def copies(bb, hh, slot):
        out = []
        for qq in range(t_new):
            for j in range(MOBA_TOPK):
                blk = top_at(bb, hh, qq, j)
                for half in range(PAGES_PER_BLOCK):
                    pg = pt_ref[bb * PAGES_PER_SEQ + PAGES_PER_BLOCK * blk + half]
                    i = qq * per_q + j * PAGES_PER_BLOCK + half
                    out.append(pltpu.make_async_copy(kc_hbm.at[layer, pg, :, hh, :], kbuf.at[slot, i], sem.at[slot, 0]))
                    out.append(pltpu.make_async_copy(vc_hbm.at[layer, pg, :, hh, :], vbuf.at[slot, i], sem.at[slot, 1]))
        return out

    slot = step % 2

    @pl.when(step == 0)
    def _():
        for c in copies(b, h, slot):
            c.start()

    @pl.when(step + 1 < nstep)
    def _():
        nxt = step + 1
        for c in copies(nxt // nh, nxt % nh, 1 - slot):
            c.start()

    tab = lambda bkt: t5_ref[h, bkt]
    q = q_ref[0] * (1.0 / math.sqrt(A_HEAD_DIM))
    qb = q.astype(bf16)
    row = lax.broadcasted_iota(i32, (SUBLANES, 1), 0)
    kn, vn = kn_ref[0], vn_ref[0]
    own = []
    for t in range(t_new):
        sc = jnp.sum(q * kn[t:t + 1, :], axis=1, keepdims=True) + _t5_bias_of(jnp.maximum(row - t, 0), tab)
        own.append(jnp.where(row >= t, sc, NEG))
    own_max = functools.reduce(jnp.maximum, own)

    for c in copies(b, h, slot):
        c.wait()

    lane = lax.broadcasted_iota(i32, (1, per_q * PAGE_SIZE), 1)
    l_sum = jnp.zeros((SUBLANES, 1), f32)
    acc = jnp.zeros((SUBLANES, A_HEAD_DIM), f32)
    for qq in range(t_new):
        kq = kbuf[slot, qq * per_q:(qq + 1) * per_q].reshape(per_q * PAGE_SIZE, A_HEAD_DIM)
        vq = vbuf[slot, qq * per_q:(qq + 1) * per_q].reshape(per_q * PAGE_SIZE, A_HEAD_DIM)
        keypos = lane - (MOBA_TOPK - 1) * MOBA_BLOCK + top_at(b, h, qq, MOBA_TOPK - 1) * MOBA_BLOCK
        for j in range(MOBA_TOPK - 2, -1, -1):
            keypos = jnp.where(lane < (j + 1) * MOBA_BLOCK, lane - j * MOBA_BLOCK + top_at(b, h, qq, j) * MOBA_BLOCK,
                               keypos)
        lg = _dot(qb, kq.astype(bf16), NT) + _t5_bias_of(PAST_LEN + qq - keypos, tab)
        mine = row == qq
        m = jnp.maximum(own_max, jnp.max(lg, axis=1, keepdims=True))
        p = jnp.where(mine, jnp.exp(lg - m), 0.0)
        l_sum = l_sum + jnp.sum(p, axis=1, keepdims=True)
        acc = acc + _dot(p.astype(bf16), vq.astype(bf16))
        for t in range(t_new):
            po = jnp.where(mine, jnp.exp(own[t] - m), 0.0)
            l_sum = l_sum + po
            acc = acc + po * vn[t:t + 1, :]
    o_ref[0] = acc / jnp.where(row < t_new, l_sum, 1.0)


def _moba_sample(pt, top, t5_t, q8, kn8, vn8, cache_k, cache_v, layer, batch, t_new):
    npage = t_new * MOBA_TOPK * PAGES_PER_BLOCK
    rows = pl.BlockSpec((1, SUBLANES, A_HEAD_DIM), lambda b, h, pt, top: (b, 0, h))
    hbm = pl.BlockSpec(memory_space=pl.ANY)
    return pl.pallas_call(
        functools.partial(_moba_sample_body, t_new, layer),
        grid_spec=pltpu.PrefetchScalarGridSpec(
            num_scalar_prefetch=2, grid=(batch, A_HEADS),
            in_specs=[pl.BlockSpec(memory_space=pltpu.SMEM), rows, rows, rows, hbm, hbm],
            out_specs=rows,
            scratch_shapes=[pltpu.VMEM((2, npage, PAGE_SIZE, A_HEAD_DIM), f32),
                            pltpu.VMEM((2, npage, PAGE_SIZE, A_HEAD_DIM), f32),
                            pltpu.SemaphoreType.DMA((2, 2))]),
        out_shape=jax.ShapeDtypeStruct((batch, SUBLANES, A_W), f32),
        compiler_params=_params("arbitrary", "arbitrary"),
        name="moba_sample",
    )(pt, top, t5_t, q8, kn8, vn8, cache_k, cache_v)


def _rwkv_prep_body(t_valid, t_total, xb_ref, halo_ref, sp_ref, mu_ref, w0_ref, a0_ref, kk_ref, ka_ref, lora_ref,
                    r_ref, l_ref, a_ref, b_ref, k_ref, v_ref):
    i = pl.program_id(1)
    x = xb_ref[0]
    tt = x.shape[0]
    row = lax.broadcasted_iota(i32, (tt, 1), 0)
    first = jnp.where(i == 0, sp_ref[0], halo_ref[0, SUBLANES - 1:SUBLANES, :])
    prev = jnp.where(row == 0, first, pltpu.roll(x, 1, 0))
    xm = x + (prev - x) * mu_ref[...]
    r, k, v = xm[:, :B_W], xm[:, B_W:2 * B_W], xm[:, 2 * B_W:3 * B_W]
    wa = xm[:, 3 * B_W:]
    lane = lax.broadcasted_iota(i32, wa.shape, 1)
    lw = _dot3(jnp.where(lane < LORA_W, jnp.tanh(wa), 0.0), lora_ref[...])
    la = _dot3(jnp.where(lane < LORA_W, 0.0, wa), lora_ref[...])
    wlog = -math.exp(-0.5) * jax.nn.sigmoid(w0_ref[...] + lw)
    a = jax.nn.sigmoid(a0_ref[...] + la)
    kk = k * kk_ref[...]
    sq = kk * kk
    ss = jnp.concatenate(_head_reduce([sq[:, p * GROUP:(p + 1) * GROUP] for p in range(B_W // GROUP)],
                                      jnp.where(_same_head(), 1.0, 0.0).astype(bf16)), axis=1)
    kk = kk * lax.rsqrt(jnp.maximum(ss, 1e-24))
    kp = k * (1.0 + (a - 1.0) * ka_ref[...])
    outs = (r, wlog, -kk, kk * a, kp, v)
    if t_valid < t_total:
        ok = (i * tt + row) < t_valid
        outs = tuple(jnp.where(ok, o, 0.0) for o in outs)
    for ref, o in zip((r_ref, l_ref, a_ref, b_ref, k_ref, v_ref), outs):
        ref[0] = o


def _rwkv_prep(xb3, shift_prev, t_valid, mu, w0, a0, k_k, k_a, lora):
    bn, tp, _ = xb3.shape
    tt = min(tp, 256)
    halo_blocks = tt // SUBLANES
    tile = lambda width: pl.BlockSpec((1, tt, width), lambda b, i: (b, i, 0))
    vec = lambda a: pl.BlockSpec(a.shape, lambda b, i: (0,) * a.ndim)
    out = jax.ShapeDtypeStruct((bn, tp, B_W), f32)
    return pl.pallas_call(
        functools.partial(_rwkv_prep_body, t_valid, tp),
        grid=(bn, tp // tt),
        in_specs=[tile(B_COLS),
                  pl.BlockSpec((1, SUBLANES, B_COLS), lambda b, i: (b, jnp.maximum(i * halo_blocks - 1, 0), 0)),
                  pl.BlockSpec((1, 1, B_COLS), lambda b, i: (b, 0, 0)),
                  vec(mu), vec(w0), vec(a0), vec(k_k), vec(k_a), vec(lora)],
        out_specs=[tile(B_W)] * 6,
        out_shape=[out] * 6,
        compiler_params=_params("parallel", "arbitrary"),
        name="rwkv_prep",
    )(xb3, xb3, shift_prev, mu, w0, a0, k_k, k_a, lora)


def _head_of(idx):
    return idx // B_HEAD_DIM


def _same_head():
    return (_head_of(lax.broadcasted_iota(i32, (GROUP, GROUP), 0))
            == _head_of(lax.broadcasted_iota(i32, (GROUP, GROUP), 1)))


def _stack(p, head):
    return jnp.concatenate([jnp.where(head == h, p, 0.0) for h in range(GROUP_HEADS)], axis=0)


def _head_reduce(parts, mat):
    n, c = len(parts), parts[0].shape[0]
    halves = [_split(a) for a in parts]
    out = _dot(jnp.concatenate([h for h, _ in halves] + [l for _, l in halves], axis=0), mat)
    return [out[i * c:(i + 1) * c] + out[(n + i) * c:(n + i + 1) * c] for i in range(n)]


def _rwkv_scan_body(npair, r_ref, l_ref, a_ref, b_ref, k_ref, v_ref, s0_ref, rk_ref, gg_ref, gb_ref, y_ref, s_ref, bd_ref):
    hd = B_HEAD_DIM
    heads = range(GROUP_HEADS)

    @pl.when(pl.program_id(1) == 0)
    def _():
        zero = jnp.zeros((hd, hd), f32)
        for p in range(npair):
            bd_ref[p] = jnp.concatenate(
                [jnp.concatenate([s0_ref[0, GROUP_HEADS * p + h].T if h == g else zero for g in heads], axis=1)
                 for h in heads], axis=0)

    c = RWKV_CHUNK
    pairs = range(npair)
    col = lambda x, p: x[:, p * GROUP:(p + 1) * GROUP]
    lane = lax.broadcasted_iota(i32, (c, GROUP), 1)
    t = lax.broadcasted_iota(i32, (c, GROUP), 0)
    j = lane & (B_HEAD_DIM - 1)
    strict, incl = t > j, t >= j
    eye_c = jnp.where(t == j, 1.0, 0.0)
    rowi = lax.broadcasted_iota(i32, (GROUP, GROUP), 0)
    coli = lax.broadcasted_iota(i32, (GROUP, GROUP), 1)
    same_head = _same_head()
    head_sum = jnp.where(same_head, 1.0, 0.0).astype(bf16)
    head_mean = jnp.where(same_head, 1.0 / B_HEAD_DIM, 0.0).astype(bf16)
    head_lane = _head_of(lane)
    st = lambda x: _stack(x, head_lane).astype(bf16)
    cast = lambda x: x.astype(bf16)

    R, L, A, B, K, V = (ref[0] for ref in (r_ref, l_ref, a_ref, b_ref, k_ref, v_ref))
    tri = jnp.where(lax.broadcasted_iota(i32, (c, c), 0) >= lax.broadcasted_iota(i32, (c, c), 1), 1.0, 0.0).astype(bf16)
    l1, l2 = _split(L)
    l3 = (L - l1.astype(f32) - l2.astype(f32)).astype(bf16)
    cum = _dot(tri, l1) + _dot(tri, l2) + _dot(tri, l3)
    last = cum[c - 1:c, :]
    wc = jnp.exp(last)
    At, Rt = A * jnp.exp(cum - L), R * jnp.exp(cum)
    einv, ew = jnp.exp(-cum), jnp.exp(last - cum)
    Bt, Kt, Bw, Kw = B * einv, K * einv, B * ew, K * ew

    ar = [cast(jnp.concatenate([col(At, p), col(Rt, p)], axis=0)) for p in pairs]
    pb = [_dot(ar[p], st(col(Bt, p)), NT) for p in pairs]
    pk = [_dot(ar[p], st(col(Kt, p)), NT) for p in pairs]
    pab = [jnp.where(strict, x[:c], 0.0) for x in pb]
    prb = [jnp.where(incl, x[c:], 0.0) for x in pb]
    pak = [jnp.where(strict, x[:c], 0.0) for x in pk]
    prk = [jnp.where(incl, x[c:], 0.0) for x in pk]
    x = [eye_c + m for m in pab]
    qp = pab
    for _ in range(int(math.log2(c)) - 1):
        qp = [_dot(cast(m), st(m)) for m in qp]
        x = [xi + _dot(cast(xi), st(m)) for xi, m in zip(x, qp)]
    akv = [_dot(cast(pak[p]), st(col(V, p))) for p in pairs]
    ua = [_dot(cast(x[p]), jnp.concatenate([st(akv[p]), st(col(At, p))], axis=1)) for p in pairs]
    u0 = [m[:, :GROUP] for m in ua]
    ah = [m[:, GROUP:] for m in ua]
    ry = [_dot(cast(prb[p]), jnp.concatenate([st(ah[p]), st(u0[p])], axis=1)) for p in pairs]
    rh = [col(Rt, p) + ry[p][:, :GROUP] for p in pairs]
    y0 = [ry[p][:, GROUP:] + _dot(cast(prk[p]), st(col(V, p))) for p in pairs]
    z1 = [_dot(cast(col(Bw, p)), cast(jnp.concatenate([ah[p], u0[p]], axis=1)), TN) for p in pairs]
    z2 = [_dot(cast(col(Kw, p)), cast(col(V, p)), TN) for p in pairs]
    mt = [jnp.where(rowi == coli, col(wc, p), 0.0) + jnp.where(same_head, z1[p][:, :GROUP], 0.0) for p in pairs]
    gt = [jnp.where(same_head, z1[p][:, GROUP:] + z2[p], 0.0) for p in pairs]
    nl = c + GROUP
    lhs = [_split(jnp.concatenate([rh[p], mt[p]], axis=0)) for p in pairs]
    bd = [_split(bd_ref[p]) for p in pairs]
    main = [_dot(jnp.concatenate(lhs[p], axis=0), bd[p][0]) for p in pairs]
    corr = [_dot(lhs[p][0], bd[p][1]) for p in pairs]
    prod = [main[p][:nl] + main[p][nl:] + corr[p] for p in pairs]
    new_bd = [prod[p][c:] + gt[p] for p in pairs]
    for p in pairs:
        bd_ref[p] = new_bd[p]

    @pl.when(pl.program_id(1) == pl.num_programs(1) - 1)
    def _():
        for p in pairs:
            for h in heads:
                s_ref[0, GROUP_HEADS * p + h] = new_bd[p][h * hd:(h + 1) * hd, h * hd:(h + 1) * hd].T

    y = [prod[p][:c] + y0[p] for p in pairs]
    mean = _head_reduce(y, head_mean)
    d = [yi - m for yi, m in zip(y, mean)]
    var = _head_reduce([di * di for di in d], head_mean)
    bonus = _head_reduce([col(R, p) * col(K, p) * col(rk_ref[...], p) for p in pairs], head_sum)
    for p in pairs:
        yn = d[p] * lax.rsqrt(var[p] + RWKV_GN_EPS) * col(gg_ref[...], p) + col(gb_ref[...], p)
        y_ref[0, :, p * GROUP:(p + 1) * GROUP] = yn + bonus[p] * col(V, p)


def _rwkv_scan(prep, s0, r_k, gn_g, gn_b):
    bn, tp, _ = prep[0].shape
    npair = B_HEADS // GROUP_HEADS
    tile = pl.BlockSpec((1, RWKV_CHUNK, B_W), lambda b, c: (b, c, 0))
    state = pl.BlockSpec((1, B_HEADS, B_HEAD_DIM, B_HEAD_DIM), lambda b, c: (b, 0, 0, 0))
    vec = pl.BlockSpec((1, B_W), lambda b, c: (0, 0))
    return pl.pallas_call(
        functools.partial(_rwkv_scan_body, npair),
        grid=(bn, tp // RWKV_CHUNK),
        in_specs=[tile] * 6 + [state, vec, vec, vec],
        out_specs=[tile, state],
        out_shape=[jax.ShapeDtypeStruct((bn, tp, B_W), f32), jax.ShapeDtypeStruct(s0.shape, f32)],
        scratch_shapes=[pltpu.VMEM((npair, GROUP, GROUP), f32)],
        compiler_params=_params("parallel", "arbitrary"),
        name="rwkv_scan",
    )(*prep, s0, r_k, gn_g, gn_b)


def _pool_body(pos0, x_ref, halo_ref, buf_ref, pw_ref, sc_ref, o_ref):
    i = pl.program_id(1)
    x = x_ref[0]
    tt = x.shape[0]
    halo = jnp.where(i == 0, buf_ref[0], halo_ref[0])
    xe = jnp.concatenate([halo, x], axis=0)
    row = lax.broadcasted_iota(i32, (tt, 1), 0)
    pos = pos0 + i * tt + row
    outs = []
    for g, w in enumerate(POOL_WINDOWS):
        cols = slice(g * POOL_GW, (g + 1) * POOL_GW)
        s = xe[:, cols]
        sh = 1
        while sh < w:
            s = s + pltpu.roll(s, sh, 0)
            sh *= 2
        cnt = jnp.minimum(w, pos + 1).astype(f32)
        pooled = s[POOL_BUF + 1:] / cnt - x[:, cols]
        outs.append(_dot(pooled.astype(bf16), pw_ref[g]))
    o_ref[0] = jnp.concatenate(outs, axis=1) * sc_ref[...]


def _pool(xc3, buf16, pos0, pool_w, layer, scale):
    bn, tp, _ = xc3.shape
    tt = min(tp, 256)
    hb = POOL_BUF + 1
    return pl.pallas_call(
        functools.partial(_pool_body, pos0),
        grid=(bn, tp // tt),
        in_specs=[pl.BlockSpec((1, tt, C_W), lambda b, i: (b, i, 0)),
                  pl.BlockSpec((1, hb, C_W), lambda b, i: (b, jnp.maximum(i * (tt // hb) - 1, 0), 0)),
                  pl.BlockSpec((1, hb, C_W), lambda b, i: (b, 0, 0)),
                  pl.BlockSpec((None,) + pool_w.shape[1:], lambda b, i: (layer, 0, 0, 0)),
                  pl.BlockSpec(scale.shape, lambda b, i: (0, 0))],
        out_specs=pl.BlockSpec((1, tt, C_W), lambda b, i: (b, i, 0)),
        out_shape=jax.ShapeDtypeStruct((bn, tp, C_W), f32),
        compiler_params=_params("parallel", "arbitrary"),
        name="pool_mix",
    )(xc3, xc3, buf16, pool_w, scale)


MEM_PROJ_HEADS = 2


def _mem_proj_body(layer, batch, x_ref, w_ref, prev_hbm, o_hbm, ybuf, sem):
    del prev_hbm
    y = _dot(x_ref[...], w_ref[0].astype(bf16))
    for hh in range(MEM_PROJ_HEADS):
        ybuf[hh] = y[:, hh * X_HEAD_DIM:(hh + 1) * X_HEAD_DIM]
    head0 = pl.program_id(0) * MEM_PROJ_HEADS
    copies = [pltpu.make_async_copy(ybuf.at[hh, b * MEM_LEN:(b + 1) * MEM_LEN], o_hbm.at[layer, b, :, head0 + hh, :],
                                    sem.at[hh, b])
              for hh in range(MEM_PROJ_HEADS) for b in range(batch)]
    for c in copies:
        c.start()
    for c in copies:
        c.wait()


def _mem_proj(x, w, layer, prev, batch):
    tn = MEM_PROJ_HEADS * X_HEAD_DIM
    return pl.pallas_call(
        functools.partial(_mem_proj_body, layer, batch),
        grid=(X_HEADS // MEM_PROJ_HEADS,),
        in_specs=[pl.BlockSpec(x.shape, lambda j: (0, 0)),
                  pl.BlockSpec((1, w.shape[1], tn), lambda j: (layer, 0, j)),
                  pl.BlockSpec(memory_space=pl.ANY)],
        out_specs=pl.BlockSpec(memory_space=pl.ANY),
        out_shape=jax.ShapeDtypeStruct(prev.shape, f32),
        scratch_shapes=[pltpu.VMEM((MEM_PROJ_HEADS, x.shape[0], X_HEAD_DIM), f32),
                        pltpu.SemaphoreType.DMA((MEM_PROJ_HEADS, batch))],
        input_output_aliases={2: 0},
        compiler_params=_params("arbitrary"),
        name="mem_proj",
    )(x, w, prev)


def _xattn_block_body(layer, nbt, x_ref, res_ref, wq_hbm, wo_hbm, k_hbm, v_hbm, g_ref, b_ref, o_ref, ob_ref,
                      wq, wo, kstage, vstage, kb, vb, wsem, sem):
    bg, i = pl.program_id(0), pl.program_id(1)

    @pl.when((bg == 0) & (i == 0))
    def _():
        copies = [pltpu.make_async_copy(wq_hbm.at[layer], wq, wsem.at[0]),
                  pltpu.make_async_copy(wo_hbm.at[layer], wo, wsem.at[1])]
        for c in copies:
            c.start()
        for c in copies:
            c.wait()

    @pl.when(i == 0)
    def _():
        for j in range(nbt):
            copies = [pltpu.make_async_copy(src.at[layer, bg * nbt + j, :, h, :], dst.at[h], sem.at[n, h])
                      for n, (src, dst) in enumerate(((k_hbm, kstage), (v_hbm, vstage))) for h in range(X_HEADS)]
            for c in copies:
                c.start()
            for c in copies:
                c.wait()
            kb[j] = kstage[...].astype(bf16)
            vb[j] = vstage[...].astype(bf16)

    q = _dot(x_ref[...], wq[...])
    rows = q.shape[0] // nbt
    per_batch = []
    for j in range(nbt):
        heads = []
        for h in range(X_HEADS):
            qh = q[j * rows:(j + 1) * rows, h * X_HEAD_DIM:(h + 1) * X_HEAD_DIM]
            s = _dot((qh * (1.0 / math.sqrt(X_HEAD_DIM))).astype(bf16), kb[j, h], NT)
            p = jnp.exp(s - jnp.max(s, axis=-1, keepdims=True))
            heads.append((_dot(p.astype(bf16), vb[j, h]) / jnp.sum(p, axis=-1, keepdims=True)).astype(bf16))
        per_batch.append(jnp.concatenate(heads, axis=1))
    att = per_batch[0] if nbt == 1 else jnp.concatenate(per_batch, axis=0)
    u = ALPHA * res_ref[...] + _dot(att, wo[...])
    mu = jnp.mean(u, axis=-1, keepdims=True)
    d = u - mu
    var = jnp.mean(d * d, axis=-1, keepdims=True)
    out = d * lax.rsqrt(var + LN_EPS) * g_ref[...] + b_ref[...]
    o_ref[...] = out
    ob_ref[...] = out.astype(bf16)


def _xattn_block(xb16, res, w_q, w_o, mk, mv, layer, g, b, batch, tq):
    m = res.shape[0]
    nt = m // batch // tq
    kv_bytes = 2 * X_HEADS * MEM_LEN * X_HEAD_DIM * 2
    nbt = max(n for n in range(1, batch + 1) if batch % n == 0 and n * kv_bytes <= VMEM_LIMIT // 4) if nt == 1 else 1
    row = pl.BlockSpec((nbt * tq, D_MODEL), lambda bg, i: (bg * nt + i, 0))
    vec = pl.BlockSpec((1, D_MODEL), lambda bg, i: (0, 0))
    hbm = pl.BlockSpec(memory_space=pl.ANY)
    mem_tile = (X_HEADS, MEM_LEN, X_HEAD_DIM)
    return pl.pallas_call(
        functools.partial(_xattn_block_body, layer, nbt),
        grid=(batch // nbt, nt),
        in_specs=[row, row, hbm, hbm, hbm, hbm, vec, vec],
        out_specs=[row, row],
        out_shape=[jax.ShapeDtypeStruct((m, D_MODEL), f32), jax.ShapeDtypeStruct((m, D_MODEL), bf16)],
        scratch_shapes=[pltpu.VMEM((D_MODEL, D_MODEL), bf16), pltpu.VMEM((D_MODEL, D_MODEL), bf16),
                        pltpu.VMEM(mem_tile, f32), pltpu.VMEM(mem_tile, f32),
                        pltpu.VMEM((nbt,) + mem_tile, bf16), pltpu.VMEM((nbt,) + mem_tile, bf16),
                        pltpu.SemaphoreType.DMA((2,)), pltpu.SemaphoreType.DMA((2, X_HEADS))],
        compiler_params=_params("arbitrary", "arbitrary"),
        name="xattn_block",
    )(xb16, res, w_q, w_o, mk, mv, g, b)


def _pad_rows(x3, rows):
    return jnp.pad(x3, ((0, 0), (0, rows - x3.shape[1]), (0, 0)))


def kernel(x_prompt, x_sample, cache_moba_k, cache_moba_v, page_table, state_rwkv, state_shift, state_pool, cache_mem_k, cache_mem_v, mem_prompt, w_in_even, w_out_even, rwkv_mu, rwkv_w0, rwkv_w_up, rwkv_a0, rwkv_a_up, rwkv_k_k, rwkv_k_a, rwkv_r_k, rwkv_gn_g, rwkv_gn_b, t5_bias, w_in_odd, pool_w, pool_scale, w_out_odd, xattn_w_q, xattn_w_k, xattn_w_v, xattn_w_o, ln_mix_g, ln_mix_b, ln_x_g, ln_x_b):
    bp, tp, _ = x_prompt.shape
    bs, ts, _ = x_sample.shape
    xp = x_prompt.reshape(bp * tp, D_MODEL)
    xs = x_sample.reshape(bs * ts, D_MODEL)
    mem = mem_prompt.reshape(bp * MEM_LEN, D_MODEL)
    pt = page_table.reshape(-1)
    t5_t = t5_bias.T
    bias_tiles = _t5_tiles(t5_t)
    row2 = lambda a: a.reshape(1, -1)
    cast = lambda a: a.astype(bf16)
    col_q, col_k, col_v, col_xb, col_z = 0, A_W, 2 * A_W, 3 * A_W, 3 * A_W + B_COLS
    w_even, w_odd, xw_k, xw_v = w_in_even, w_in_odd, xattn_w_k, xattn_w_v
    w_oe, w_oo, pw, xw_q, xw_o = cast(w_out_even), cast(w_out_odd), cast(pool_w), cast(xattn_w_q), cast(xattn_w_o)
    xpb, xsb, memb = cast(xp), cast(xs), cast(mem)

    kp_l, vp_l, sp_l, shp_l, poolp_l = [], [], [], [], []
    mem_k = mem_v = jnp.zeros((DEPTH, bp, MEM_LEN, X_HEADS, X_HEAD_DIM), f32)
    ks_l, vs_l, ss_l, shs_l, pools_l = [], [], [], [], []
    for l in range(DEPTH):
        if l % 2 == 0:
            e = l // 2
            in_proj = lambda x: (_matmul(x, w_even, e, col_q, A_W), _matmul(x, w_even, e, col_k, A_W),
                                 _matmul(x, w_even, e, col_v, A_W), _matmul(x, w_even, e, col_xb, B_COLS, tn=640),
                                 _matmul(x, w_even, e, col_z, D_MODEL))
            lora = jnp.concatenate([rwkv_w_up[e], rwkv_a_up[e]], axis=0)
            rw = (row2(rwkv_mu[e]), row2(rwkv_w0[e]), row2(rwkv_a0[e]), row2(rwkv_k_k[e]), row2(rwkv_k_a[e]),
                  lora)
            gn = (row2(rwkv_r_k[e]), row2(rwkv_gn_g[e]), row2(rwkv_gn_b[e]))

            def rwkv(xb, batch, t_real, t_pad, shift_prev, s0):
                xb3 = xb.reshape(batch, t_real, B_COLS)
                if t_pad != t_real:
                    xb3 = _pad_rows(xb3, t_pad)
                prep = _rwkv_prep(xb3, shift_prev, t_real, *rw)
                y, s_new = _rwkv_scan(prep, s0, *gn)
                return y[:, :t_real].reshape(batch * t_real, B_W), xb3[:, t_real - 1], s_new

            q, k, v, xb, z = in_proj(xpb)
            a_out = _moba_prompt(q, k, v, bias_tiles, t5_t, bp, tp)
            b_out, shp, s_p = rwkv(xb, bp, tp, tp, jnp.zeros((bp, 1, B_COLS), f32),
                                   jnp.zeros((bp, B_HEADS, B_HEAD_DIM, B_HEAD_DIM), f32))
            mp_args = ([a_out, b_out], z, w_oe, e)
            kp_l.append(k.reshape(bp, tp, A_HEADS, A_HEAD_DIM))
            vp_l.append(v.reshape(bp, tp, A_HEADS, A_HEAD_DIM))
            sp_l.append(s_p)
            shp_l.append(shp)

            q, k, v, xb, z = in_proj(xsb)
            pad8 = lambda a: _pad_rows(a.reshape(bs, ts, A_W), SUBLANES)
            q8, k8, v8 = pad8(q), pad8(k), pad8(v)
            top = _moba_topk(q8, _moba_kmean(pt, cache_moba_k, e, bs), bs)
            top = top.reshape(bs, A_HEADS, SUBLANES, LANES)[:, :, :ts, :MOBA_TOPK].reshape(-1)
            a_out = _moba_sample(pt, top, t5_t, q8, k8, v8, cache_moba_k, cache_moba_v, e, bs, ts)
            a_out = a_out[:, :ts].reshape(bs * ts, A_W)
            b_out, shs, s_s = rwkv(xb, bs, ts, RWKV_CHUNK, state_shift[e].reshape(bs, 1, B_COLS), state_rwkv[e])
            ms_args = ([a_out, b_out], z, w_oe, e)
            ks_l.append(k.reshape(bs, ts, A_HEADS, A_HEAD_DIM))
            vs_l.append(v.reshape(bs, ts, A_HEADS, A_HEAD_DIM))
            ss_l.append(s_s)
            shs_l.append(shs)
        else:
            o = l // 2
            sc = row2(pool_scale[o])
            in_proj = lambda x: (_matmul(x, w_odd, o, 0, C_W), _matmul(x, w_odd, o, C_W, D_MODEL))

            xc, z = in_proj(xpb)
            xc3 = xc.reshape(bp, tp, C_W)
            y = _pool(xc3, jnp.zeros((bp, POOL_BUF + 1, C_W), f32), 0, pw, o, sc)
            mp_args = ([y.reshape(bp * tp, C_W)], z, w_oo, o)
            poolp_l.append(xc3[:, tp - POOL_BUF:])

            xc, z = in_proj(xsb)
            xc3 = xc.reshape(bs, ts, C_W)
            buf16 = jnp.pad(state_pool[o], ((0, 0), (1, 0), (0, 0)))
            y = _pool(_pad_rows(xc3, POOL_BUF + 1), buf16, PAST_LEN, pw, o, sc)[:, :ts]
            ms_args = ([y.reshape(bs * ts, C_W)], z, w_oo, o)
            pools_l.append(jnp.concatenate([state_pool[o], xc3], axis=1)[:, -POOL_BUF:])

        g, b = row2(ln_mix_g[l]), row2(ln_mix_b[l])
        xp, xpb = _proj_ln(*mp_args, xp, g, b)
        xs, xsb = _proj_ln(*ms_args, xs, g, b)

        g, b = row2(ln_x_g[l]), row2(ln_x_b[l])
        mem_k, mem_v = _mem_proj(memb, xw_k, l, mem_k, bp), _mem_proj(memb, xw_v, l, mem_v, bp)
        xp, xpb = _xattn_block(xpb, xp, xw_q, xw_o, mem_k, mem_v, l, g, b, bp, 256)
        pad = lambda a: _pad_rows(a.reshape(bs, ts, D_MODEL), BF16_ROWS).reshape(bs * BF16_ROWS, D_MODEL)
        unpad = lambda a: a.reshape(bs, BF16_ROWS, D_MODEL)[:, :ts].reshape(bs * ts, D_MODEL)
        xs, xsb = map(unpad, _xattn_block(pad(xsb), pad(xs), xw_q, xw_o, cache_mem_k, cache_mem_v, l, g, b,
                                          bs, BF16_ROWS))

    return (xp.reshape(bp, tp, D_MODEL), xs.reshape(bs, ts, D_MODEL),
            jnp.stack(kp_l), jnp.stack(vp_l), jnp.stack(sp_l), jnp.stack(shp_l), jnp.stack(poolp_l),
            mem_k, mem_v,
            jnp.stack(ks_l), jnp.stack(vs_l), jnp.stack(ss_l), jnp.stack(shs_l), jnp.stack(pools_l))
```

```python
import functools
import math

import numpy as np
import jax
import jax.numpy as jnp
from jax import lax
from jax.experimental import pallas as pl
from jax.experimental.pallas import tpu as pltpu

f32, bf16, i32 = jnp.float32, jnp.bfloat16, jnp.int32

D_MODEL = 2048
DEPTH = 4
PAST_LEN = 16384
PAGE_SIZE = 128
A_HEAD_DIM = 128
A_W = 1024
A_HEADS = 8
MOBA_BLOCK = 256
MOBA_TOPK = 3
B_HEAD_DIM = 64
B_W = 1024
B_HEADS = 16
LORA_W = 64
B_COLS = 3 * B_W + 2 * LORA_W
RWKV_GN_EPS = 64e-5
C_W = 2048
POOL_WINDOWS = (2, 4, 8, 16)
POOL_GW = C_W // len(POOL_WINDOWS)
POOL_BUF = max(POOL_WINDOWS) - 1
MEM_LEN = 256
X_HEADS = 4
X_HEAD_DIM = D_MODEL // X_HEADS
T5_BUCKETS = 32
T5_MAX_DIST = 128
LN_EPS = 1e-5
ALPHA = (2 * DEPTH) ** 0.25

NEG = -1e30
LANES = 128
SUBLANES = 8
BF16_ROWS = 2 * SUBLANES
RWKV_CHUNK = 64
GROUP_HEADS = LANES // B_HEAD_DIM
GROUP = LANES
VMEM_LIMIT = 48 * 1024 * 1024

NT = (((1,), (1,)), ((), ()))
TN = (((0,), (0,)), ((), ()))


def _t5_thresholds():
    exact = T5_BUCKETS // 2
    rel = np.arange(0, 4 * T5_MAX_DIST)
    relf = np.maximum(rel, exact).astype(np.float32)
    large = exact + (np.log(relf / np.float32(exact)) / np.float32(math.log(T5_MAX_DIST / exact))
                     * np.float32(T5_BUCKETS - exact)).astype(np.int32)
    bucket = np.where(rel < exact, rel, np.minimum(large, T5_BUCKETS - 1))
    assert (np.diff(bucket) >= 0).all() and bucket[-1] == T5_BUCKETS - 1
    return [int(np.argmax(bucket >= b)) for b in range(T5_BUCKETS)]


T5_THR = _t5_thresholds()


def _dot(a, b, dn=None):
    if dn is None:
        return jnp.dot(a, b, preferred_element_type=f32)
    return lax.dot_general(a, b, dn, preferred_element_type=f32)


def _split(x):
    hi = x.astype(bf16)
    lo = (x - hi.astype(f32)).astype(bf16)
    return hi, lo


def _dot3(a, b, dn=None):
    ah, al = _split(a)
    bh, bl = _split(b)
    return _dot(ah, bh, dn) + _dot(ah, bl, dn) + _dot(al, bh, dn)


def _dot2l(a, b_exact, dn=None):
    ah, al = _split(a)
    return _dot(ah, b_exact, dn) + _dot(al, b_exact, dn)


def _t5_bias_of(rel, tab):
    bias = jnp.full(rel.shape, tab(0), f32)
    for b in range(1, T5_BUCKETS):
        bias = jnp.where(rel >= T5_THR[b], tab(b), bias)
    return bias


def _params(*sem):
    return pltpu.CompilerParams(dimension_semantics=sem, vmem_limit_bytes=VMEM_LIMIT)


def _mm_body(x_ref, w_ref, o_ref):
    o_ref[...] = _dot(x_ref[...], w_ref[0].astype(bf16))


def _matmul(x, w, layer, col0, n, tn=512):
    m, k = x.shape
    tm = min(m, 2048)
    assert col0 % LANES == 0 and n % tn == 0 and m % tm == 0
    if col0 % tn == 0:
        wspec = pl.BlockSpec((1, k, tn), lambda i, j: (layer, 0, col0 // tn + j))
    else:
        wspec = pl.BlockSpec((pl.Element(1), pl.Element(k), pl.Element(tn)),
                             lambda i, j: (layer, 0, pl.multiple_of(col0 + j * tn, LANES)))
    return pl.pallas_call(
        _mm_body,
        grid=(m // tm, n // tn),
        in_specs=[pl.BlockSpec((tm, k), lambda i, j: (i, 0)), wspec],
        out_specs=pl.BlockSpec((tm, tn), lambda i, j: (i, j)),
        out_shape=jax.ShapeDtypeStruct((m, n), f32),
        compiler_params=_params("parallel", "arbitrary"),
        name="proj_matmul",
    )(x, w)


def _proj_ln_body(n_lhs, gated, *refs):
    lhs = [r[...] for r in refs[:n_lhs]]
    refs = refs[n_lhs:]
    y = lhs[0] if n_lhs == 1 else jnp.concatenate(lhs, axis=1)
    if gated:
        z = refs[0][...]
        refs = refs[1:]
        y = y * (z * jax.nn.sigmoid(z))
    w_ref, res_ref, g_ref, b_ref, o_ref, ob_ref = refs
    u = ALPHA * res_ref[...] + _dot(y.astype(bf16), w_ref[...])
    mu = jnp.mean(u, axis=-1, keepdims=True)
    d = u - mu
    var = jnp.mean(d * d, axis=-1, keepdims=True)
    out = d * lax.rsqrt(var + LN_EPS) * g_ref[...] + b_ref[...]
    o_ref[...] = out
    ob_ref[...] = out.astype(bf16)


def _proj_ln(lhs, z, w, layer, res, g, b):
    m = res.shape[0]
    tm = min(m, 256)
    row = lambda width: pl.BlockSpec((tm, width), lambda i: (i, 0))
    full = lambda a: pl.BlockSpec(a.shape, lambda i: (0, 0))
    args = list(lhs) + ([z] if z is not None else []) + [w, res, g, b]
    specs = [row(a.shape[1]) for a in lhs] + ([row(z.shape[1])] if z is not None else []) + [
        pl.BlockSpec((None,) + w.shape[1:], lambda i: (layer, 0, 0)), row(D_MODEL), full(g), full(b)]
    return pl.pallas_call(
        functools.partial(_proj_ln_body, len(lhs), z is not None),
        grid=(m // tm,),
        in_specs=specs,
        out_specs=[row(D_MODEL), row(D_MODEL)],
        out_shape=[jax.ShapeDtypeStruct((m, D_MODEL), f32), jax.ShapeDtypeStruct((m, D_MODEL), bf16)],
        compiler_params=_params("parallel"),
        name="proj_ln",
    )(*args)


def _t5_tiles_body(t5_ref, o_ref):
    h = pl.program_id(0)
    key = lax.broadcasted_iota(i32, (MOBA_BLOCK, MOBA_BLOCK), 0)
    qry = lax.broadcasted_iota(i32, (MOBA_BLOCK, MOBA_BLOCK), 1)
    tab = lambda b: t5_ref[h, b]
    rel = qry - key
    o_ref[0, 0] = jnp.where(rel >= 0, _t5_bias_of(jnp.maximum(rel, 0), tab), NEG)
    o_ref[0, 1] = _t5_bias_of(rel + MOBA_BLOCK, tab)


def _t5_tiles(t5_t):
    return pl.pallas_call(
        _t5_tiles_body,
        grid=(A_HEADS,),
        in_specs=[pl.BlockSpec(memory_space=pltpu.SMEM)],
        out_specs=pl.BlockSpec((1, 2, MOBA_BLOCK, MOBA_BLOCK), lambda h: (h, 0, 0, 0)),
        out_shape=jax.ShapeDtypeStruct((A_HEADS, 2, MOBA_BLOCK, MOBA_BLOCK), f32),
        compiler_params=_params("parallel"),
        name="t5_tiles",
    )(t5_t)


def _moba_prompt_body(nb, t5_ref, q_ref, k_ref, v_ref, bias_ref, o_ref):
    blkw = MOBA_BLOCK
    far_bias = t5_ref[pl.program_id(1), T5_BUCKETS - 1]
    k = k_ref[...]
    q = q_ref[...]
    kb = k.astype(bf16)
    vt = v_ref[...].T.astype(bf16)
    km = jnp.concatenate([jnp.mean(k[n * blkw:(n + 1) * blkw], axis=0, keepdims=True) for n in range(nb)], axis=0)
    gate_all = _dot3(km, q, NT)
    qs = (q * (1.0 / math.sqrt(A_HEAD_DIM))).astype(bf16)
    blk = lax.broadcasted_iota(i32, (nb, blkw), 0)

    def scores(qi):
        rows = slice(qi * blkw, (qi + 1) * blkw)
        gate = gate_all[:, rows]
        beaten = jnp.zeros(gate.shape, f32)
        for m in range(qi):
            gm = gate[m:m + 1, :]
            beaten = beaten + jnp.where((gm > gate) | ((gm == gate) & (m < blk)), 1.0, 0.0)
        sel = jnp.where((blk < qi) & (beaten < MOBA_TOPK), 0.0, NEG)
        s_all = _dot(kb[:(qi + 1) * blkw], qs[rows], NT)
        tiles = []
        for n in range(qi + 1):
            s = s_all[n * blkw:(n + 1) * blkw]
            if n == qi:
                s = s + bias_ref[0, 0]
            elif n == qi - 1:
                s = s + sel[n:n + 1, :] + bias_ref[0, 1]
            else:
                s = s + (sel[n:n + 1, :] + far_bias)
            tiles.append(s)
        return tiles

    def attend(qi, tiles):
        mx = functools.reduce(jnp.maximum, [jnp.max(s, axis=0, keepdims=True) for s in tiles])
        ps = [jnp.exp(s - mx) for s in tiles]
        den = functools.reduce(jnp.add, [jnp.sum(p, axis=0, keepdims=True) for p in ps])
        pcat = jnp.concatenate([p.astype(bf16) for p in ps], axis=0)
        acc = _dot(vt[:, :(qi + 1) * blkw], pcat)
        o_ref[qi * blkw:(qi + 1) * blkw, :] = (acc / den).T

    pending = scores(0)
    for qi in range(nb):
        ahead = scores(qi + 1) if qi + 1 < nb else None
        attend(qi, pending)
        pending = ahead


def _moba_prompt(q, k, v, bias_tiles, t5_t, batch, seq):
    assert 2 * MOBA_BLOCK - (MOBA_BLOCK - 1) >= T5_THR[-1]
    nb = seq // MOBA_BLOCK
    spec = pl.BlockSpec((seq, A_HEAD_DIM), lambda b, h: (b, h))
    return pl.pallas_call(
        functools.partial(_moba_prompt_body, nb),
        grid=(batch, A_HEADS),
        in_specs=[pl.BlockSpec(memory_space=pltpu.SMEM), spec, spec, spec,
                  pl.BlockSpec((1, 2, MOBA_BLOCK, MOBA_BLOCK), lambda b, h: (h, 0, 0, 0))],
        out_specs=spec,
        out_shape=jax.ShapeDtypeStruct((batch * seq, A_W), f32),
        compiler_params=_params("parallel", "parallel"),
        name="moba_prompt",
    )(t5_t, q, k, v, bias_tiles)


PAGES_PER_SEQ = PAST_LEN // PAGE_SIZE
PAST_BLOCKS = PAST_LEN // MOBA_BLOCK
PAGES_PER_BLOCK = MOBA_BLOCK // PAGE_SIZE
KMEAN_BLOCKS_PER_STEP = SUBLANES


def _kmean_body(pt_ref, *refs):
    pages, o_ref = refs[:-1], refs[-1]
    for jj in range(KMEAN_BLOCKS_PER_STEP):
        s = functools.reduce(jnp.add, [jnp.sum(pages[PAGES_PER_BLOCK * jj + i][0, 0], axis=0)
                                       for i in range(PAGES_PER_BLOCK)])
        s = s * (1.0 / MOBA_BLOCK)
        for h in range(A_HEADS):
            o_ref[0, h, jj:jj + 1, :] = s[h:h + 1, :]


def _moba_kmean(pt, cache_k, layer, batch):
    per_step = PAGES_PER_BLOCK * KMEAN_BLOCKS_PER_STEP
    page = lambda i: pl.BlockSpec((1, 1, PAGE_SIZE, A_HEADS, A_HEAD_DIM),
                                  lambda b, g, pt: (layer, pt[b * PAGES_PER_SEQ + g * per_step + i], 0, 0, 0))
    return pl.pallas_call(
        _kmean_body,
        grid_spec=pltpu.PrefetchScalarGridSpec(
            num_scalar_prefetch=1, grid=(batch, PAST_BLOCKS // KMEAN_BLOCKS_PER_STEP),
            in_specs=[page(i) for i in range(per_step)],
            out_specs=pl.BlockSpec((1, A_HEADS, KMEAN_BLOCKS_PER_STEP, A_HEAD_DIM), lambda b, g, pt: (b, 0, g, 0))),
        out_shape=jax.ShapeDtypeStruct((batch, A_HEADS, PAST_BLOCKS, A_HEAD_DIM), f32),
        compiler_params=_params("parallel", "arbitrary"),
        name="moba_kmean",
    )(pt, *([cache_k] * per_step))


def _moba_topk_body(q_ref, km_ref, o_ref):
    lane = lax.broadcasted_iota(i32, (SUBLANES, LANES), 1)
    for h in range(A_HEADS):
        cols = slice(h * A_HEAD_DIM, (h + 1) * A_HEAD_DIM)
        gate = _dot3(q_ref[0, :, cols], km_ref[0, h], NT)
        blk = lax.broadcasted_iota(i32, gate.shape, 1)
        out = jnp.zeros((SUBLANES, LANES), i32)
        for j in range(MOBA_TOPK):
            best = jnp.max(gate, axis=1, keepdims=True)
            idx = jnp.min(jnp.where(gate == best, blk, PAST_BLOCKS), axis=1, keepdims=True)
            out = jnp.where(lane == j, idx, out)
            gate = jnp.where(blk == idx, -jnp.inf, gate)
        o_ref[0, h * SUBLANES:(h + 1) * SUBLANES, :] = out


def _moba_topk(q8, kmean, batch):
    return pl.pallas_call(
        _moba_topk_body,
        grid=(batch,),
        in_specs=[pl.BlockSpec((1, SUBLANES, A_W), lambda b: (b, 0, 0)),
                  pl.BlockSpec((1, A_HEADS, PAST_BLOCKS, A_HEAD_DIM), lambda b: (b, 0, 0, 0))],
        out_specs=pl.BlockSpec((1, A_HEADS * SUBLANES, LANES), lambda b: (b, 0, 0)),
        out_shape=jax.ShapeDtypeStruct((batch, A_HEADS * SUBLANES, LANES), i32),
        compiler_params=_params("parallel"),
        name="moba_topk",
    )(q8, kmean)


def _moba_sample_body(t_new, layer, pt_ref, top_ref, t5_ref, q_ref, kn_ref, vn_ref, kc_hbm, vc_hbm, o_ref,
                      kbuf, vbuf, sem):
    b, h = pl.program_id(0), pl.program_id(1)
    nh = pl.num_programs(1)
    step = b * nh + h
    nstep = pl.num_programs(0) * nh
    per_q = MOBA_TOPK * PAGES_PER_BLOCK
    top_at = lambda bb, hh, qq, j: top_ref[((bb * A_HEADS + hh) * t_new + qq) * MOBA_TOPK + j]

    def copies(bb, hh, slot):
        out = []
        for qq in range(t_new):
            for j in range(MOBA_TOPK):
                blk = top_at(bb, hh, qq, j)
                for half in range(PAGES_PER_BLOCK):
                    pg = pt_ref[bb * PAGES_PER_SEQ + PAGES_PER_BLOCK * blk + half]
                    i = qq * per_q + j * PAGES_PER_BLOCK + half
                    out.append(pltpu.make_async_copy(kc_hbm.at[layer, pg, :, hh, :], kbuf.at[slot, i], sem.at[slot, 0]))
                    out.append(pltpu.make_async_copy(vc_hbm.at[layer, pg, :, hh, :], vbuf.at[slot, i], sem.at[slot, 1]))
        return out

    slot = step % 2

    @pl.when(step == 0)
    def _():
        for c in copies(b, h, slot):
            c.start()

    @pl.when(step + 1 < nstep)
    def _():
        nxt = step + 1
        for c in copies(nxt // nh, nxt % nh, 1 - slot):
            c.start()

    tab = lambda bkt: t5_ref[h, bkt]
    q = q_ref[0] * (1.0 / math.sqrt(A_HEAD_DIM))
    qb = q.astype(bf16)
    row = lax.broadcasted_iota(i32, (SUBLANES, 1), 0)
    kn, vn = kn_ref[0], vn_ref[0]
    own = []
    for t in range(t_new):
        sc = jnp.sum(q * kn[t:t + 1, :], axis=1, keepdims=True) + _t5_bias_of(jnp.maximum(row - t, 0), tab)
        own.append(jnp.where(row >= t, sc, NEG))
    own_max = functools.reduce(jnp.maximum, own)

    for c in copies(b, h, slot):
        c.wait()

    lane = lax.broadcasted_iota(i32, (1, per_q * PAGE_SIZE), 1)
    l_sum = jnp.zeros((SUBLANES, 1), f32)
    acc = jnp.zeros((SUBLANES, A_HEAD_DIM), f32)
    for qq in range(t_new):
        kq = kbuf[slot, qq * per_q:(qq + 1) * per_q].reshape(per_q * PAGE_SIZE, A_HEAD_DIM)
        vq = vbuf[slot, qq * per_q:(qq + 1) * per_q].reshape(per_q * PAGE_SIZE, A_HEAD_DIM)
        keypos = lane - (MOBA_TOPK - 1) * MOBA_BLOCK + top_at(b, h, qq, MOBA_TOPK - 1) * MOBA_BLOCK
        for j in range(MOBA_TOPK - 2, -1, -1):
            keypos = jnp.where(lane < (j + 1) * MOBA_BLOCK, lane - j * MOBA_BLOCK + top_at(b, h, qq, j) * MOBA_BLOCK,
                               keypos)
        lg = _dot(qb, kq.astype(bf16), NT) + _t5_bias_of(PAST_LEN + qq - keypos, tab)
        mine = row == qq
        m = jnp.maximum(own_max, jnp.max(lg, axis=1, keepdims=True))
        p = jnp.where(mine, jnp.exp(lg - m), 0.0)
        l_sum = l_sum + jnp.sum(p, axis=1, keepdims=True)
        acc = acc + _dot(p.astype(bf16), vq.astype(bf16))
        for t in range(t_new):
            po = jnp.where(mine, jnp.exp(own[t] - m), 0.0)
            l_sum = l_sum + po
            acc = acc + po * vn[t:t + 1, :]
    o_ref[0] = acc / jnp.where(row < t_new, l_sum, 1.0)


def _moba_sample(pt, top, t5_t, q8, kn8, vn8, cache_k, cache_v, layer, batch, t_new):
    npage = t_new * MOBA_TOPK * PAGES_PER_BLOCK
    rows = pl.BlockSpec((1, SUBLANES, A_HEAD_DIM), lambda b, h, pt, top: (b, 0, h))
    hbm = pl.BlockSpec(memory_space=pl.ANY)
    return pl.pallas_call(
        functools.partial(_moba_sample_body, t_new, layer),
        grid_spec=pltpu.PrefetchScalarGridSpec(
            num_scalar_prefetch=2, grid=(batch, A_HEADS),
            in_specs=[pl.BlockSpec(memory_space=pltpu.SMEM), rows, rows, rows, hbm, hbm],
            out_specs=rows,
            scratch_shapes=[pltpu.VMEM((2, npage, PAGE_SIZE, A_HEAD_DIM), f32),
                            pltpu.VMEM((2, npage, PAGE_SIZE, A_HEAD_DIM), f32),
                            pltpu.SemaphoreType.DMA((2, 2))]),
        out_shape=jax.ShapeDtypeStruct((batch, SUBLANES, A_W), f32),
        compiler_params=_params("arbitrary", "arbitrary"),
        name="moba_sample",
    )(pt, top, t5_t, q8, kn8, vn8, cache_k, cache_v)


def _rwkv_prep_body(t_valid, t_total, xb_ref, halo_ref, sp_ref, mu_ref, w0_ref, a0_ref, kk_ref, ka_ref, lora_ref,
                    r_ref, l_ref, a_ref, b_ref, k_ref, v_ref):
    i = pl.program_id(1)
    x = xb_ref[0]
    tt = x.shape[0]
    row = lax.broadcasted_iota(i32, (tt, 1), 0)
    first = jnp.where(i == 0, sp_ref[0], halo_ref[0, SUBLANES - 1:SUBLANES, :])
    prev = jnp.where(row == 0, first, pltpu.roll(x, 1, 0))
    xm = x + (prev - x) * mu_ref[...]
    r, k, v = xm[:, :B_W], xm[:, B_W:2 * B_W], xm[:, 2 * B_W:3 * B_W]
    wa = xm[:, 3 * B_W:]
    lane = lax.broadcasted_iota(i32, wa.shape, 1)
    lw = _dot3(jnp.where(lane < LORA_W, jnp.tanh(wa), 0.0), lora_ref[...])
    la = _dot3(jnp.where(lane < LORA_W, 0.0, wa), lora_ref[...])
    wlog = -math.exp(-0.5) * jax.nn.sigmoid(w0_ref[...] + lw)
    a = jax.nn.sigmoid(a0_ref[...] + la)
    kk = k * kk_ref[...]
    sq = kk * kk
    ss = jnp.concatenate(_head_reduce([sq[:, p * GROUP:(p + 1) * GROUP] for p in range(B_W // GROUP)],
                                      jnp.where(_same_head(), 1.0, 0.0).astype(bf16)), axis=1)
    kk = kk * lax.rsqrt(jnp.maximum(ss, 1e-24))
    kp = k * (1.0 + (a - 1.0) * ka_ref[...])
    outs = (r, wlog, -kk, kk * a, kp, v)
    if t_valid < t_total:
        ok = (i * tt + row) < t_valid
        outs = tuple(jnp.where(ok, o, 0.0) for o in outs)
    for ref, o in zip((r_ref, l_ref, a_ref, b_ref, k_ref, v_ref), outs):
        ref[0] = o


def _rwkv_prep(xb3, shift_prev, t_valid, mu, w0, a0, k_k, k_a, lora):
    bn, tp, _ = xb3.shape
    tt = min(tp, 256)
    halo_blocks = tt // SUBLANES
    tile = lambda width: pl.BlockSpec((1, tt, width), lambda b, i: (b, i, 0))
    vec = lambda a: pl.BlockSpec(a.shape, lambda b, i: (0,) * a.ndim)
    out = jax.ShapeDtypeStruct((bn, tp, B_W), f32)
    return pl.pallas_call(
        functools.partial(_rwkv_prep_body, t_valid, tp),
        grid=(bn, tp // tt),
        in_specs=[tile(B_COLS),
                  pl.BlockSpec((1, SUBLANES, B_COLS), lambda b, i: (b, jnp.maximum(i * halo_blocks - 1, 0), 0)),
                  pl.BlockSpec((1, 1, B_COLS), lambda b, i: (b, 0, 0)),
                  vec(mu), vec(w0), vec(a0), vec(k_k), vec(k_a), vec(lora)],
        out_specs=[tile(B_W)] * 6,
        out_shape=[out] * 6,
        compiler_params=_params("parallel", "arbitrary"),
        name="rwkv_prep",
    )(xb3, xb3, shift_prev, mu, w0, a0, k_k, k_a, lora)


def _head_of(idx):
    return idx // B_HEAD_DIM


def _same_head():
    return (_head_of(lax.broadcasted_iota(i32, (GROUP, GROUP), 0))
            == _head_of(lax.broadcasted_iota(i32, (GROUP, GROUP), 1)))


def _stack(p, head):
    return jnp.concatenate([jnp.where(head == h, p, 0.0) for h in range(GROUP_HEADS)], axis=0)


def _head_reduce(parts, mat):
    n, c = len(parts), parts[0].shape[0]
    halves = [_split(a) for a in parts]
    out = _dot(jnp.concatenate([h for h, _ in halves] + [l for _, l in halves], axis=0), mat)
    return [out[i * c:(i + 1) * c] + out[(n + i) * c:(n + i + 1) * c] for i in range(n)]


def _rwkv_scan_body(npair, r_ref, l_ref, a_ref, b_ref, k_ref, v_ref, s0_ref, rk_ref, gg_ref, gb_ref, y_ref, s_ref, bd_ref):
    hd = B_HEAD_DIM
    heads = range(GROUP_HEADS)
    gpb = B_HEADS // GROUP_HEADS
    nb = npair // gpb

    @pl.when(pl.program_id(1) == 0)
    def _():
        zero = jnp.zeros((hd, hd), f32)
        for p in range(npair):
            bb, gi = divmod(p, gpb)
            bd_ref[p] = jnp.concatenate(
                [jnp.concatenate([s0_ref[bb, GROUP_HEADS * gi + h].T if h == g else zero for g in heads], axis=1)
                 for h in heads], axis=0)

    c = RWKV_CHUNK
    pairs = range(npair)
    col = lambda x, p: x[:, p * GROUP:(p + 1) * GROUP]
    lane = lax.broadcasted_iota(i32, (c, GROUP), 1)
    t = lax.broadcasted_iota(i32, (c, GROUP), 0)
    j = lane & (B_HEAD_DIM - 1)
    strict, incl = t > j, t >= j
    eye_c = jnp.where(t == j, 1.0, 0.0)
    rowi = lax.broadcasted_iota(i32, (GROUP, GROUP), 0)
    coli = lax.broadcasted_iota(i32, (GROUP, GROUP), 1)
    same_head = _same_head()
    head_sum = jnp.where(same_head, 1.0, 0.0).astype(bf16)
    head_mean = jnp.where(same_head, 1.0 / B_HEAD_DIM, 0.0).astype(bf16)
    head_lane = _head_of(lane)
    st = lambda x: _stack(x, head_lane).astype(bf16)
    cast = lambda x: x.astype(bf16)

    R, L, A, B, K, V = (jnp.concatenate([ref[bb] for bb in range(nb)], axis=1)
                        for ref in (r_ref, l_ref, a_ref, b_ref, k_ref, v_ref))
    rk, gg, gb = (jnp.concatenate([ref[...]] * nb, axis=1) for ref in (rk_ref, gg_ref, gb_ref))
    tri = jnp.where(lax.broadcasted_iota(i32, (c, c), 0) >= lax.broadcasted_iota(i32, (c, c), 1), 1.0, 0.0).astype(bf16)
    l1, l2 = _split(L)
    l3 = (L - l1.astype(f32) - l2.astype(f32)).astype(bf16)
    cum = _dot(tri, l1) + _dot(tri, l2) + _dot(tri, l3)
    last = cum[c - 1:c, :]
    wc = jnp.exp(last)
    At, Rt = A * jnp.exp(cum - L), R * jnp.exp(cum)
    einv, ew = jnp.exp(-cum), jnp.exp(last - cum)
    Bt, Kt, Bw, Kw = B * einv, K * einv, B * ew, K * ew

    ar = [cast(jnp.concatenate([col(At, p), col(Rt, p)], axis=0)) for p in pairs]
    pb = [_dot(ar[p], st(col(Bt, p)), NT) for p in pairs]
    pk = [_dot(ar[p], st(col(Kt, p)), NT) for p in pairs]
    pab = [jnp.where(strict, x[:c], 0.0) for x in pb]
    prb = [jnp.where(incl, x[c:], 0.0) for x in pb]
    pak = [jnp.where(strict, x[:c], 0.0) for x in pk]
    prk = [jnp.where(incl, x[c:], 0.0) for x in pk]
    x = [eye_c + m for m in pab]
    qp = pab
    for _ in range(int(math.log2(c)) - 1):
        qp = [_dot(cast(m), st(m)) for m in qp]
        x = [xi + _dot(cast(xi), st(m)) for xi, m in zip(x, qp)]
    akv = [_dot(cast(pak[p]), st(col(V, p))) for p in pairs]
    ua = [_dot(cast(x[p]), jnp.concatenate([st(akv[p]), st(col(At, p))], axis=1)) for p in pairs]
    u0 = [m[:, :GROUP] for m in ua]
    ah = [m[:, GROUP:] for m in ua]
    ry = [_dot(cast(prb[p]), jnp.concatenate([st(ah[p]), st(u0[p])], axis=1)) for p in pairs]
    rh = [col(Rt, p) + ry[p][:, :GROUP] for p in pairs]
    y0 = [ry[p][:, GROUP:] + _dot(cast(prk[p]), st(col(V, p))) for p in pairs]
    z1 = [_dot(cast(col(Bw, p)), cast(jnp.concatenate([ah[p], u0[p]], axis=1)), TN) for p in pairs]
    z2 = [_dot(cast(col(Kw, p)), cast(col(V, p)), TN) for p in pairs]
    mt = [jnp.where(rowi == coli, col(wc, p), 0.0) + jnp.where(same_head, z1[p][:, :GROUP], 0.0) for p in pairs]
    gt = [jnp.where(same_head, z1[p][:, GROUP:] + z2[p], 0.0) for p in pairs]
    nl = c + GROUP
    lhs = [_split(jnp.concatenate([rh[p], mt[p]], axis=0)) for p in pairs]
    bd = [_split(bd_ref[p]) for p in pairs]
    main = [_dot(jnp.concatenate(lhs[p], axis=0), bd[p][0]) for p in pairs]
    corr = [_dot(lhs[p][0], bd[p][1]) for p in pairs]
    prod = [main[p][:nl] + main[p][nl:] + corr[p] for p in pairs]
    new_bd = [prod[p][c:] + gt[p] for p in pairs]
    for p in pairs:
        bd_ref[p] = new_bd[p]

    @pl.when(pl.program_id(1) == pl.num_programs(1) - 1)
    def _():
        for p in pairs:
            for h in heads:
                s_ref[p // gpb, GROUP_HEADS * (p % gpb) + h] = new_bd[p][h * hd:(h + 1) * hd, h * hd:(h + 1) * hd].T

    y = [prod[p][:c] + y0[p] for p in pairs]
    mean = _head_reduce(y, head_mean)
    d = [yi - m for yi, m in zip(y, mean)]
    var = _head_reduce([di * di for di in d], head_mean)
    bonus = _head_reduce([col(R, p) * col(K, p) * col(rk, p) for p in pairs], head_sum)
    for p in pairs:
        yn = d[p] * lax.rsqrt(var[p] + RWKV_GN_EPS) * col(gg, p) + col(gb, p)
        y_ref[p // gpb, :, (p % gpb) * GROUP:(p % gpb + 1) * GROUP] = yn + bonus[p] * col(V, p)


SCAN_BATCHES = 4


def _rwkv_scan(prep, s0, r_k, gn_g, gn_b):
    bn, tp, _ = prep[0].shape
    nb = SCAN_BATCHES
    assert bn % nb == 0
    npair = nb * B_HEADS // GROUP_HEADS
    tile = pl.BlockSpec((nb, RWKV_CHUNK, B_W), lambda b, c: (b, c, 0))
    state = pl.BlockSpec((nb, B_HEADS, B_HEAD_DIM, B_HEAD_DIM), lambda b, c: (b, 0, 0, 0))
    vec = pl.BlockSpec((1, B_W), lambda b, c: (0, 0))
    return pl.pallas_call(
        functools.partial(_rwkv_scan_body, npair),
        grid=(bn // nb, tp // RWKV_CHUNK),
        in_specs=[tile] * 6 + [state, vec, vec, vec],
        out_specs=[tile, state],
        out_shape=[jax.ShapeDtypeStruct((bn, tp, B_W), f32), jax.ShapeDtypeStruct(s0.shape, f32)],
        scratch_shapes=[pltpu.VMEM((npair, GROUP, GROUP), f32)],
        compiler_params=_params("parallel", "arbitrary"),
        name="rwkv_scan",
    )(*prep, s0, r_k, gn_g, gn_b)


def _pool_body(pos0, x_ref, halo_ref, buf_ref, pw_ref, sc_ref, o_ref):
    i = pl.program_id(1)
    x = x_ref[0]
    tt = x.shape[0]
    halo = jnp.where(i == 0, buf_ref[0], halo_ref[0])
    xe = jnp.concatenate([halo, x], axis=0)
    row = lax.broadcasted_iota(i32, (tt, 1), 0)
    pos = pos0 + i * tt + row
    outs = []
    for g, w in enumerate(POOL_WINDOWS):
        cols = slice(g * POOL_GW, (g + 1) * POOL_GW)
        s = xe[:, cols]
        sh = 1
        while sh < w:
            s = s + pltpu.roll(s, sh, 0)
            sh *= 2
        cnt = jnp.minimum(w, pos + 1).astype(f32)
        pooled = s[POOL_BUF + 1:] / cnt - x[:, cols]
        outs.append(_dot(pooled.astype(bf16), pw_ref[g]))
    o_ref[0] = jnp.concatenate(outs, axis=1) * sc_ref[...]


def _pool(xc3, buf16, pos0, pool_w, layer, scale):
    bn, tp, _ = xc3.shape
    tt = min(tp, 256)
    hb = POOL_BUF + 1
    return pl.pallas_call(
        functools.partial(_pool_body, pos0),
        grid=(bn, tp // tt),
        in_specs=[pl.BlockSpec((1, tt, C_W), lambda b, i: (b, i, 0)),
                  pl.BlockSpec((1, hb, C_W), lambda b, i: (b, jnp.maximum(i * (tt // hb) - 1, 0), 0)),
                  pl.BlockSpec((1, hb, C_W), lambda b, i: (b, 0, 0)),
                  pl.BlockSpec((None,) + pool_w.shape[1:], lambda b, i: (layer, 0, 0, 0)),
                  pl.BlockSpec(scale.shape, lambda b, i: (0, 0))],
        out_specs=pl.BlockSpec((1, tt, C_W), lambda b, i: (b, i, 0)),
        out_shape=jax.ShapeDtypeStruct((bn, tp, C_W), f32),
        compiler_params=_params("parallel", "arbitrary"),
        name="pool_mix",
    )(xc3, xc3, buf16, pool_w, scale)


MEM_PROJ_HEADS = 2


def _mem_proj_body(layer, batch, x_ref, w_ref, prev_hbm, o_hbm, ybuf, sem):
    del prev_hbm
    y = _dot(x_ref[...], w_ref[0].astype(bf16))
    for hh in range(MEM_PROJ_HEADS):
        ybuf[hh] = y[:, hh * X_HEAD_DIM:(hh + 1) * X_HEAD_DIM]
    head0 = pl.program_id(0) * MEM_PROJ_HEADS
    copies = [pltpu.make_async_copy(ybuf.at[hh, b * MEM_LEN:(b + 1) * MEM_LEN], o_hbm.at[layer, b, :, head0 + hh, :],
                                    sem.at[hh, b])
              for hh in range(MEM_PROJ_HEADS) for b in range(batch)]
    for c in copies:
        c.start()
    for c in copies:
        c.wait()


def _mem_proj(x, w, layer, prev, batch):
    tn = MEM_PROJ_HEADS * X_HEAD_DIM
    return pl.pallas_call(
        functools.partial(_mem_proj_body, layer, batch),
        grid=(X_HEADS // MEM_PROJ_HEADS,),
        in_specs=[pl.BlockSpec(x.shape, lambda j: (0, 0)),
                  pl.BlockSpec((1, w.shape[1], tn), lambda j: (layer, 0, j)),
                  pl.BlockSpec(memory_space=pl.ANY)],
        out_specs=pl.BlockSpec(memory_space=pl.ANY),
        out_shape=jax.ShapeDtypeStruct(prev.shape, f32),
        scratch_shapes=[pltpu.VMEM((MEM_PROJ_HEADS, x.shape[0], X_HEAD_DIM), f32),
                        pltpu.SemaphoreType.DMA((MEM_PROJ_HEADS, batch))],
        input_output_aliases={2: 0},
        compiler_params=_params("arbitrary"),
        name="mem_proj",
    )(x, w, prev)


XATTN_SUB_ROWS = 256


def _xattn_block_body(layer, nbt, x_ref, res_ref, wq_hbm, wo_hbm, k_hbm, v_hbm, g_ref, b_ref, o_ref, ob_ref,
                      wq, wo, kstage, vstage, kb, vb, wsem, sem):
    bg, i = pl.program_id(0), pl.program_id(1)

    @pl.when((bg == 0) & (i == 0))
    def _():
        copies = [pltpu.make_async_copy(wq_hbm.at[layer], wq, wsem.at[0]),
                  pltpu.make_async_copy(wo_hbm.at[layer], wo, wsem.at[1])]
        for c in copies:
            c.start()
        for c in copies:
            c.wait()

    @pl.when(i == 0)
    def _():
        for j in range(nbt):
            copies = [pltpu.make_async_copy(src.at[layer, bg * nbt + j, :, h, :], dst.at[h], sem.at[n, h])
                      for n, (src, dst) in enumerate(((k_hbm, kstage), (v_hbm, vstage))) for h in range(X_HEADS)]
            for c in copies:
                c.start()
            for c in copies:
                c.wait()
            kb[j] = kstage[...].astype(bf16)
            vb[j] = vstage[...].astype(bf16)

    q = _dot(x_ref[...], wq[...])
    rows = q.shape[0] // nbt
    per_batch = []
    for j in range(nbt):
        heads = []
        for h in range(X_HEADS):
            qh = q[j * rows:(j + 1) * rows, h * X_HEAD_DIM:(h + 1) * X_HEAD_DIM]
            s = _dot((qh * (1.0 / math.sqrt(X_HEAD_DIM))).astype(bf16), kb[j, h], NT)
            p = jnp.exp(s - jnp.max(s, axis=-1, keepdims=True))
            heads.append((_dot(p.astype(bf16), vb[j, h]) / jnp.sum(p, axis=-1, keepdims=True)).astype(bf16))
        per_batch.append(jnp.concatenate(heads, axis=1))
    att = per_batch[0] if nbt == 1 else jnp.concatenate(per_batch, axis=0)
    u = ALPHA * res_ref[...] + _dot(att, wo[...])
    mu = jnp.mean(u, axis=-1, keepdims=True)
    d = u - mu
    var = jnp.mean(d * d, axis=-1, keepdims=True)
    out = d * lax.rsqrt(var + LN_EPS) * g_ref[...] + b_ref[...]
    o_ref[...] = out
    ob_ref[...] = out.astype(bf16)


def _xattn_block(xb16, res, w_q, w_o, mk, mv, layer, g, b, batch, tq):
    m = res.shape[0]
    nt = m // batch // tq
    kv_bytes = 2 * X_HEADS * MEM_LEN * X_HEAD_DIM * 2
    nbt = max(n for n in range(1, batch + 1) if batch % n == 0 and n * kv_bytes <= VMEM_LIMIT // 4) if nt == 1 else 1
    row = pl.BlockSpec((nbt * tq, D_MODEL), lambda bg, i: (bg * nt + i, 0))
    vec = pl.BlockSpec((1, D_MODEL), lambda bg, i: (0, 0))
    hbm = pl.BlockSpec(memory_space=pl.ANY)
    mem_tile = (X_HEADS, MEM_LEN, X_HEAD_DIM)
    return pl.pallas_call(
        functools.partial(_xattn_block_body, layer, nbt),
        grid=(batch // nbt, nt),
        in_specs=[row, row, hbm, hbm, hbm, hbm, vec, vec],
        out_specs=[row, row],
        out_shape=[jax.ShapeDtypeStruct((m, D_MODEL), f32), jax.ShapeDtypeStruct((m, D_MODEL), bf16)],
        scratch_shapes=[pltpu.VMEM((D_MODEL, D_MODEL), bf16), pltpu.VMEM((D_MODEL, D_MODEL), bf16),
                        pltpu.VMEM(mem_tile, f32), pltpu.VMEM(mem_tile, f32),
                        pltpu.VMEM((nbt,) + mem_tile, bf16), pltpu.VMEM((nbt,) + mem_tile, bf16),
                        pltpu.SemaphoreType.DMA((2,)), pltpu.SemaphoreType.DMA((2, X_HEADS))],
        compiler_params=_params("arbitrary", "arbitrary"),
        name="xattn_block",
    )(xb16, res, w_q, w_o, mk, mv, g, b)


def _pad_rows(x3, rows):
    return jnp.pad(x3, ((0, 0), (0, rows - x3.shape[1]), (0, 0)))


def kernel(x_prompt, x_sample, cache_moba_k, cache_moba_v, page_table, state_rwkv, state_shift, state_pool, cache_mem_k, cache_mem_v, mem_prompt, w_in_even, w_out_even, rwkv_mu, rwkv_w0, rwkv_w_up, rwkv_a0, rwkv_a_up, rwkv_k_k, rwkv_k_a, rwkv_r_k, rwkv_gn_g, rwkv_gn_b, t5_bias, w_in_odd, pool_w, pool_scale, w_out_odd, xattn_w_q, xattn_w_k, xattn_w_v, xattn_w_o, ln_mix_g, ln_mix_b, ln_x_g, ln_x_b):
    bp, tp, _ = x_prompt.shape
    bs, ts, _ = x_sample.shape
    xp = x_prompt.reshape(bp * tp, D_MODEL)
    xs = x_sample.reshape(bs * ts, D_MODEL)
    mem = mem_prompt.reshape(bp * MEM_LEN, D_MODEL)
    pt = page_table.reshape(-1)
    t5_t = t5_bias.T
    bias_tiles = _t5_tiles(t5_t)
    row2 = lambda a: a.reshape(1, -1)
    cast = lambda a: a.astype(bf16)
    col_q, col_k, col_v, col_xb, col_z = 0, A_W, 2 * A_W, 3 * A_W, 3 * A_W + B_COLS
    w_even, w_odd, xw_k, xw_v = w_in_even, w_in_odd, xattn_w_k, xattn_w_v
    w_oe, w_oo, pw, xw_q, xw_o = cast(w_out_even), cast(w_out_odd), cast(pool_w), cast(xattn_w_q), cast(xattn_w_o)
    xpb, xsb, memb = cast(xp), cast(xs), cast(mem)

    kp_l, vp_l, sp_l, shp_l, poolp_l = [], [], [], [], []
    mem_k = mem_v = jnp.zeros((DEPTH, bp, MEM_LEN, X_HEADS, X_HEAD_DIM), f32)
    ks_l, vs_l, ss_l, shs_l, pools_l = [], [], [], [], []
    for l in range(DEPTH):
        if l % 2 == 0:
            e = l // 2
            in_proj = lambda x: (_matmul(x, w_even, e, col_q, A_W), _matmul(x, w_even, e, col_k, A_W),
                                 _matmul(x, w_even, e, col_v, A_W), _matmul(x, w_even, e, col_xb, B_COLS, tn=640),
                                 _matmul(x, w_even, e, col_z, D_MODEL))
            lora = jnp.concatenate([rwkv_w_up[e], rwkv_a_up[e]], axis=0)
            rw = (row2(rwkv_mu[e]), row2(rwkv_w0[e]), row2(rwkv_a0[e]), row2(rwkv_k_k[e]), row2(rwkv_k_a[e]),
                  lora)
            gn = (row2(rwkv_r_k[e]), row2(rwkv_gn_g[e]), row2(rwkv_gn_b[e]))

            def rwkv(xb, batch, t_real, t_pad, shift_prev, s0):
                xb3 = xb.reshape(batch, t_real, B_COLS)
                if t_pad != t_real:
                    xb3 = _pad_rows(xb3, t_pad)
                prep = _rwkv_prep(xb3, shift_prev, t_real, *rw)
                y, s_new = _rwkv_scan(prep, s0, *gn)
                return y[:, :t_real].reshape(batch * t_real, B_W), xb3[:, t_real - 1], s_new

            q, k, v, xb, z = in_proj(xpb)
            a_out = _moba_prompt(q, k, v, bias_tiles, t5_t, bp, tp)
            b_out, shp, s_p = rwkv(xb, bp, tp, tp, jnp.zeros((bp, 1, B_COLS), f32),
                                   jnp.zeros((bp, B_HEADS, B_HEAD_DIM, B_HEAD_DIM), f32))
            mp_args = ([a_out, b_out], z, w_oe, e)
            kp_l.append(k.reshape(bp, tp, A_HEADS, A_HEAD_DIM))
            vp_l.append(v.reshape(bp, tp, A_HEADS, A_HEAD_DIM))
            sp_l.append(s_p)
            shp_l.append(shp)

            q, k, v, xb, z = in_proj(xsb)
            pad8 = lambda a: _pad_rows(a.reshape(bs, ts, A_W), SUBLANES)
            q8, k8, v8 = pad8(q), pad8(k), pad8(v)
            top = _moba_topk(q8, _moba_kmean(pt, cache_moba_k, e, bs), bs)
            top = top.reshape(bs, A_HEADS, SUBLANES, LANES)[:, :, :ts, :MOBA_TOPK].reshape(-1)
            a_out = _moba_sample(pt, top, t5_t, q8, k8, v8, cache_moba_k, cache_moba_v, e, bs, ts)
            a_out = a_out[:, :ts].reshape(bs * ts, A_W)
            b_out, shs, s_s = rwkv(xb, bs, ts, RWKV_CHUNK, state_shift[e].reshape(bs, 1, B_COLS), state_rwkv[e])
            ms_args = ([a_out, b_out], z, w_oe, e)
            ks_l.append(k.reshape(bs, ts, A_HEADS, A_HEAD_DIM))
            vs_l.append(v.reshape(bs, ts, A_HEADS, A_HEAD_DIM))
            ss_l.append(s_s)
            shs_l.append(shs)
        else:
            o = l // 2
            sc = row2(pool_scale[o])
            in_proj = lambda x: (_matmul(x, w_odd, o, 0, C_W), _matmul(x, w_odd, o, C_W, D_MODEL))

            xc, z = in_proj(xpb)
            xc3 = xc.reshape(bp, tp, C_W)
            y = _pool(xc3, jnp.zeros((bp, POOL_BUF + 1, C_W), f32), 0, pw, o, sc)
            mp_args = ([y.reshape(bp * tp, C_W)], z, w_oo, o)
            poolp_l.append(xc3[:, tp - POOL_BUF:])

            xc, z = in_proj(xsb)
            xc3 = xc.reshape(bs, ts, C_W)
            buf16 = jnp.pad(state_pool[o], ((0, 0), (1, 0), (0, 0)))
            y = _pool(_pad_rows(xc3, POOL_BUF + 1), buf16, PAST_LEN, pw, o, sc)[:, :ts]
            ms_args = ([y.reshape(bs * ts, C_W)], z, w_oo, o)
            pools_l.append(jnp.concatenate([state_pool[o], xc3], axis=1)[:, -POOL_BUF:])

        g, b = row2(ln_mix_g[l]), row2(ln_mix_b[l])
        xp, xpb = _proj_ln(*mp_args, xp, g, b)
        xs, xsb = _proj_ln(*ms_args, xs, g, b)

        g, b = row2(ln_x_g[l]), row2(ln_x_b[l])
        mem_k, mem_v = _mem_proj(memb, xw_k, l, mem_k, bp), _mem_proj(memb, xw_v, l, mem_v, bp)
        xp, xpb = _xattn_block(xpb, xp, xw_q, xw_o, mem_k, mem_v, l, g, b, bp, XATTN_SUB_ROWS)
        pad = lambda a: _pad_rows(a.reshape(bs, ts, D_MODEL), BF16_ROWS).reshape(bs * BF16_ROWS, D_MODEL)
        unpad = lambda a: a.reshape(bs, BF16_ROWS, D_MODEL)[:, :ts].reshape(bs * ts, D_MODEL)
        xs, xsb = map(unpad, _xattn_block(pad(xsb), pad(xs), xw_q, xw_o, cache_mem_k, cache_mem_v, l, g, b,
                                          bs, BF16_ROWS))

    return (xp.reshape(bp, tp, D_MODEL), xs.reshape(bs, ts, D_MODEL),
            jnp.stack(kp_l), jnp.stack(vp_l), jnp.stack(sp_l), jnp.stack(shp_l), jnp.stack(poolp_l),
            mem_k, mem_v,
            jnp.stack(ks_l), jnp.stack(vs_l), jnp.stack(ss_l), jnp.stack(shs_l), jnp.stack(pools_l))
```

```python
import functools
import math

import numpy as np
import jax
import jax.numpy as jnp
from jax import lax
from jax.experimental import pallas as pl
from jax.experimental.pallas import tpu as pltpu

f32, bf16, i32 = jnp.float32, jnp.bfloat16, jnp.int32

D_MODEL = 2048
DEPTH = 4
PAST_LEN = 16384
PAGE_SIZE = 128
A_HEAD_DIM = 128
A_W = 1024
A_HEADS = 8
MOBA_BLOCK = 256
MOBA_TOPK = 3
B_HEAD_DIM = 64
B_W = 1024
B_HEADS = 16
LORA_W = 64
B_COLS = 3 * B_W + 2 * LORA_W
RWKV_GN_EPS = 64e-5
C_W = 2048
POOL_WINDOWS = (2, 4, 8, 16)
POOL_GW = C_W // len(POOL_WINDOWS)
POOL_BUF = max(POOL_WINDOWS) - 1
MEM_LEN = 256
X_HEADS = 4
X_HEAD_DIM = D_MODEL // X_HEADS
T5_BUCKETS = 32
T5_MAX_DIST = 128
LN_EPS = 1e-5
ALPHA = (2 * DEPTH) ** 0.25

NEG = -1e30
LANES = 128
SUBLANES = 8
BF16_ROWS = 2 * SUBLANES
RWKV_CHUNK = 64
GROUP_HEADS = LANES // B_HEAD_DIM
GROUP = LANES
VMEM_LIMIT = 48 * 1024 * 1024

NT = (((1,), (1,)), ((), ()))
TN = (((0,), (0,)), ((), ()))


def _t5_thresholds():
    exact = T5_BUCKETS // 2
    rel = np.arange(0, 4 * T5_MAX_DIST)
    relf = np.maximum(rel, exact).astype(np.float32)
    large = exact + (np.log(relf / np.float32(exact)) / np.float32(math.log(T5_MAX_DIST / exact))
                     * np.float32(T5_BUCKETS - exact)).astype(np.int32)
    bucket = np.where(rel < exact, rel, np.minimum(large, T5_BUCKETS - 1))
    assert (np.diff(bucket) >= 0).all() and bucket[-1] == T5_BUCKETS - 1
    return [int(np.argmax(bucket >= b)) for b in range(T5_BUCKETS)]


T5_THR = _t5_thresholds()


def _dot(a, b, dn=None):
    if dn is None:
        return jnp.dot(a, b, preferred_element_type=f32)
    return lax.dot_general(a, b, dn, preferred_element_type=f32)


def _split(x):
    hi = x.astype(bf16)
    lo = (x - hi.astype(f32)).astype(bf16)
    return hi, lo


def _dot3(a, b, dn=None):
    ah, al = _split(a)
    bh, bl = _split(b)
    return _dot(ah, bh, dn) + _dot(ah, bl, dn) + _dot(al, bh, dn)


def _dot2l(a, b_exact, dn=None):
    ah, al = _split(a)
    return _dot(ah, b_exact, dn) + _dot(al, b_exact, dn)


def _t5_bias_of(rel, tab):
    bias = jnp.full(rel.shape, tab(0), f32)
    for b in range(1, T5_BUCKETS):
        bias = jnp.where(rel >= T5_THR[b], tab(b), bias)
    return bias


def _params(*sem):
    return pltpu.CompilerParams(dimension_semantics=sem, vmem_limit_bytes=VMEM_LIMIT)


def _mm_body(x_ref, w_ref, o_ref):
    o_ref[...] = _dot(x_ref[...], w_ref[0].astype(bf16))


def _matmul(x, w, layer, col0, n, tn=512):
    m, k = x.shape
    tm = min(m, 2048)
    assert col0 % LANES == 0 and n % tn == 0 and m % tm == 0
    if col0 % tn == 0:
        wspec = pl.BlockSpec((1, k, tn), lambda i, j: (layer, 0, col0 // tn + j))
    else:
        wspec = pl.BlockSpec((pl.Element(1), pl.Element(k), pl.Element(tn)),
                             lambda i, j: (layer, 0, pl.multiple_of(col0 + j * tn, LANES)))
    return pl.pallas_call(
        _mm_body,
        grid=(m // tm, n // tn),
        in_specs=[pl.BlockSpec((tm, k), lambda i, j: (i, 0)), wspec],
        out_specs=pl.BlockSpec((tm, tn), lambda i, j: (i, j)),
        out_shape=jax.ShapeDtypeStruct((m, n), f32),
        compiler_params=_params("parallel", "arbitrary"),
        name="proj_matmul",
    )(x, w)


def _proj_ln_body(n_lhs, gated, *refs):
    lhs = [r[...] for r in refs[:n_lhs]]
    refs = refs[n_lhs:]
    y = lhs[0] if n_lhs == 1 else jnp.concatenate(lhs, axis=1)
    if gated:
        z = refs[0][...]
        refs = refs[1:]
        y = y * (z * jax.nn.sigmoid(z))
    w_ref, res_ref, g_ref, b_ref, o_ref, ob_ref = refs
    u = ALPHA * res_ref[...] + _dot(y.astype(bf16), w_ref[...])
    mu = jnp.mean(u, axis=-1, keepdims=True)
    d = u - mu
    var = jnp.mean(d * d, axis=-1, keepdims=True)
    out = d * lax.rsqrt(var + LN_EPS) * g_ref[...] + b_ref[...]
    o_ref[...] = out
    ob_ref[...] = out.astype(bf16)


def _proj_ln(lhs, z, w, layer, res, g, b):
    m = res.shape[0]
    tm = min(m, 256)
    row = lambda width: pl.BlockSpec((tm, width), lambda i: (i, 0))
    full = lambda a: pl.BlockSpec(a.shape, lambda i: (0, 0))
    args = list(lhs) + ([z] if z is not None else []) + [w, res, g, b]
    specs = [row(a.shape[1]) for a in lhs] + ([row(z.shape[1])] if z is not None else []) + [
        pl.BlockSpec((None,) + w.shape[1:], lambda i: (layer, 0, 0)), row(D_MODEL), full(g), full(b)]
    return pl.pallas_call(
        functools.partial(_proj_ln_body, len(lhs), z is not None),
        grid=(m // tm,),
        in_specs=specs,
        out_specs=[row(D_MODEL), row(D_MODEL)],
        out_shape=[jax.ShapeDtypeStruct((m, D_MODEL), f32), jax.ShapeDtypeStruct((m, D_MODEL), bf16)],
        compiler_params=_params("parallel"),
        name="proj_ln",
    )(*args)


def _t5_tiles_body(t5_ref, o_ref):
    h = pl.program_id(0)
    key = lax.broadcasted_iota(i32, (MOBA_BLOCK, MOBA_BLOCK), 0)
    qry = lax.broadcasted_iota(i32, (MOBA_BLOCK, MOBA_BLOCK), 1)
    tab = lambda b: t5_ref[h, b]
    rel = qry - key
    o_ref[0, 0] = jnp.where(rel >= 0, _t5_bias_of(jnp.maximum(rel, 0), tab), NEG)
    o_ref[0, 1] = _t5_bias_of(rel + MOBA_BLOCK, tab)


def _t5_tiles(t5_t):
    return pl.pallas_call(
        _t5_tiles_body,
        grid=(A_HEADS,),
        in_specs=[pl.BlockSpec(memory_space=pltpu.SMEM)],
        out_specs=pl.BlockSpec((1, 2, MOBA_BLOCK, MOBA_BLOCK), lambda h: (h, 0, 0, 0)),
        out_shape=jax.ShapeDtypeStruct((A_HEADS, 2, MOBA_BLOCK, MOBA_BLOCK), f32),
        compiler_params=_params("parallel"),
        name="t5_tiles",
    )(t5_t)


def _moba_prompt_body(nb, t5_ref, q_ref, k_ref, v_ref, bias_ref, o_ref):
    blkw = MOBA_BLOCK
    far_bias = t5_ref[pl.program_id(1), T5_BUCKETS - 1]
    k = k_ref[...]
    q = q_ref[...]
    kb = k.astype(bf16)
    vt = v_ref[...].T.astype(bf16)
    km = jnp.concatenate([jnp.mean(k[n * blkw:(n + 1) * blkw], axis=0, keepdims=True) for n in range(nb)], axis=0)
    gate_all = _dot3(km, q, NT)
    qs = (q * (1.0 / math.sqrt(A_HEAD_DIM))).astype(bf16)
    blk = lax.broadcasted_iota(i32, (nb, blkw), 0)

    def scores(qi):
        rows = slice(qi * blkw, (qi + 1) * blkw)
        gate = gate_all[:, rows]
        beaten = jnp.zeros(gate.shape, f32)
        for m in range(qi):
            gm = gate[m:m + 1, :]
            beaten = beaten + jnp.where((gm > gate) | ((gm == gate) & (m < blk)), 1.0, 0.0)
        sel = jnp.where((blk < qi) & (beaten < MOBA_TOPK), 0.0, NEG)
        s_all = _dot(kb[:(qi + 1) * blkw], qs[rows], NT)
        tiles = []
        for n in range(qi + 1):
            s = s_all[n * blkw:(n + 1) * blkw]
            if n == qi:
                s = s + bias_ref[0, 0]
            elif n == qi - 1:
                s = s + sel[n:n + 1, :] + bias_ref[0, 1]
            else:
                s = s + (sel[n:n + 1, :] + far_bias)
            tiles.append(s)
        return tiles

    def attend(qi, tiles):
        mx = functools.reduce(jnp.maximum, [jnp.max(s, axis=0, keepdims=True) for s in tiles])
        ps = [jnp.exp(s - mx) for s in tiles]
        den = functools.reduce(jnp.add, [jnp.sum(p, axis=0, keepdims=True) for p in ps])
        pcat = jnp.concatenate([p.astype(bf16) for p in ps], axis=0)
        acc = _dot(vt[:, :(qi + 1) * blkw], pcat)
        o_ref[qi * blkw:(qi + 1) * blkw, :] = (acc / den).T

    pending = scores(0)
    for qi in range(nb):
        ahead = scores(qi + 1) if qi + 1 < nb else None
        attend(qi, pending)
        pending = ahead


def _moba_prompt(qkv, bias_tiles, t5_t, batch, seq):
    assert 2 * MOBA_BLOCK - (MOBA_BLOCK - 1) >= T5_THR[-1]
    nb = seq // MOBA_BLOCK
    spec = lambda part: pl.BlockSpec((seq, A_HEAD_DIM), lambda b, h: (b, part * A_HEADS + h))
    return pl.pallas_call(
        functools.partial(_moba_prompt_body, nb),
        grid=(batch, A_HEADS),
        in_specs=[pl.BlockSpec(memory_space=pltpu.SMEM), spec(0), spec(1), spec(2),
                  pl.BlockSpec((1, 2, MOBA_BLOCK, MOBA_BLOCK), lambda b, h: (h, 0, 0, 0))],
        out_specs=spec(0),
        out_shape=jax.ShapeDtypeStruct((batch * seq, A_W), f32),
        compiler_params=_params("parallel", "parallel"),
        name="moba_prompt",
    )(t5_t, qkv, qkv, qkv, bias_tiles)


PAGES_PER_SEQ = PAST_LEN // PAGE_SIZE
PAST_BLOCKS = PAST_LEN // MOBA_BLOCK
PAGES_PER_BLOCK = MOBA_BLOCK // PAGE_SIZE
KMEAN_BLOCKS_PER_STEP = SUBLANES


def _kmean_body(pt_ref, *refs):
    pages, o_ref = refs[:-1], refs[-1]
    for jj in range(KMEAN_BLOCKS_PER_STEP):
        s = functools.reduce(jnp.add, [jnp.sum(pages[PAGES_PER_BLOCK * jj + i][0, 0], axis=0)
                                       for i in range(PAGES_PER_BLOCK)])
        s = s * (1.0 / MOBA_BLOCK)
        for h in range(A_HEADS):
            o_ref[0, h, jj:jj + 1, :] = s[h:h + 1, :]


def _moba_kmean(pt, cache_k, layer, batch):
    per_step = PAGES_PER_BLOCK * KMEAN_BLOCKS_PER_STEP
    page = lambda i: pl.BlockSpec((1, 1, PAGE_SIZE, A_HEADS, A_HEAD_DIM),
                                  lambda b, g, pt: (layer, pt[b * PAGES_PER_SEQ + g * per_step + i], 0, 0, 0))
    return pl.pallas_call(
        _kmean_body,
        grid_spec=pltpu.PrefetchScalarGridSpec(
            num_scalar_prefetch=1, grid=(batch, PAST_BLOCKS // KMEAN_BLOCKS_PER_STEP),
            in_specs=[page(i) for i in range(per_step)],
            out_specs=pl.BlockSpec((1, A_HEADS, KMEAN_BLOCKS_PER_STEP, A_HEAD_DIM), lambda b, g, pt: (b, 0, g, 0))),
        out_shape=jax.ShapeDtypeStruct((batch, A_HEADS, PAST_BLOCKS, A_HEAD_DIM), f32),
        compiler_params=_params("parallel", "arbitrary"),
        name="moba_kmean",
    )(pt, *([cache_k] * per_step))


def _moba_topk_body(q_ref, km_ref, o_ref):
    lane = lax.broadcasted_iota(i32, (SUBLANES, LANES), 1)
    for h in range(A_HEADS):
        cols = slice(h * A_HEAD_DIM, (h + 1) * A_HEAD_DIM)
        gate = _dot3(q_ref[0, :, cols], km_ref[0, h], NT)
        blk = lax.broadcasted_iota(i32, gate.shape, 1)
        out = jnp.zeros((SUBLANES, LANES), i32)
        for j in range(MOBA_TOPK):
            best = jnp.max(gate, axis=1, keepdims=True)
            idx = jnp.min(jnp.where(gate == best, blk, PAST_BLOCKS), axis=1, keepdims=True)
            out = jnp.where(lane == j, idx, out)
            gate = jnp.where(blk == idx, -jnp.inf, gate)
        o_ref[0, h * SUBLANES:(h + 1) * SUBLANES, :] = out


def _moba_topk(q8, kmean, batch):
    return pl.pallas_call(
        _moba_topk_body,
        grid=(batch,),
        in_specs=[pl.BlockSpec((1, SUBLANES, A_W), lambda b: (b, 0, 0)),
                  pl.BlockSpec((1, A_HEADS, PAST_BLOCKS, A_HEAD_DIM), lambda b: (b, 0, 0, 0))],
        out_specs=pl.BlockSpec((1, A_HEADS * SUBLANES, LANES), lambda b: (b, 0, 0)),
        out_shape=jax.ShapeDtypeStruct((batch, A_HEADS * SUBLANES, LANES), i32),
        compiler_params=_params("parallel"),
        name="moba_topk",
    )(q8, kmean)


def _moba_sample_body(t_new, layer, pt_ref, top_ref, t5_ref, q_ref, kn_ref, vn_ref, kc_hbm, vc_hbm, o_ref,
                      kbuf, vbuf, sem):
    b, h = pl.program_id(0), pl.program_id(1)
    nh = pl.num_programs(1)
    step = b * nh + h
    nstep = pl.num_programs(0) * nh
    per_q = MOBA_TOPK * PAGES_PER_BLOCK
    top_at = lambda bb, hh, qq, j: top_ref[((bb * A_HEADS + hh) * t_new + qq) * MOBA_TOPK + j]

    def copies(bb, hh, slot):
        out = []
        for qq in range(t_new):
            for j in range(MOBA_TOPK):
                blk = top_at(bb, hh, qq, j)
                for half in range(PAGES_PER_BLOCK):
                    pg = pt_ref[bb * PAGES_PER_SEQ + PAGES_PER_BLOCK * blk + half]
                    i = qq * per_q + j * PAGES_PER_BLOCK + half
                    out.append(pltpu.make_async_copy(kc_hbm.at[layer, pg, :, hh, :], kbuf.at[slot, i], sem.at[slot, 0]))
                    out.append(pltpu.make_async_copy(vc_hbm.at[layer, pg, :, hh, :], vbuf.at[slot, i], sem.at[slot, 1]))
        return out

    slot = step % 2

    @pl.when(step == 0)
    def _():
        for c in copies(b, h, slot):
            c.start()

    @pl.when(step + 1 < nstep)
    def _():
        nxt = step + 1
        for c in copies(nxt // nh, nxt % nh, 1 - slot):
            c.start()

    tab = lambda bkt: t5_ref[h, bkt]
    q = q_ref[0] * (1.0 / math.sqrt(A_HEAD_DIM))
    qb = q.astype(bf16)
    row = lax.broadcasted_iota(i32, (SUBLANES, 1), 0)
    kn, vn = kn_ref[0], vn_ref[0]
    own = []
    for t in range(t_new):
        sc = jnp.sum(q * kn[t:t + 1, :], axis=1, keepdims=True) + _t5_bias_of(jnp.maximum(row - t, 0), tab)
        own.append(jnp.where(row >= t, sc, NEG))
    own_max = functools.reduce(jnp.maximum, own)

    for c in copies(b, h, slot):
        c.wait()

    lane = lax.broadcasted_iota(i32, (1, per_q * PAGE_SIZE), 1)
    l_sum = jnp.zeros((SUBLANES, 1), f32)
    acc = jnp.zeros((SUBLANES, A_HEAD_DIM), f32)
    for qq in range(t_new):
        kq = kbuf[slot, qq * per_q:(qq + 1) * per_q].reshape(per_q * PAGE_SIZE, A_HEAD_DIM)
        vq = vbuf[slot, qq * per_q:(qq + 1) * per_q].reshape(per_q * PAGE_SIZE, A_HEAD_DIM)
        keypos = lane - (MOBA_TOPK - 1) * MOBA_BLOCK + top_at(b, h, qq, MOBA_TOPK - 1) * MOBA_BLOCK
        for j in range(MOBA_TOPK - 2, -1, -1):
            keypos = jnp.where(lane < (j + 1) * MOBA_BLOCK, lane - j * MOBA_BLOCK + top_at(b, h, qq, j) * MOBA_BLOCK,
                               keypos)
        lg = _dot(qb, kq.astype(bf16), NT) + _t5_bias_of(PAST_LEN + qq - keypos, tab)
        mine = row == qq
        m = jnp.maximum(own_max, jnp.max(lg, axis=1, keepdims=True))
        p = jnp.where(mine, jnp.exp(lg - m), 0.0)
        l_sum = l_sum + jnp.sum(p, axis=1, keepdims=True)
        acc = acc + _dot(p.astype(bf16), vq.astype(bf16))
        for t in range(t_new):
            po = jnp.where(mine, jnp.exp(own[t] - m), 0.0)
            l_sum = l_sum + po
            acc = acc + po * vn[t:t + 1, :]
    o_ref[0] = acc / jnp.where(row < t_new, l_sum, 1.0)


def _moba_sample(pt, top, t5_t, q8, kn8, vn8, cache_k, cache_v, layer, batch, t_new):
    npage = t_new * MOBA_TOPK * PAGES_PER_BLOCK
    rows = pl.BlockSpec((1, SUBLANES, A_HEAD_DIM), lambda b, h, pt, top: (b, 0, h))
    hbm = pl.BlockSpec(memory_space=pl.ANY)
    return pl.pallas_call(
        functools.partial(_moba_sample_body, t_new, layer),
        grid_spec=pltpu.PrefetchScalarGridSpec(
            num_scalar_prefetch=2, grid=(batch, A_HEADS),
            in_specs=[pl.BlockSpec(memory_space=pltpu.SMEM), rows, rows, rows, hbm, hbm],
            out_specs=rows,
            scratch_shapes=[pltpu.VMEM((2, npage, PAGE_SIZE, A_HEAD_DIM), f32),
                            pltpu.VMEM((2, npage, PAGE_SIZE, A_HEAD_DIM), f32),
                            pltpu.SemaphoreType.DMA((2, 2))]),
        out_shape=jax.ShapeDtypeStruct((batch, SUBLANES, A_W), f32),
        compiler_params=_params("arbitrary", "arbitrary"),
        name="moba_sample",
    )(pt, top, t5_t, q8, kn8, vn8, cache_k, cache_v)


def _rwkv_prep_body(t_valid, t_total, xb_ref, halo_ref, sp_ref, mu_ref, w0_ref, a0_ref, kk_ref, ka_ref, lora_ref,
                    r_ref, l_ref, a_ref, b_ref, k_ref, v_ref):
    i = pl.program_id(1)
    x = xb_ref[0]
    tt = x.shape[0]
    row = lax.broadcasted_iota(i32, (tt, 1), 0)
    first = jnp.where(i == 0, sp_ref[0], halo_ref[0, SUBLANES - 1:SUBLANES, :])
    prev = jnp.where(row == 0, first, pltpu.roll(x, 1, 0))
    xm = x + (prev - x) * mu_ref[...]
    r, k, v = xm[:, :B_W], xm[:, B_W:2 * B_W], xm[:, 2 * B_W:3 * B_W]
    wa = xm[:, 3 * B_W:]
    lane = lax.broadcasted_iota(i32, wa.shape, 1)
    lw = _dot3(jnp.where(lane < LORA_W, jnp.tanh(wa), 0.0), lora_ref[...])
    la = _dot3(jnp.where(lane < LORA_W, 0.0, wa), lora_ref[...])
    wlog = -math.exp(-0.5) * jax.nn.sigmoid(w0_ref[...] + lw)
    a = jax.nn.sigmoid(a0_ref[...] + la)
    kk = k * kk_ref[...]
    sq = kk * kk
    ss = jnp.concatenate(_head_reduce([sq[:, p * GROUP:(p + 1) * GROUP] for p in range(B_W // GROUP)],
                                      jnp.where(_same_head(), 1.0, 0.0).astype(bf16)), axis=1)
    kk = kk * lax.rsqrt(jnp.maximum(ss, 1e-24))
    kp = k * (1.0 + (a - 1.0) * ka_ref[...])
    outs = (r, wlog, -kk, kk * a, kp, v)
    if t_valid < t_total:
        ok = (i * tt + row) < t_valid
        outs = tuple(jnp.where(ok, o, 0.0) for o in outs)
    for ref, o in zip((r_ref, l_ref, a_ref, b_ref, k_ref, v_ref), outs):
        ref[0] = o


def _rwkv_prep(xb3, shift_prev, t_valid, mu, w0, a0, k_k, k_a, lora):
    bn, tp, _ = xb3.shape
    tt = min(tp, 256)
    halo_blocks = tt // SUBLANES
    tile = lambda width: pl.BlockSpec((1, tt, width), lambda b, i: (b, i, 0))
    vec = lambda a: pl.BlockSpec(a.shape, lambda b, i: (0,) * a.ndim)
    out = jax.ShapeDtypeStruct((bn, tp, B_W), f32)
    return pl.pallas_call(
        functools.partial(_rwkv_prep_body, t_valid, tp),
        grid=(bn, tp // tt),
        in_specs=[tile(B_COLS),
                  pl.BlockSpec((1, SUBLANES, B_COLS), lambda b, i: (b, jnp.maximum(i * halo_blocks - 1, 0), 0)),
                  pl.BlockSpec((1, 1, B_COLS), lambda b, i: (b, 0, 0)),
                  vec(mu), vec(w0), vec(a0), vec(k_k), vec(k_a), vec(lora)],
        out_specs=[tile(B_W)] * 6,
        out_shape=[out] * 6,
        compiler_params=_params("parallel", "arbitrary"),
        name="rwkv_prep",
    )(xb3, xb3, shift_prev, mu, w0, a0, k_k, k_a, lora)


def _head_of(idx):
    return idx // B_HEAD_DIM


def _same_head():
    return (_head_of(lax.broadcasted_iota(i32, (GROUP, GROUP), 0))
            == _head_of(lax.broadcasted_iota(i32, (GROUP, GROUP), 1)))


def _stack(p, head):
    return jnp.concatenate([jnp.where(head == h, p, 0.0) for h in range(GROUP_HEADS)], axis=0)


def _head_reduce(parts, mat):
    n, c = len(parts), parts[0].shape[0]
    halves = [_split(a) for a in parts]
    out = _dot(jnp.concatenate([h for h, _ in halves] + [l for _, l in halves], axis=0), mat)
    return [out[i * c:(i + 1) * c] + out[(n + i) * c:(n + i + 1) * c] for i in range(n)]


def _rwkv_scan_body(npair, r_ref, l_ref, a_ref, b_ref, k_ref, v_ref, s0_ref, rk_ref, gg_ref, gb_ref, y_ref, s_ref, bd_ref):
    hd = B_HEAD_DIM
    heads = range(GROUP_HEADS)
    gpb = B_HEADS // GROUP_HEADS
    nb = npair // gpb

    @pl.when(pl.program_id(1) == 0)
    def _():
        zero = jnp.zeros((hd, hd), f32)
        for p in range(npair):
            bb, gi = divmod(p, gpb)
            bd_ref[p] = jnp.concatenate(
                [jnp.concatenate([s0_ref[bb, GROUP_HEADS * gi + h].T if h == g else zero for g in heads], axis=1)
                 for h in heads], axis=0)

    c = RWKV_CHUNK
    pairs = range(npair)
    col = lambda x, p: x[:, p * GROUP:(p + 1) * GROUP]
    lane = lax.broadcasted_iota(i32, (c, GROUP), 1)
    t = lax.broadcasted_iota(i32, (c, GROUP), 0)
    j = lane & (B_HEAD_DIM - 1)
    strict, incl = t > j, t >= j
    eye_c = jnp.where(t == j, 1.0, 0.0)
    rowi = lax.broadcasted_iota(i32, (GROUP, GROUP), 0)
    coli = lax.broadcasted_iota(i32, (GROUP, GROUP), 1)
    same_head = _same_head()
    head_sum = jnp.where(same_head, 1.0, 0.0).astype(bf16)
    head_mean = jnp.where(same_head, 1.0 / B_HEAD_DIM, 0.0).astype(bf16)
    head_lane = _head_of(lane)
    st = lambda x: _stack(x, head_lane).astype(bf16)
    cast = lambda x: x.astype(bf16)

    R, L, A, B, K, V = (jnp.concatenate([ref[bb] for bb in range(nb)], axis=1)
                        for ref in (r_ref, l_ref, a_ref, b_ref, k_ref, v_ref))
    rk, gg, gb = (jnp.concatenate([ref[...]] * nb, axis=1) for ref in (rk_ref, gg_ref, gb_ref))
    tri = jnp.where(lax.broadcasted_iota(i32, (c, c), 0) >= lax.broadcasted_iota(i32, (c, c), 1), 1.0, 0.0).astype(bf16)
    l1, l2 = _split(L)
    l3 = (L - l1.astype(f32) - l2.astype(f32)).astype(bf16)
    cum = _dot(tri, l1) + _dot(tri, l2) + _dot(tri, l3)
    last = cum[c - 1:c, :]
    wc = jnp.exp(last)
    At, Rt = A * jnp.exp(cum - L), R * jnp.exp(cum)
    einv, ew = jnp.exp(-cum), jnp.exp(last - cum)
    Bt, Kt, Bw, Kw = B * einv, K * einv, B * ew, K * ew

    ar = [cast(jnp.concatenate([col(At, p), col(Rt, p)], axis=0)) for p in pairs]
    pb = [_dot(ar[p], st(col(Bt, p)), NT) for p in pairs]
    pk = [_dot(ar[p], st(col(Kt, p)), NT) for p in pairs]
    pab = [jnp.where(strict, x[:c], 0.0) for x in pb]
    prb = [jnp.where(incl, x[c:], 0.0) for x in pb]
    pak = [jnp.where(strict, x[:c], 0.0) for x in pk]
    prk = [jnp.where(incl, x[c:], 0.0) for x in pk]
    x = [eye_c + m for m in pab]
    qp = pab
    for _ in range(int(math.log2(c)) - 1):
        qp = [_dot(cast(m), st(m)) for m in qp]
        x = [xi + _dot(cast(xi), st(m)) for xi, m in zip(x, qp)]
    akv = [_dot(cast(pak[p]), st(col(V, p))) for p in pairs]
    ua = [_dot(cast(x[p]), jnp.concatenate([st(akv[p]), st(col(At, p))], axis=1)) for p in pairs]
    u0 = [m[:, :GROUP] for m in ua]
    ah = [m[:, GROUP:] for m in ua]
    ry = [_dot(cast(prb[p]), jnp.concatenate([st(ah[p]), st(u0[p])], axis=1)) for p in pairs]
    rh = [col(Rt, p) + ry[p][:, :GROUP] for p in pairs]
    y0 = [ry[p][:, GROUP:] + _dot(cast(prk[p]), st(col(V, p))) for p in pairs]
    z1 = [_dot(cast(col(Bw, p)), cast(jnp.concatenate([ah[p], u0[p]], axis=1)), TN) for p in pairs]
    z2 = [_dot(cast(col(Kw, p)), cast(col(V, p)), TN) for p in pairs]
    mt = [jnp.where(rowi == coli, col(wc, p), 0.0) + jnp.where(same_head, z1[p][:, :GROUP], 0.0) for p in pairs]
    gt = [jnp.where(same_head, z1[p][:, GROUP:] + z2[p], 0.0) for p in pairs]
    nl = c + GROUP
    lhs = [_split(jnp.concatenate([rh[p], mt[p]], axis=0)) for p in pairs]
    bd = [_split(bd_ref[p]) for p in pairs]
    main = [_dot(jnp.concatenate(lhs[p], axis=0), bd[p][0]) for p in pairs]
    corr = [_dot(lhs[p][0], bd[p][1]) for p in pairs]
    prod = [main[p][:nl] + main[p][nl:] + corr[p] for p in pairs]
    new_bd = [prod[p][c:] + gt[p] for p in pairs]
    for p in pairs:
        bd_ref[p] = new_bd[p]

    @pl.when(pl.program_id(1) == pl.num_programs(1) - 1)
    def _():
        for p in pairs:
            for h in heads:
                s_ref[p // gpb, GROUP_HEADS * (p % gpb) + h] = new_bd[p][h * hd:(h + 1) * hd, h * hd:(h + 1) * hd].T

    y = [prod[p][:c] + y0[p] for p in pairs]
    mean = _head_reduce(y, head_mean)
    d = [yi - m for yi, m in zip(y, mean)]
    var = _head_reduce([di * di for di in d], head_mean)
    bonus = _head_reduce([col(R, p) * col(K, p) * col(rk, p) for p in pairs], head_sum)
    for p in pairs:
        yn = d[p] * lax.rsqrt(var[p] + RWKV_GN_EPS) * col(gg, p) + col(gb, p)
        y_ref[p // gpb, :, (p % gpb) * GROUP:(p % gpb + 1) * GROUP] = yn + bonus[p] * col(V, p)


SCAN_BATCHES = 4


def _rwkv_scan(prep, s0, r_k, gn_g, gn_b):
    bn, tp, _ = prep[0].shape
    nb = SCAN_BATCHES
    assert bn % nb == 0
    npair = nb * B_HEADS // GROUP_HEADS
    tile = pl.BlockSpec((nb, RWKV_CHUNK, B_W), lambda b, c: (b, c, 0))
    state = pl.BlockSpec((nb, B_HEADS, B_HEAD_DIM, B_HEAD_DIM), lambda b, c: (b, 0, 0, 0))
    vec = pl.BlockSpec((1, B_W), lambda b, c: (0, 0))
    return pl.pallas_call(
        functools.partial(_rwkv_scan_body, npair),
        grid=(bn // nb, tp // RWKV_CHUNK),
        in_specs=[tile] * 6 + [state, vec, vec, vec],
        out_specs=[tile, state],
        out_shape=[jax.ShapeDtypeStruct((bn, tp, B_W), f32), jax.ShapeDtypeStruct(s0.shape, f32)],
        scratch_shapes=[pltpu.VMEM((npair, GROUP, GROUP), f32)],
        compiler_params=_params("parallel", "arbitrary"),
        name="rwkv_scan",
    )(*prep, s0, r_k, gn_g, gn_b)


def _pool_body(pos0, x_ref, halo_ref, buf_ref, pw_ref, sc_ref, o_ref):
    i = pl.program_id(1)
    x = x_ref[0]
    tt = x.shape[0]
    halo = jnp.where(i == 0, buf_ref[0], halo_ref[0])
    xe = jnp.concatenate([halo, x], axis=0)
    row = lax.broadcasted_iota(i32, (tt, 1), 0)
    pos = pos0 + i * tt + row
    outs = []
    for g, w in enumerate(POOL_WINDOWS):
        cols = slice(g * POOL_GW, (g + 1) * POOL_GW)
        s = xe[:, cols]
        sh = 1
        while sh < w:
            s = s + pltpu.roll(s, sh, 0)
            sh *= 2
        cnt = jnp.minimum(w, pos + 1).astype(f32)
        pooled = s[POOL_BUF + 1:] / cnt - x[:, cols]
        outs.append(_dot(pooled.astype(bf16), pw_ref[g]))
    o_ref[0] = jnp.concatenate(outs, axis=1) * sc_ref[...]


def _pool(xc3, buf16, pos0, pool_w, layer, scale):
    bn, tp, _ = xc3.shape
    tt = min(tp, 256)
    hb = POOL_BUF + 1
    return pl.pallas_call(
        functools.partial(_pool_body, pos0),
        grid=(bn, tp // tt),
        in_specs=[pl.BlockSpec((1, tt, C_W), lambda b, i: (b, i, 0)),
                  pl.BlockSpec((1, hb, C_W), lambda b, i: (b, jnp.maximum(i * (tt // hb) - 1, 0), 0)),
                  pl.BlockSpec((1, hb, C_W), lambda b, i: (b, 0, 0)),
                  pl.BlockSpec((None,) + pool_w.shape[1:], lambda b, i: (layer, 0, 0, 0)),
                  pl.BlockSpec(scale.shape, lambda b, i: (0, 0))],
        out_specs=pl.BlockSpec((1, tt, C_W), lambda b, i: (b, i, 0)),
        out_shape=jax.ShapeDtypeStruct((bn, tp, C_W), f32),
        compiler_params=_params("parallel", "arbitrary"),
        name="pool_mix",
    )(xc3, xc3, buf16, pool_w, scale)


MEM_PROJ_HEADS = 2


def _mem_proj_body(layer, batch, x_ref, w_ref, prev_hbm, o_hbm, ybuf, sem):
    del prev_hbm
    y = _dot(x_ref[...], w_ref[0].astype(bf16))
    for hh in range(MEM_PROJ_HEADS):
        ybuf[hh] = y[:, hh * X_HEAD_DIM:(hh + 1) * X_HEAD_DIM]
    head0 = pl.program_id(0) * MEM_PROJ_HEADS
    copies = [pltpu.make_async_copy(ybuf.at[hh, b * MEM_LEN:(b + 1) * MEM_LEN], o_hbm.at[layer, b, :, head0 + hh, :],
                                    sem.at[hh, b])
              for hh in range(MEM_PROJ_HEADS) for b in range(batch)]
    for c in copies:
        c.start()
    for c in copies:
        c.wait()


def _mem_proj(x, w, layer, prev, batch):
    tn = MEM_PROJ_HEADS * X_HEAD_DIM
    return pl.pallas_call(
        functools.partial(_mem_proj_body, layer, batch),
        grid=(X_HEADS // MEM_PROJ_HEADS,),
        in_specs=[pl.BlockSpec(x.shape, lambda j: (0, 0)),
                  pl.BlockSpec((1, w.shape[1], tn), lambda j: (layer, 0, j)),
                  pl.BlockSpec(memory_space=pl.ANY)],
        out_specs=pl.BlockSpec(memory_space=pl.ANY),
        out_shape=jax.ShapeDtypeStruct(prev.shape, f32),
        scratch_shapes=[pltpu.VMEM((MEM_PROJ_HEADS, x.shape[0], X_HEAD_DIM), f32),
                        pltpu.SemaphoreType.DMA((MEM_PROJ_HEADS, batch))],
        input_output_aliases={2: 0},
        compiler_params=_params("arbitrary"),
        name="mem_proj",
    )(x, w, prev)


XATTN_SUB_ROWS = 256


def _xattn_block_body(layer, nbt, x_ref, res_ref, wq_hbm, wo_hbm, k_hbm, v_hbm, g_ref, b_ref, o_ref, ob_ref,
                      wq, wo, kstage, vstage, kb, vb, wsem, sem):
    bg, i = pl.program_id(0), pl.program_id(1)

    @pl.when((bg == 0) & (i == 0))
    def _():
        copies = [pltpu.make_async_copy(wq_hbm.at[layer], wq, wsem.at[0]),
                  pltpu.make_async_copy(wo_hbm.at[layer], wo, wsem.at[1])]
        for c in copies:
            c.start()
        for c in copies:
            c.wait()

    @pl.when(i == 0)
    def _():
        for j in range(nbt):
            copies = [pltpu.make_async_copy(src.at[layer, bg * nbt + j, :, h, :], dst.at[h], sem.at[n, h])
                      for n, (src, dst) in enumerate(((k_hbm, kstage), (v_hbm, vstage))) for h in range(X_HEADS)]
            for c in copies:
                c.start()
            for c in copies:
                c.wait()
            kb[j] = kstage[...].astype(bf16)
            vb[j] = vstage[...].astype(bf16)

    q = _dot(x_ref[...], wq[...])
    rows = q.shape[0] // nbt
    per_batch = []
    for j in range(nbt):
        heads = []
        for h in range(X_HEADS):
            qh = q[j * rows:(j + 1) * rows, h * X_HEAD_DIM:(h + 1) * X_HEAD_DIM]
            s = _dot((qh * (1.0 / math.sqrt(X_HEAD_DIM))).astype(bf16), kb[j, h], NT)
            p = jnp.exp(s - jnp.max(s, axis=-1, keepdims=True))
            heads.append((_dot(p.astype(bf16), vb[j, h]) / jnp.sum(p, axis=-1, keepdims=True)).astype(bf16))
        per_batch.append(jnp.concatenate(heads, axis=1))
    att = per_batch[0] if nbt == 1 else jnp.concatenate(per_batch, axis=0)
    u = ALPHA * res_ref[...] + _dot(att, wo[...])
    mu = jnp.mean(u, axis=-1, keepdims=True)
    d = u - mu
    var = jnp.mean(d * d, axis=-1, keepdims=True)
    out = d * lax.rsqrt(var + LN_EPS) * g_ref[...] + b_ref[...]
    o_ref[...] = out
    ob_ref[...] = out.astype(bf16)


def _xattn_block(xb16, res, w_q, w_o, mk, mv, layer, g, b, batch, tq):
    m = res.shape[0]
    nt = m // batch // tq
    kv_bytes = 2 * X_HEADS * MEM_LEN * X_HEAD_DIM * 2
    nbt = max(n for n in range(1, batch + 1) if batch % n == 0 and n * kv_bytes <= VMEM_LIMIT // 4) if nt == 1 else 1
    row = pl.BlockSpec((nbt * tq, D_MODEL), lambda bg, i: (bg * nt + i, 0))
    vec = pl.BlockSpec((1, D_MODEL), lambda bg, i: (0, 0))
    hbm = pl.BlockSpec(memory_space=pl.ANY)
    mem_tile = (X_HEADS, MEM_LEN, X_HEAD_DIM)
    return pl.pallas_call(
        functools.partial(_xattn_block_body, layer, nbt),
        grid=(batch // nbt, nt),
        in_specs=[row, row, hbm, hbm, hbm, hbm, vec, vec],
        out_specs=[row, row],
        out_shape=[jax.ShapeDtypeStruct((m, D_MODEL), f32), jax.ShapeDtypeStruct((m, D_MODEL), bf16)],
        scratch_shapes=[pltpu.VMEM((D_MODEL, D_MODEL), bf16), pltpu.VMEM((D_MODEL, D_MODEL), bf16),
                        pltpu.VMEM(mem_tile, f32), pltpu.VMEM(mem_tile, f32),
                        pltpu.VMEM((nbt,) + mem_tile, bf16), pltpu.VMEM((nbt,) + mem_tile, bf16),
                        pltpu.SemaphoreType.DMA((2,)), pltpu.SemaphoreType.DMA((2, X_HEADS))],
        compiler_params=_params("arbitrary", "arbitrary"),
        name="xattn_block",
    )(xb16, res, w_q, w_o, mk, mv, g, b)


def _pad_rows(x3, rows):
    return jnp.pad(x3, ((0, 0), (0, rows - x3.shape[1]), (0, 0)))


def kernel(x_prompt, x_sample, cache_moba_k, cache_moba_v, page_table, state_rwkv, state_shift, state_pool, cache_mem_k, cache_mem_v, mem_prompt, w_in_even, w_out_even, rwkv_mu, rwkv_w0, rwkv_w_up, rwkv_a0, rwkv_a_up, rwkv_k_k, rwkv_k_a, rwkv_r_k, rwkv_gn_g, rwkv_gn_b, t5_bias, w_in_odd, pool_w, pool_scale, w_out_odd, xattn_w_q, xattn_w_k, xattn_w_v, xattn_w_o, ln_mix_g, ln_mix_b, ln_x_g, ln_x_b):
    bp, tp, _ = x_prompt.shape
    bs, ts, _ = x_sample.shape
    xp = x_prompt.reshape(bp * tp, D_MODEL)
    xs = x_sample.reshape(bs * ts, D_MODEL)
    mem = mem_prompt.reshape(bp * MEM_LEN, D_MODEL)
    pt = page_table.reshape(-1)
    t5_t = t5_bias.T
    bias_tiles = _t5_tiles(t5_t)
    row2 = lambda a: a.reshape(1, -1)
    cast = lambda a: a.astype(bf16)
    col_q, col_xb, col_z = 0, 3 * A_W, 3 * A_W + B_COLS
    w_even, w_odd, xw_k, xw_v = w_in_even, w_in_odd, xattn_w_k, xattn_w_v
    w_oe, w_oo, pw, xw_q, xw_o = cast(w_out_even), cast(w_out_odd), cast(pool_w), cast(xattn_w_q), cast(xattn_w_o)
    xpb, xsb, memb = cast(xp), cast(xs), cast(mem)

    kp_l, vp_l, sp_l, shp_l, poolp_l = [], [], [], [], []
    mem_k = mem_v = jnp.zeros((DEPTH, bp, MEM_LEN, X_HEADS, X_HEAD_DIM), f32)
    ks_l, vs_l, ss_l, shs_l, pools_l = [], [], [], [], []
    for l in range(DEPTH):
        if l % 2 == 0:
            e = l // 2
            in_proj = lambda x: (_matmul(x, w_even, e, col_q, 3 * A_W), _matmul(x, w_even, e, col_xb, B_COLS, tn=640),
                                 _matmul(x, w_even, e, col_z, D_MODEL))
            split3 = lambda a: (a[:, :A_W], a[:, A_W:2 * A_W], a[:, 2 * A_W:])
            lora = jnp.concatenate([rwkv_w_up[e], rwkv_a_up[e]], axis=0)
            rw = (row2(rwkv_mu[e]), row2(rwkv_w0[e]), row2(rwkv_a0[e]), row2(rwkv_k_k[e]), row2(rwkv_k_a[e]),
                  lora)
            gn = (row2(rwkv_r_k[e]), row2(rwkv_gn_g[e]), row2(rwkv_gn_b[e]))

            def rwkv(xb, batch, t_real, t_pad, shift_prev, s0):
                xb3 = xb.reshape(batch, t_real, B_COLS)
                if t_pad != t_real:
                    xb3 = _pad_rows(xb3, t_pad)
                prep = _rwkv_prep(xb3, shift_prev, t_real, *rw)
                y, s_new = _rwkv_scan(prep, s0, *gn)
                return y[:, :t_real].reshape(batch * t_real, B_W), xb3[:, t_real - 1], s_new

            qkv, xb, z = in_proj(xpb)
            _, k, v = split3(qkv)
            a_out = _moba_prompt(qkv, bias_tiles, t5_t, bp, tp)
            b_out, shp, s_p = rwkv(xb, bp, tp, tp, jnp.zeros((bp, 1, B_COLS), f32),
                                   jnp.zeros((bp, B_HEADS, B_HEAD_DIM, B_HEAD_DIM), f32))
            mp_args = ([a_out, b_out], z, w_oe, e)
            kp_l.append(k.reshape(bp, tp, A_HEADS, A_HEAD_DIM))
            vp_l.append(v.reshape(bp, tp, A_HEADS, A_HEAD_DIM))
            sp_l.append(s_p)
            shp_l.append(shp)

            qkv, xb, z = in_proj(xsb)
            q, k, v = split3(qkv)
            pad8 = lambda a: _pad_rows(a.reshape(bs, ts, A_W), SUBLANES)
            q8, k8, v8 = pad8(q), pad8(k), pad8(v)
            top = _moba_topk(q8, _moba_kmean(pt, cache_moba_k, e, bs), bs)
            top = top.reshape(bs, A_HEADS, SUBLANES, LANES)[:, :, :ts, :MOBA_TOPK].reshape(-1)
            a_out = _moba_sample(pt, top, t5_t, q8, k8, v8, cache_moba_k, cache_moba_v, e, bs, ts)
            a_out = a_out[:, :ts].reshape(bs * ts, A_W)
            b_out, shs, s_s = rwkv(xb, bs, ts, RWKV_CHUNK, state_shift[e].reshape(bs, 1, B_COLS), state_rwkv[e])
            ms_args = ([a_out, b_out], z, w_oe, e)
            ks_l.append(k.reshape(bs, ts, A_HEADS, A_HEAD_DIM))
            vs_l.append(v.reshape(bs, ts, A_HEADS, A_HEAD_DIM))
            ss_l.append(s_s)
            shs_l.append(shs)
        else:
            o = l // 2
            sc = row2(pool_scale[o])
            in_proj = lambda x: (_matmul(x, w_odd, o, 0, C_W), _matmul(x, w_odd, o, C_W, D_MODEL))

            xc, z = in_proj(xpb)
            xc3 = xc.reshape(bp, tp, C_W)
            y = _pool(xc3, jnp.zeros((bp, POOL_BUF + 1, C_W), f32), 0, pw, o, sc)
            mp_args = ([y.reshape(bp * tp, C_W)], z, w_oo, o)
            poolp_l.append(xc3[:, tp - POOL_BUF:])

            xc, z = in_proj(xsb)
            xc3 = xc.reshape(bs, ts, C_W)
            buf16 = jnp.pad(state_pool[o], ((0, 0), (1, 0), (0, 0)))
            y = _pool(_pad_rows(xc3, POOL_BUF + 1), buf16, PAST_LEN, pw, o, sc)[:, :ts]
            ms_args = ([y.reshape(bs * ts, C_W)], z, w_oo, o)
            pools_l.append(jnp.concatenate([state_pool[o], xc3], axis=1)[:, -POOL_BUF:])

        g, b = row2(ln_mix_g[l]), row2(ln_mix_b[l])
        xp, xpb = _proj_ln(*mp_args, xp, g, b)
        xs, xsb = _proj_ln(*ms_args, xs, g, b)

        g, b = row2(ln_x_g[l]), row2(ln_x_b[l])
        mem_k, mem_v = _mem_proj(memb, xw_k, l, mem_k, bp), _mem_proj(memb, xw_v, l, mem_v, bp)
        xp, xpb = _xattn_block(xpb, xp, xw_q, xw_o, mem_k, mem_v, l, g, b, bp, XATTN_SUB_ROWS)
        pad = lambda a: _pad_rows(a.reshape(bs, ts, D_MODEL), BF16_ROWS).reshape(bs * BF16_ROWS, D_MODEL)
        unpad = lambda a: a.reshape(bs, BF16_ROWS, D_MODEL)[:, :ts].reshape(bs * ts, D_MODEL)
        xs, xsb = map(unpad, _xattn_block(pad(xsb), pad(xs), xw_q, xw_o, cache_mem_k, cache_mem_v, l, g, b,
                                          bs, BF16_ROWS))

    return (xp.reshape(bp, tp, D_MODEL), xs.reshape(bs, ts, D_MODEL),
            jnp.stack(kp_l), jnp.stack(vp_l), jnp.stack(sp_l), jnp.stack(shp_l), jnp.stack(poolp_l),
            mem_k, mem_v,
            jnp.stack(ks_l), jnp.stack(vs_l), jnp.stack(ss_l), jnp.stack(shs_l), jnp.stack(pools_l))
```
